```python
import math
import jax
import jax.numpy as jnp
from jax import lax
import numpy as np

D_MODEL = 1024
BATCH = 8
SEQ = 2048
DEPTH = 4
DEC_BATCH = 128
DEC_SEQ = 8
PAST_LEN = 2048
PAGE_SIZE = 128

N_META = 16
EPS = 1e-6
A_HEAD_DIM = 64
A_WIDTH = D_MODEL // 4
A_HEADS = A_WIDTH // A_HEAD_DIM
A_W_LORA = 64
A_A_LORA = 64
A_G_LORA = 128
A_COLS = 3 * A_WIDTH + A_W_LORA + A_A_LORA + A_G_LORA
A_GN_EPS = 64e-5
B_WIDTH = D_MODEL // 4
B_COLS = 2 * B_WIDTH
CONV_WIDTH = 31
C_QK_DIM = 64
C_V_DIM = 2 * C_QK_DIM
C_WIDTH = D_MODEL // 2
C_HEADS = C_WIDTH // C_V_DIM
C_QK_COLS = C_HEADS * 2 * C_QK_DIM
C_COLS = 2 * C_QK_COLS + C_WIDTH
Q_BLOCK = 128
IN_COLS = A_COLS + B_COLS + C_COLS
MIX_WIDTH = A_WIDTH + B_WIDTH + C_WIDTH
D_FF = 11 * D_MODEL // 4
N_EXPERTS = 8
TOP_K = 2
D_FF_EXPERT = D_FF
N_DENSE = (DEPTH + 1) // 2
N_MOE = DEPTH // 2

kernel_name = 'hymba_rwkv7_conformer_diffattn_step'


def rms_norm(x, g):
    xf = x.astype(jnp.float32)
    y = xf * lax.rsqrt(jnp.mean(xf * xf, axis=-1, keepdims=True) + EPS)
    return (y * g.astype(jnp.float32)).astype(x.dtype)


def layer_norm(x, g, b):
    xf = x.astype(jnp.float32)
    xc = xf - jnp.mean(xf, axis=-1, keepdims=True)
    y = xc * lax.rsqrt(jnp.mean(xc * xc, axis=-1, keepdims=True) + EPS)
    return (y * g.astype(jnp.float32) + b.astype(jnp.float32)).astype(x.dtype)


def alibi_slopes(n_heads):
    start = 2.0 ** (-8.0 / n_heads)
    return jnp.asarray(start ** np.arange(1, n_heads + 1), dtype=jnp.float32)


def split_cols(z, sizes):
    out, start = [], 0
    for s in sizes:
        out.append(z[..., start:start + s])
        start += s
    return out


def _rwkv7_step(state, inp):
    r_t, w_t, k_t, v_t, kk_t, b_t = inp
    sa = jnp.einsum('nhvk,nhk->nhv', state, kk_t)
    state = (state * w_t[:, :, None, :] - sa[..., None] * b_t[:, :, None, :]
             + v_t[..., None] * k_t[:, :, None, :])
    return state, jnp.einsum('nhvk,nhk->nhv', state, r_t)


def rwkv7_time_mix(za, shift0, wkv0, lp):
    n, t, _ = za.shape
    f32 = jnp.float32
    prev = jnp.concatenate([shift0[:, None, :].astype(za.dtype), za[:, :-1]], axis=1)
    zs = (za + lp['a_mu'].astype(za.dtype) * (prev - za)).astype(f32)
    r, k, v, wd, ad, gd = split_cols(zs, (A_WIDTH, A_WIDTH, A_WIDTH, A_W_LORA, A_A_LORA, A_G_LORA))
    log_w = -jax.nn.softplus(-(lp['a_w0'] + jnp.tanh(wd) @ lp['a_w_up'])) - 0.5
    decay = jnp.exp(-jnp.exp(log_w))
    a = jax.nn.sigmoid(lp['a_a0'] + ad @ lp['a_a_up'])
    g = jax.nn.sigmoid(gd) @ lp['a_g_up']
    kk = k * lp['a_k_k']
    k = k * (1.0 + (a - 1.0) * lp['a_k_a'])
    heads = lambda u: u.reshape(n, t, A_HEADS, A_HEAD_DIM).astype(f32)
    r, k, v, kk, a, decay = (heads(u) for u in (r, k, v, kk, a, decay))
    kk = kk * lax.rsqrt(jnp.maximum(jnp.sum(kk * kk, axis=-1, keepdims=True), 1e-24))
    seq = tuple(jnp.moveaxis(u, 1, 0) for u in (r, decay, k, v, kk, kk * a))
    s_final, y = lax.scan(_rwkv7_step, wkv0.astype(f32), seq)
    y = jnp.moveaxis(y, 0, 1)
    yc = y - jnp.mean(y, axis=-1, keepdims=True)
    y = yc * lax.rsqrt(jnp.mean(yc * yc, axis=-1, keepdims=True) + A_GN_EPS)
    y = y.reshape(n, t, A_WIDTH) * lp['a_lnx_g'] + lp['a_lnx_b']
    bonus = jnp.sum(r * k * lp['a_r_k'], axis=-1, keepdims=True) * v
    y = (y + bonus.reshape(n, t, A_WIDTH)) * g
    return y.astype(za.dtype), s_final.astype(wkv0.dtype), za[:, -1]


def conformer_conv(zb, conv0, lp):
    u = zb[..., :B_WIDTH] * jax.nn.sigmoid(zb[..., B_WIDTH:])
    ext = jnp.concatenate([conv0.astype(u.dtype), u], axis=1)
    h = lax.conv_general_dilated(
        ext, lp['b_conv_w'][:, None, :].astype(u.dtype), window_strides=(1,), padding='VALID',
        dimension_numbers=('NWC', 'WIO', 'NWC'), feature_group_count=B_WIDTH)
    h = layer_norm(h + lp['b_conv_b'].astype(h.dtype), lp['b_ln_g'], lp['b_ln_b'])
    h = jax.nn.silu(h)
    y = h @ lp['b_pw_w'] + lp['b_pw_b']
    return y, ext[:, ext.shape[1] - (CONV_WIDTH - 1):]


def diff_qkv(zc, lp):
    n, t, _ = zc.shape
    q = zc[..., :C_QK_COLS].reshape(n, t, C_HEADS, 2, C_QK_DIM)
    k = zc[..., C_QK_COLS:2 * C_QK_COLS].reshape(n, t, C_HEADS, 2, C_QK_DIM)
    v = zc[..., 2 * C_QK_COLS:].reshape(n, t, C_HEADS, C_V_DIM)
    return rms_norm(q, lp['c_qn_g']), rms_norm(k, lp['c_kn_g']), v


def diff_attention_core(q, k, v, q_pos, k_pos, slopes, lam):
    s = jnp.einsum('nqhmd,nshmd->nhmqs', q, k, preferred_element_type=jnp.float32) * (C_QK_DIM ** -0.5)
    dist = (q_pos[:, None] - k_pos[None, :]).astype(jnp.float32)
    s = s - slopes[:, None, None, None] * dist
    s = jnp.where(k_pos[None, :] <= q_pos[:, None], s, -jnp.inf)
    p = jax.nn.softmax(s, axis=-1)
    attn = p[:, :, 0] - lam * p[:, :, 1]
    return jnp.einsum('nhqs,nshe->nqhe', attn.astype(v.dtype), v)


def prompt_diff_attention(q, k, v, slopes, lam):
    n, t = q.shape[0], q.shape[1]
    nb = -(-t // Q_BLOCK)
    qp = jnp.pad(q, ((0, 0), (0, nb * Q_BLOCK - t), (0, 0), (0, 0), (0, 0)))
    qb = jnp.moveaxis(qp.reshape(n, nb, Q_BLOCK, C_HEADS, 2, C_QK_DIM), 1, 0)
    pos = jnp.arange(nb * Q_BLOCK, dtype=jnp.int32).reshape(nb, Q_BLOCK)
    k_pos = jnp.arange(t, dtype=jnp.int32)
    ob = lax.map(lambda blk: diff_attention_core(blk[0], k, v, blk[1], k_pos, slopes, lam), (qb, pos))
    return jnp.moveaxis(ob, 0, 1).reshape(n, nb * Q_BLOCK, C_HEADS, C_V_DIM)[:, :t]


def diff_head_out(o, lp, lam_init):
    n, t = o.shape[0], o.shape[1]
    o = rms_norm(o, lp['c_subln_g']) * (1.0 - lam_init)
    return o.reshape(n, t, C_WIDTH)


def token_mixers(x, shift0, wkv0, conv0, attend, lp, lam_init):
    h = rms_norm(x, lp['norm1_g'])
    z = h @ lp['w_in']
    za, zb, zc = split_cols(z, (A_COLS, B_COLS, C_COLS))
    ya, wkv1, shift1 = rwkv7_time_mix(za, shift0, wkv0, lp)
    yb, conv1 = conformer_conv(zb, conv0, lp)
    q, k, v = diff_qkv(zc, lp)
    yc = diff_head_out(attend(q, k, v), lp, lam_init)
    y = jnp.concatenate([ya.astype(x.dtype), yb.astype(x.dtype), yc.astype(x.dtype)], axis=-1) @ lp['w_out']
    k_rows = k.reshape(k.shape[0], k.shape[1], C_HEADS, 2 * C_QK_DIM)
    return x + y.astype(x.dtype), k_rows, v, wkv1, shift1, conv1


def swiglu(x, wg, wu, wd):
    return (jax.nn.silu(x @ wg) * (x @ wu)) @ wd


def moe_swiglu(x, router, wg, wu, wd):
    logits = jnp.einsum('ntd,de->nte', x, router, preferred_element_type=jnp.float32)
    top_val, top_idx = lax.top_k(logits, TOP_K)
    gates = jax.nn.softmax(top_val, axis=-1)
    out = jnp.zeros(x.shape, jnp.float32)
    for e in range(N_EXPERTS):
        g_e = jnp.sum(jnp.where(top_idx == e, gates, 0.0), axis=-1)
        out = out + g_e[..., None] * swiglu(x, wg[e], wu[e], wd[e]).astype(jnp.float32)
    return out.astype(x.dtype)


def _normal(key, shape, scale):
    return scale * jax.random.normal(key, shape, jnp.float32)


def _gain(key, shape):
    return 1.0 + 0.02 * jax.random.normal(key, shape, jnp.float32)


def setup_inputs(seed: int = 0) -> dict:
    key = jax.random.key(seed)
    ks = iter(jax.random.split(key, 64))
    n_pages = PAST_LEN // PAGE_SIZE
    n_pool = (DEC_BATCH * n_pages * 5) // 4
    perm = jax.random.permutation(next(ks), n_pool)
    page_table = perm[:DEC_BATCH * n_pages].reshape(DEC_BATCH, n_pages).astype(jnp.int32)
    return {
        'x_prompt': _normal(next(ks), (BATCH, SEQ, D_MODEL), 1.0),
        'x_sample': _normal(next(ks), (DEC_BATCH, DEC_SEQ, D_MODEL), 1.0),
        'cache_k': _normal(next(ks), (DEPTH, n_pool, PAGE_SIZE, C_HEADS, 2 * C_QK_DIM), 1.0),
        'cache_v': _normal(next(ks), (DEPTH, n_pool, PAGE_SIZE, C_HEADS, C_V_DIM), 1.0),
        'page_table': page_table,
        'state_wkv': _normal(next(ks), (DEPTH, DEC_BATCH, A_HEADS, A_HEAD_DIM, A_HEAD_DIM), 0.1),
        'state_shift': _normal(next(ks), (DEPTH, DEC_BATCH, A_COLS), 1.0),
        'state_conv': _normal(next(ks), (DEPTH, DEC_BATCH, CONV_WIDTH - 1, B_WIDTH), 0.5),
        'meta_tokens': _normal(next(ks), (N_META, D_MODEL), 1.0),
        'norm1_g': _gain(next(ks), (DEPTH, D_MODEL)),
        'norm2_g': _gain(next(ks), (DEPTH, D_MODEL)),
        'w_in': _normal(next(ks), (DEPTH, D_MODEL, IN_COLS), D_MODEL ** -0.5),
        'w_out': _normal(next(ks), (DEPTH, MIX_WIDTH, D_MODEL), 0.5 * MIX_WIDTH ** -0.5),
        'a_mu': jax.random.uniform(next(ks), (DEPTH, A_COLS), jnp.float32),
        'a_w0': _normal(next(ks), (DEPTH, A_WIDTH), 0.5),
        'a_w_up': _normal(next(ks), (DEPTH, A_W_LORA, A_WIDTH), 0.1),
        'a_a0': _normal(next(ks), (DEPTH, A_WIDTH), 0.1),
        'a_a_up': _normal(next(ks), (DEPTH, A_A_LORA, A_WIDTH), 0.1),
        'a_g_up': _normal(next(ks), (DEPTH, A_G_LORA, A_WIDTH), A_G_LORA ** -0.5),
        'a_k_k': 0.85 + _normal(next(ks), (DEPTH, A_WIDTH), 0.02),
        'a_k_a': _gain(next(ks), (DEPTH, A_WIDTH)),
        'a_r_k': _normal(next(ks), (DEPTH, A_HEADS, A_HEAD_DIM), 0.1),
        'a_lnx_g': _gain(next(ks), (DEPTH, A_WIDTH)),
        'a_lnx_b': _normal(next(ks), (DEPTH, A_WIDTH), 0.02),
        'b_conv_w': _normal(next(ks), (DEPTH, CONV_WIDTH, B_WIDTH), CONV_WIDTH ** -0.5),
        'b_conv_b': _normal(next(ks), (DEPTH, B_WIDTH), 0.02),
        'b_ln_g': _gain(next(ks), (DEPTH, B_WIDTH)),
        'b_ln_b': _normal(next(ks), (DEPTH, B_WIDTH), 0.02),
        'b_pw_w': _normal(next(ks), (DEPTH, B_WIDTH, B_WIDTH), B_WIDTH ** -0.5),
        'b_pw_b': _normal(next(ks), (DEPTH, B_WIDTH), 0.02),
        'c_qn_g': _gain(next(ks), (DEPTH, C_QK_DIM)),
        'c_kn_g': _gain(next(ks), (DEPTH, C_QK_DIM)),
        'c_lam_q1': _normal(next(ks), (DEPTH, C_QK_DIM), 0.1),
        'c_lam_k1': _normal(next(ks), (DEPTH, C_QK_DIM), 0.1),
        'c_lam_q2': _normal(next(ks), (DEPTH, C_QK_DIM), 0.1),
        'c_lam_k2': _normal(next(ks), (DEPTH, C_QK_DIM), 0.1),
        'c_subln_g': _gain(next(ks), (DEPTH, C_V_DIM)),
        'ffn_w_gate': _normal(next(ks), (N_DENSE, D_MODEL, D_FF), D_MODEL ** -0.5),
        'ffn_w_up': _normal(next(ks), (N_DENSE, D_MODEL, D_FF), D_MODEL ** -0.5),
        'ffn_w_down': _normal(next(ks), (N_DENSE, D_FF, D_MODEL), 0.5 * D_FF ** -0.5),
        'moe_router': _normal(next(ks), (N_MOE, D_MODEL, N_EXPERTS), D_MODEL ** -0.5),
        'moe_w_gate': _normal(next(ks), (N_MOE, N_EXPERTS, D_MODEL, D_FF_EXPERT), D_MODEL ** -0.5),
        'moe_w_up': _normal(next(ks), (N_MOE, N_EXPERTS, D_MODEL, D_FF_EXPERT), D_MODEL ** -0.5),
        'moe_w_down': _normal(next(ks), (N_MOE, N_EXPERTS, D_FF_EXPERT, D_MODEL), 0.5 * D_FF_EXPERT ** -0.5),
    }


def reference(x_prompt, x_sample, cache_k, cache_v, page_table, state_wkv, state_shift, state_conv,
              meta_tokens, norm1_g, norm2_g, w_in, w_out,
              a_mu, a_w0, a_w_up, a_a0, a_a_up, a_g_up, a_k_k, a_k_a, a_r_k, a_lnx_g, a_lnx_b,
              b_conv_w, b_conv_b, b_ln_g, b_ln_b, b_pw_w, b_pw_b,
              c_qn_g, c_kn_g, c_lam_q1, c_lam_k1, c_lam_q2, c_lam_k2, c_subln_g,
              ffn_w_gate, ffn_w_up, ffn_w_down, moe_router, moe_w_gate, moe_w_up, moe_w_down):
    f32 = jnp.float32
    dtype = x_prompt.dtype
    b = x_prompt.shape[0]
    n_s, t_s = x_sample.shape[0], x_sample.shape[1]
    past = page_table.shape[1] * PAGE_SIZE
    slopes = alibi_slopes(C_HEADS)
    meta = jnp.broadcast_to(meta_tokens.astype(dtype)[None], (b, N_META, D_MODEL))
    xp = jnp.concatenate([meta, x_prompt], axis=1)
    xs = x_sample
    q_pos_s = past + jnp.arange(t_s, dtype=jnp.int32)
    k_pos_s = jnp.arange(past + t_s, dtype=jnp.int32)
    kp_l, vp_l, ks_l, vs_l, wp_l, ws_l, sp_l, ss_l, cp_l, cs_l = ([] for _ in range(10))
    for l in range(DEPTH):
        lp = {
            'norm1_g': norm1_g[l], 'w_in': w_in[l], 'w_out': w_out[l],
            'a_mu': a_mu[l], 'a_w0': a_w0[l], 'a_w_up': a_w_up[l], 'a_a0': a_a0[l], 'a_a_up': a_a_up[l],
            'a_g_up': a_g_up[l], 'a_k_k': a_k_k[l], 'a_k_a': a_k_a[l], 'a_r_k': a_r_k[l],
            'a_lnx_g': a_lnx_g[l], 'a_lnx_b': a_lnx_b[l],
            'b_conv_w': b_conv_w[l], 'b_conv_b': b_conv_b[l], 'b_ln_g': b_ln_g[l], 'b_ln_b': b_ln_b[l],
            'b_pw_w': b_pw_w[l], 'b_pw_b': b_pw_b[l],
            'c_qn_g': c_qn_g[l], 'c_kn_g': c_kn_g[l], 'c_subln_g': c_subln_g[l],
        }
        lam_init = 0.8 - 0.6 * math.exp(-0.3 * l)
        lam = (jnp.exp(jnp.sum(c_lam_q1[l].astype(f32) * c_lam_k1[l].astype(f32)))
               - jnp.exp(jnp.sum(c_lam_q2[l].astype(f32) * c_lam_k2[l].astype(f32))) + lam_init)

        def attend_prompt(q, k, v):
            return prompt_diff_attention(q, k, v, slopes, lam)

        def attend_sample(q, k, v):
            pk = cache_k[l][page_table].reshape(n_s, past, C_HEADS, 2, C_QK_DIM).astype(k.dtype)
            pv = cache_v[l][page_table].reshape(n_s, past, C_HEADS, C_V_DIM).astype(v.dtype)
            keys = jnp.concatenate([pk, k], axis=1)
            vals = jnp.concatenate([pv, v], axis=1)
            return diff_attention_core(q, keys, vals, q_pos_s, k_pos_s, slopes, lam)

        xp, k_p, v_p, wkv_p, shift_p, conv_p = token_mixers(
            xp, jnp.zeros((b, A_COLS), dtype), jnp.zeros((b, A_HEADS, A_HEAD_DIM, A_HEAD_DIM), dtype),
            jnp.zeros((b, CONV_WIDTH - 1, B_WIDTH), dtype), attend_prompt, lp, lam_init)
        xs, k_s, v_s, wkv_s, shift_s, conv_s = token_mixers(
            xs, state_shift[l], state_wkv[l], state_conv[l], attend_sample, lp, lam_init)

        i = l // 2
        if l % 2 == 0:
            xp = xp + swiglu(rms_norm(xp, norm2_g[l]), ffn_w_gate[i], ffn_w_up[i], ffn_w_down[i]).astype(dtype)
            xs = xs + swiglu(rms_norm(xs, norm2_g[l]), ffn_w_gate[i], ffn_w_up[i], ffn_w_down[i]).astype(xs.dtype)
        else:
            xp = xp + moe_swiglu(rms_norm(xp, norm2_g[l]), moe_router[i], moe_w_gate[i], moe_w_up[i], moe_w_down[i])
            xs = xs + moe_swiglu(rms_norm(xs, norm2_g[l]), moe_router[i], moe_w_gate[i], moe_w_up[i], moe_w_down[i])

        kp_l.append(k_p); vp_l.append(v_p); ks_l.append(k_s); vs_l.append(v_s)
        wp_l.append(wkv_p); ws_l.append(wkv_s); sp_l.append(shift_p); ss_l.append(shift_s)
        cp_l.append(conv_p); cs_l.append(conv_s)

    y_prompt = xp[:, N_META:]
    y_sample = xs
    return (y_prompt, y_sample,
            jnp.stack(kp_l), jnp.stack(vp_l), jnp.stack(ks_l), jnp.stack(vs_l),
            jnp.stack(wp_l), jnp.stack(ws_l), jnp.stack(sp_l), jnp.stack(ss_l),
            jnp.stack(cp_l), jnp.stack(cs_l))
```

```python
import functools
import math

import jax
import jax.numpy as jnp
from jax import lax
from jax.experimental import pallas as pl
from jax.experimental.pallas import tpu as pltpu

F32 = jnp.float32
BF16 = jnp.bfloat16
HIGHEST = lax.Precision.HIGHEST

D_MODEL = 1024
EPS = 1e-6
N_META = 16
A_HEAD_DIM = 64
A_WIDTH = 256
A_HEADS = 4
A_W_LORA = 64
A_A_LORA = 64
A_G_LORA = 128
A_COLS = 1024
A_GN_EPS = 64e-5
B_WIDTH = 256
B_COLS = 512
CONV_WIDTH = 31
CONV_HIST = CONV_WIDTH - 1
C_QK_DIM = 64
C_V_DIM = 128
C_WIDTH = 512
C_HEADS = 4
C_QK_COLS = 512
IN_COLS = 3072
Q_OFF = A_COLS + B_COLS
K_OFF = Q_OFF + C_QK_COLS
V_OFF = K_OFF + C_QK_COLS
D_FF = 2816
N_EXPERTS = 8
PAGE_SIZE = 128
LANES = 128
NEG_BIG = -1e30
ALIBI_SLOPES = tuple(float((2.0 ** (-8.0 / C_HEADS)) ** (h + 1)) for h in range(C_HEADS))
EXP_NEG_HALF = math.exp(-0.5)
SCAN_SEQS = 8
VMEM_LIMIT = 56 * 1024 * 1024


def _params(*sem):
    return pltpu.CompilerParams(dimension_semantics=sem, vmem_limit_bytes=VMEM_LIMIT)


def _pick_tile(n, target, mult=8):
    best = None
    for d in range(mult, min(n, target) + 1, mult):
        if n % d == 0:
            best = d
    return n if best is None else best


def _dot(a, b):
    return jnp.dot(a, b, preferred_element_type=F32)


def _dot_hi(a, b):
    return jnp.dot(a, b, preferred_element_type=F32, precision=HIGHEST)


def _dot_nt(a, b):
    return lax.dot_general(a, b, (((1,), (1,)), ((), ())), preferred_element_type=F32)


def _sigmoid(x):
    return 1.0 / (1.0 + jnp.exp(-x))


def _seg_sum(x, bd2):
    hi = x.astype(BF16)
    lo = (x - hi.astype(F32)).astype(BF16)
    return _dot(jnp.concatenate([hi, lo], axis=-1), bd2)


def _block_diag2(width, seg=64):
    r = jnp.arange(width) // seg
    bd = (r[:, None] == r[None, :]).astype(BF16)
    return jnp.concatenate([bd, bd], axis=0)


def _full(shape):
    nd = len(shape)
    return pl.BlockSpec(shape, lambda *_: (0,) * nd)


def _in_proj_kernel(x_ref, g_ref, w_ref, qg_ref, kg_ref, bd_ref, z_ref, qn_ref, kn_ref, kb_ref, vb_ref):
    x = x_ref[...]
    h = x * lax.rsqrt(jnp.mean(x * x, axis=-1, keepdims=True) + EPS) * g_ref[...]
    z = _dot(h.astype(BF16), w_ref[...])
    z_ref[...] = z
    q = z[:, Q_OFF:Q_OFF + C_QK_COLS]
    k = z[:, K_OFF:K_OFF + C_QK_COLS]
    bd = bd_ref[...]
    inv = 1.0 / C_QK_DIM
    qn = q * lax.rsqrt(_seg_sum(q * q, bd) * inv + EPS) * qg_ref[...]
    kn = k * lax.rsqrt(_seg_sum(k * k, bd) * inv + EPS) * kg_ref[...]
    qn_ref[...] = qn * (C_QK_DIM ** -0.5)
    kn_ref[...] = kn
    kb_ref[...] = kn.astype(BF16)
    vb_ref[...] = z[:, V_OFF:].astype(BF16)


def _in_proj(x, g, w_bf, qg, kg, bd512):
    rows = x.shape[0]
    tm = _pick_tile(rows, 256, 16)
    row = lambda c: pl.BlockSpec((tm, c), lambda i: (i, 0))
    return pl.pallas_call(
        _in_proj_kernel,
        grid=(rows // tm,),
        in_specs=[row(D_MODEL), _full((1, D_MODEL)), _full((D_MODEL, IN_COLS)),
                  _full((1, C_QK_COLS)), _full((1, C_QK_COLS)), _full((2 * C_QK_COLS, C_QK_COLS))],
        out_specs=[row(IN_COLS), row(C_QK_COLS), row(C_QK_COLS), row(C_QK_COLS), row(C_WIDTH)],
        out_shape=[jax.ShapeDtypeStruct((rows, IN_COLS), F32),
                   jax.ShapeDtypeStruct((rows, C_QK_COLS), F32),
                   jax.ShapeDtypeStruct((rows, C_QK_COLS), F32),
                   jax.ShapeDtypeStruct((rows, C_QK_COLS), BF16),
                   jax.ShapeDtypeStruct((rows, C_WIDTH), BF16)],
        compiler_params=_params("parallel"),
        name="in_proj",
    )(x, g, w_bf, qg, kg, bd512)


def _rwkv_pre_kernel(za_ref, prev_ref, mu_ref, w0_ref, wup_ref, a0_ref, aup_ref, gup_ref,
                     kk_ref, ka_ref, rk_ref, bd_ref, sc_ref, gb_ref):
    za = za_ref[...]
    zs = za + mu_ref[...] * (prev_ref[...] - za)
    w = A_WIDTH
    r = zs[:, 0:w]
    k = zs[:, w:2 * w]
    v = zs[:, 2 * w:3 * w]
    wd = zs[:, 3 * w:3 * w + A_W_LORA]
    ad = zs[:, 3 * w + A_W_LORA:3 * w + A_W_LORA + A_A_LORA]
    gd = zs[:, 3 * w + A_W_LORA + A_A_LORA:]
    lw = w0_ref[...] + _dot_hi(jnp.tanh(wd), wup_ref[...])
    decay = jnp.exp(-EXP_NEG_HALF * _sigmoid(lw))
    a = _sigmoid(a0_ref[...] + _dot_hi(ad, aup_ref[...]))
    g = _dot_hi(_sigmoid(gd), gup_ref[...])
    bd = bd_ref[...]
    kk = k * kk_ref[...]
    k2 = k * (1.0 + (a - 1.0) * ka_ref[...])
    kk = kk * lax.rsqrt(jnp.maximum(_seg_sum(kk * kk, bd), 1e-24))
    bonus = _seg_sum(r * k2 * rk_ref[...], bd) * v
    sc_ref[:, 0:w] = r
    sc_ref[:, w:2 * w] = decay
    sc_ref[:, 2 * w:3 * w] = k2
    sc_ref[:, 3 * w:4 * w] = v
    sc_ref[:, 4 * w:5 * w] = kk
    sc_ref[:, 5 * w:6 * w] = kk * a
    gb_ref[:, 0:w] = g
    gb_ref[:, w:2 * w] = bonus


def _rwkv_pre(z, prev, lp, bd256):
    rows = z.shape[0]
    tm = _pick_tile(rows, 512, 8)
    row = lambda c: pl.BlockSpec((tm, c), lambda i: (i, 0))
    vec = lambda a: a.reshape(1, -1)
    args = (z, prev, vec(lp['a_mu']), vec(lp['a_w0']), lp['a_w_up'], vec(lp['a_a0']), lp['a_a_up'],
            lp['a_g_up'], vec(lp['a_k_k']), vec(lp['a_k_a']), vec(lp['a_r_k']), bd256)
    in_specs = [row(A_COLS), row(A_COLS)] + [_full(a.shape) for a in args[2:]]
    return pl.pallas_call(
        _rwkv_pre_kernel,
        grid=(rows // tm,),
        in_specs=in_specs,
        out_specs=[row(6 * A_WIDTH), row(2 * A_WIDTH)],
        out_shape=[jax.ShapeDtypeStruct((rows, 6 * A_WIDTH), F32),
                   jax.ShapeDtypeStruct((rows, 2 * A_WIDTH), F32)],
        compiler_params=_params("parallel"),
        name="rwkv_pre",
    )(*args)


def _rwkv_scan_kernel(x_ref, s0_ref, bd_ref, eye_ref, y_ref, s1_ref, s_scr, *, tb):
    nseq = x_ref.shape[1]
    npair = 2 * nseq
    w = A_WIDTH

    @pl.when(pl.program_id(1) == 0)
    def _():
        s_scr[...] = s0_ref[0]

    bd = bd_ref[...]
    eye = eye_ref[...]

    def seg(a):
        return _seg_sum(a.reshape(A_HEAD_DIM * npair, LANES), bd).reshape(A_HEAD_DIM, npair, LANES)

    def step(t, carry):
        xt = x_ref[t]

        def rowvec(c):
            return jnp.concatenate([xt[:, c:c + LANES], xt[:, c + LANES:c + 2 * LANES]], axis=0)[None]
        r, dec, k, v, kk, b = (rowvec(i * w) for i in range(6))
        s = s_scr[...]
        sa = seg(s * kk)
        vb = seg(eye * v)
        s = s * dec - sa * b + vb * k
        s_scr[...] = s
        y = jnp.sum(seg(s * r) * eye, axis=0)
        y_ref[t] = jnp.concatenate([y[0:nseq], y[nseq:npair]], axis=-1)
        return carry

    lax.fori_loop(0, tb, step, 0)

    @pl.when(pl.program_id(1) == pl.num_programs(1) - 1)
    def _():
        s1_ref[0] = s_scr[...]


def _rwkv_scan(sc_t, s0, bd128, eye):
    t, n, _ = sc_t.shape
    nb = s0.shape[2] // 2
    tb = _pick_tile(t, 64, 1)
    state_spec = pl.BlockSpec((1, A_HEAD_DIM, 2 * nb, LANES), lambda i, j: (i, 0, 0, 0))
    return pl.pallas_call(
        functools.partial(_rwkv_scan_kernel, tb=tb),
        grid=(n // nb, t // tb),
        in_specs=[pl.BlockSpec((tb, nb, 6 * A_WIDTH), lambda i, j: (j, i, 0)),
                  state_spec,
                  _full((2 * LANES, LANES)), _full((A_HEAD_DIM, 2 * nb, LANES))],
        out_specs=[pl.BlockSpec((tb, nb, A_WIDTH), lambda i, j: (j, i, 0)), state_spec],
        out_shape=[jax.ShapeDtypeStruct((t, n, A_WIDTH), F32),
                   jax.ShapeDtypeStruct(s0.shape, F32)],
        scratch_shapes=[pltpu.VMEM((A_HEAD_DIM, 2 * nb, LANES), F32)],
        compiler_params=_params("parallel", "arbitrary"),
        name="rwkv_scan",
    )(sc_t, s0, bd128, eye)


_CONV_PAD = 32


def _conv_kernel(zb_ref, c0_ref, cw_ref, cb_ref, lg_ref, lb_ref, pw_ref, pb_ref,
                 y_ref, c1_ref, ext_scr, h_scr, *, nb, t, tc):
    cw = cw_ref[...]
    for s in range(nb):
        zb = zb_ref[s * t:(s + 1) * t, :]
        u = zb[:, :B_WIDTH] * _sigmoid(zb[:, B_WIDTH:])
        ext_scr[s, _CONV_PAD - CONV_HIST:_CONV_PAD, :] = c0_ref[s]
        ext_scr[s, _CONV_PAD:_CONV_PAD + t, :] = u
        c1_ref[s] = ext_scr[s, t + _CONV_PAD - CONV_HIST:t + _CONV_PAD, :]

        def chunk(c, carry, s=s):
            base = pl.multiple_of(c * tc, 8)
            win = ext_scr[s, pl.ds(base, tc + _CONV_PAD), :]
            acc = jnp.zeros((tc, B_WIDTH), F32)
            for j in range(CONV_WIDTH):
                o = _CONV_PAD - CONV_HIST + j
                acc = acc + win[o:o + tc, :] * cw[j:j + 1, :]
            acc = acc + cb_ref[...]
            xc = acc - jnp.mean(acc, axis=-1, keepdims=True)
            hn = xc * lax.rsqrt(jnp.mean(xc * xc, axis=-1, keepdims=True) + EPS)
            hn = hn * lg_ref[...] + lb_ref[...]
            hn = hn * _sigmoid(hn)
            h_scr[pl.ds(pl.multiple_of(s * t + c * tc, 8), tc), :] = hn
            return carry

        lax.fori_loop(0, t // tc, chunk, 0)
    y_ref[...] = _dot(h_scr[...].astype(BF16), pw_ref[...]) + pb_ref[...]


def _conformer_conv(z, conv0, lp, pw_bf, n, t):
    nb = 1 if t >= 64 else _pick_tile(n, 16, 1)
    tc = _pick_tile(t, 48, 8)
    rows = nb * t
    vec = lambda a: a.reshape(1, -1)
    return pl.pallas_call(
        functools.partial(_conv_kernel, nb=nb, t=t, tc=tc),
        grid=(n // nb,),
        in_specs=[pl.BlockSpec((rows, B_COLS), lambda i: (i, A_COLS // B_COLS)),
                  pl.BlockSpec((nb, CONV_HIST, B_WIDTH), lambda i: (i, 0, 0)),
                  _full((CONV_WIDTH, B_WIDTH)), _full((1, B_WIDTH)), _full((1, B_WIDTH)),
                  _full((1, B_WIDTH)), _full((B_WIDTH, B_WIDTH)), _full((1, B_WIDTH))],
        out_specs=[pl.BlockSpec((rows, B_WIDTH), lambda i: (i, 0)),
                   pl.BlockSpec((nb, CONV_HIST, B_WIDTH), lambda i: (i, 0, 0))],
        out_shape=[jax.ShapeDtypeStruct((n * t, B_WIDTH), F32),
                   jax.ShapeDtypeStruct((n, CONV_HIST, B_WIDTH), F32)],
        scratch_shapes=[pltpu.VMEM((nb, t + _CONV_PAD, B_WIDTH), F32),
                        pltpu.VMEM((rows, B_WIDTH), F32)],
        compiler_params=_params("parallel"),
        name="conformer_conv",
    )(z, conv0, lp['b_conv_w'], vec(lp['b_conv_b']), vec(lp['b_ln_g']), vec(lp['b_ln_b']),
      pw_bf, vec(lp['b_pw_b']))


def _lambda_from(lamv_ref, lam_init):
    lv = lamv_ref[...]
    s1 = jnp.sum(lv[0:1] * lv[1:2], axis=-1, keepdims=True)
    s2 = jnp.sum(lv[2:3] * lv[3:4], axis=-1, keepdims=True)
    return jnp.exp(s1) - jnp.exp(s2) + lam_init


def _sub_ln(o, sg, lam_init):
    o = o * lax.rsqrt(jnp.mean(o * o, axis=-1, keepdims=True) + EPS) * sg
    return o * (1.0 - lam_init)


def _map_masks(rows):
    lane = lax.broadcasted_iota(jnp.int32, (rows, LANES), 1)
    return lane < C_QK_DIM, lane >= C_QK_DIM


def _attn_prompt_kernel(q_ref, k_ref, v_ref, lamv_ref, sg_ref, o_ref, m_scr, l_scr, acc_scr,
                        *, tq, lam_init):
    qi = pl.program_id(1)
    kj = pl.program_id(2)

    @pl.when(kj == 0)
    def _():
        m_scr[...] = jnp.full(m_scr.shape, NEG_BIG, F32)
        l_scr[...] = jnp.zeros(l_scr.shape, F32)
        acc_scr[...] = jnp.zeros(acc_scr.shape, F32)

    @pl.when(kj <= qi)
    def _():
        qpos = qi * tq + lax.broadcasted_iota(jnp.int32, (tq, tq), 0)
        kpos = kj * tq + lax.broadcasted_iota(jnp.int32, (tq, tq), 1)
        dist = (qpos - kpos).astype(F32)
        causal = kpos <= qpos
        masks = _map_masks(tq)
        for h in range(C_HEADS):
            qh = q_ref[:, h * LANES:(h + 1) * LANES]
            kh = k_ref[:, h * LANES:(h + 1) * LANES]
            vh = v_ref[:, h * LANES:(h + 1) * LANES]
            for m in range(2):
                i = 2 * h + m
                qm = jnp.where(masks[m], qh, 0.0).astype(BF16)
                s = _dot_nt(qm, kh) - ALIBI_SLOPES[h] * dist
                s = jnp.where(causal, s, NEG_BIG)
                m_prev = m_scr[i]
                m_new = jnp.maximum(m_prev, jnp.max(s, axis=-1, keepdims=True))
                alpha = jnp.exp(m_prev - m_new)
                p = jnp.exp(s - m_new)
                l_scr[i] = alpha * l_scr[i] + jnp.sum(p, axis=-1, keepdims=True)
                acc_scr[i] = alpha * acc_scr[i] + _dot(p.astype(BF16), vh)
                m_scr[i] = m_new

    @pl.when(kj == qi)
    def _():
        lam = _lambda_from(lamv_ref, lam_init)
        for h in range(C_HEADS):
            o = acc_scr[2 * h] / l_scr[2 * h] - lam * (acc_scr[2 * h + 1] / l_scr[2 * h + 1])
            o_ref[:, h * LANES:(h + 1) * LANES] = _sub_ln(o, sg_ref[...], lam_init)


def _attn_prompt(qn, kb, vb, lamv, sg, n, t, lam_init):
    tq = _pick_tile(t, 704, 16)
    nq = t // tq
    return pl.pallas_call(
        functools.partial(_attn_prompt_kernel, tq=tq, lam_init=lam_init),
        grid=(n, nq, nq),
        in_specs=[pl.BlockSpec((tq, C_QK_COLS), lambda b, i, j: (b * nq + i, 0)),
                  pl.BlockSpec((tq, C_QK_COLS), lambda b, i, j: (b * nq + jnp.minimum(i, j), 0)),
                  pl.BlockSpec((tq, C_WIDTH), lambda b, i, j: (b * nq + jnp.minimum(i, j), 0)),
                  _full((4, C_QK_DIM)), _full((1, C_V_DIM))],
        out_specs=pl.BlockSpec((tq, C_WIDTH), lambda b, i, j: (b * nq + i, 0)),
        out_shape=jax.ShapeDtypeStruct((n * t, C_WIDTH), F32),
        scratch_shapes=[pltpu.VMEM((2 * C_HEADS, tq, 1), F32),
                        pltpu.VMEM((2 * C_HEADS, tq, 1), F32),
                        pltpu.VMEM((2 * C_HEADS, tq, C_V_DIM), F32)],
        compiler_params=_params("parallel", "parallel", "arbitrary"),
        name="attn_prompt",
    )(qn, kb, vb, lamv, sg)


def _attn_sample_kernel(pt_ref, q_ref, kn_ref, vn_ref, lamv_ref, sg_ref, *rest,
                        n_pages, t_s, lam_init):
    k_refs = rest[:n_pages]
    v_refs = rest[n_pages:2 * n_pages]
    o_ref = rest[2 * n_pages]
    del pt_ref
    rows = t_s * C_HEADS
    past = n_pages * PAGE_SIZE
    q = q_ref[0]
    masks = _map_masks(rows)
    qm = [jnp.where(masks[m], q, 0.0).astype(BF16) for m in range(2)]
    head_of = lambda i: jnp.bitwise_and(i, C_HEADS - 1)
    token_of = lambda i: jnp.right_shift(i, C_HEADS.bit_length() - 1)
    rid = lax.broadcasted_iota(jnp.int32, (rows, 1), 0)
    rhead = head_of(rid)
    qpos = past + token_of(rid)
    slope = jnp.zeros((rows, 1), F32)
    for h in range(C_HEADS):
        slope = jnp.where(rhead == h, ALIBI_SLOPES[h], slope)

    m_run = [jnp.full((rows, 1), NEG_BIG, F32) for _ in range(2)]
    l_run = [jnp.zeros((rows, 1), F32) for _ in range(2)]
    acc = [jnp.zeros((rows, C_V_DIM), F32) for _ in range(2)]

    def update(kblk, vblk, kpos, valid):
        bias = slope * (qpos - kpos).astype(F32)
        for m in range(2):
            s = jnp.where(valid, _dot_nt(qm[m], kblk) - bias, NEG_BIG)
            m_new = jnp.maximum(m_run[m], jnp.max(s, axis=-1, keepdims=True))
            alpha = jnp.exp(m_run[m] - m_new)
            p = jnp.where(valid, jnp.exp(s - m_new), 0.0)
            l_run[m] = alpha * l_run[m] + jnp.sum(p, axis=-1, keepdims=True)
            acc[m] = alpha * acc[m] + _dot(p.astype(BF16), vblk)
            m_run[m] = m_new

    pcols = PAGE_SIZE * C_HEADS
    cid = lax.broadcasted_iota(jnp.int32, (rows, pcols), 1)
    same_head = head_of(cid) == rhead
    for j in range(n_pages):
        kpos = j * PAGE_SIZE + token_of(cid)
        update(k_refs[j][0].astype(BF16), v_refs[j][0].astype(BF16), kpos, same_head)
    cid = lax.broadcasted_iota(jnp.int32, (rows, rows), 1)
    kpos = past + token_of(cid)
    valid = (head_of(cid) == rhead) & (kpos <= qpos)
    update(kn_ref[0].astype(BF16), vn_ref[0].astype(BF16), kpos, valid)

    lam = _lambda_from(lamv_ref, lam_init)
    o = acc[0] / l_run[0] - lam * (acc[1] / l_run[1])
    o_ref[0] = _sub_ln(o, sg_ref[...], lam_init)


def _attn_sample(qn3, kn3, vn3, ck, cv, page_table, lamv, sg, lam_init):
    n, rows, _ = qn3.shape
    n_pages = page_table.shape[1]
    pcols = PAGE_SIZE * C_HEADS
    seq = lambda: pl.BlockSpec((1, rows, LANES), lambda i, pt: (i, 0, 0))
    page = lambda j: pl.BlockSpec((1, pcols, LANES), lambda i, pt, j=j: (pt[i, j], 0, 0))
    grid_spec = pltpu.PrefetchScalarGridSpec(
        num_scalar_prefetch=1,
        grid=(n,),
        in_specs=[seq(), seq(), seq(),
                  pl.BlockSpec((4, C_QK_DIM), lambda i, pt: (0, 0)),
                  pl.BlockSpec((1, C_V_DIM), lambda i, pt: (0, 0))]
                 + [page(j) for j in range(n_pages)] + [page(j) for j in range(n_pages)],
        out_specs=seq(),
    )
    return pl.pallas_call(
        functools.partial(_attn_sample_kernel, n_pages=n_pages, t_s=rows // C_HEADS, lam_init=lam_init),
        grid_spec=grid_spec,
        out_shape=jax.ShapeDtypeStruct((n, rows, LANES), F32),
        compiler_params=_params("parallel"),
        name="attn_sample",
    )(page_table, qn3, kn3, vn3, lamv, sg, *([ck] * n_pages), *([cv] * n_pages))


def _out_proj_kernel(x_ref, y_ref, gb_ref, yb_ref, yc_ref, lg_ref, lb_ref, bd_ref, w_ref, o_ref):
    y = y_ref[...]
    bd = bd_ref[...]
    inv = 1.0 / A_HEAD_DIM
    yc = y - _seg_sum(y, bd) * inv
    yn = yc * lax.rsqrt(_seg_sum(yc * yc, bd) * inv + A_GN_EPS)
    ya = (yn * lg_ref[...] + lb_ref[...] + gb_ref[:, A_WIDTH:]) * gb_ref[:, :A_WIDTH]
    acc = _dot(ya.astype(BF16), w_ref[0:A_WIDTH, :])
    acc += _dot(yb_ref[...].astype(BF16), w_ref[A_WIDTH:A_WIDTH + B_WIDTH, :])
    acc += _dot(yc_ref[...].astype(BF16), w_ref[A_WIDTH + B_WIDTH:, :])
    o_ref[...] = x_ref[...] + acc


def _out_proj(x, y, gb, yb, yc, lp, bd256, w_bf):
    rows = x.shape[0]
    tm = _pick_tile(rows, 512, 8)
    row = lambda c: pl.BlockSpec((tm, c), lambda i: (i, 0))
    vec = lambda a: a.reshape(1, -1)
    return pl.pallas_call(
        _out_proj_kernel,
        grid=(rows // tm,),
        in_specs=[row(D_MODEL), row(A_WIDTH), row(2 * A_WIDTH), row(B_WIDTH), row(C_WIDTH),
                  _full((1, A_WIDTH)), _full((1, A_WIDTH)), _full((2 * A_WIDTH, A_WIDTH)),
                  _full((D_MODEL, D_MODEL))],
        out_specs=row(D_MODEL),
        out_shape=jax.ShapeDtypeStruct((rows, D_MODEL), F32),
        compiler_params=_params("parallel"),
        name="out_proj",
    )(x, y, gb, yb, yc, vec(lp['a_lnx_g']), vec(lp['a_lnx_b']), bd256, w_bf)


def _router_kernel(x_ref, g_ref, r_ref, o_ref):
    x = x_ref[...]
    h = x * lax.rsqrt(jnp.mean(x * x, axis=-1, keepdims=True) + EPS) * g_ref[...]
    logits = _dot_hi(h, r_ref[...])
    lane = lax.broadcasted_iota(jnp.int32, logits.shape, 1).astype(F32)
    lg = jnp.where(lane < N_EXPERTS, logits, NEG_BIG)
    m1 = jnp.max(lg, axis=-1, keepdims=True)
    i1 = jnp.min(jnp.where(lg == m1, lane, float(LANES)), axis=-1, keepdims=True)
    lg2 = jnp.where(lane == i1, NEG_BIG, lg)
    m2 = jnp.max(lg2, axis=-1, keepdims=True)
    i2 = jnp.min(jnp.where(lg2 == m2, lane, float(LANES)), axis=-1, keepdims=True)
    e = jnp.exp(m2 - m1)
    g1 = 1.0 / (1.0 + e)
    o_ref[...] = jnp.where(lane == i1, g1, 0.0) + jnp.where(lane == i2, e * g1, 0.0)


def _router(x, g, router_pad):
    rows = x.shape[0]
    tm = _pick_tile(rows, 512, 8)
    return pl.pallas_call(
        _router_kernel,
        grid=(rows // tm,),
        in_specs=[pl.BlockSpec((tm, D_MODEL), lambda i: (i, 0)), _full((1, D_MODEL)),
                  _full((D_MODEL, LANES))],
        out_specs=pl.BlockSpec((tm, LANES), lambda i: (i, 0)),
        out_shape=jax.ShapeDtypeStruct((rows, LANES), F32),
        compiler_params=_params("parallel"),
        name="router",
    )(x, g, router_pad)


def _ffn_kernel(x_ref, g_ref, gate_ref, wg_ref, wu_ref, wd_ref, o_ref, h_scr, acc_scr):
    e = pl.program_id(1)
    f = pl.program_id(2)

    @pl.when((e == 0) & (f == 0))
    def _():
        x = x_ref[...]
        h = x * lax.rsqrt(jnp.mean(x * x, axis=-1, keepdims=True) + EPS) * g_ref[...]
        h_scr[...] = h.astype(BF16)
        acc_scr[...] = jnp.zeros(acc_scr.shape, F32)

    h = h_scr[...]
    a = _dot(h, wg_ref[0])
    u = _dot(h, wu_ref[0])
    act = (a * _sigmoid(a) * u).astype(BF16)
    lane = lax.broadcasted_iota(jnp.int32, gate_ref.shape, 1)
    gcol = jnp.sum(jnp.where(lane == e, gate_ref[...], 0.0), axis=-1, keepdims=True)
    acc_scr[...] += gcol * _dot(act, wd_ref[0])

    @pl.when((e == pl.num_programs(1) - 1) & (f == pl.num_programs(2) - 1))
    def _():
        o_ref[...] = x_ref[...] + acc_scr[...]


def _ffn(x, g, gates, wg, wu, wd):
    rows = x.shape[0]
    n_e = wg.shape[0]
    tm = _pick_tile(rows, 704, 16)
    tf = D_FF // 2
    return pl.pallas_call(
        _ffn_kernel,
        grid=(rows // tm, n_e, D_FF // tf),
        in_specs=[pl.BlockSpec((tm, D_MODEL), lambda i, e, f: (i, 0)),
                  pl.BlockSpec((1, D_MODEL), lambda i, e, f: (0, 0)),
                  pl.BlockSpec((tm, LANES), lambda i, e, f: (i, 0)),
                  pl.BlockSpec((1, D_MODEL, tf), lambda i, e, f: (e, 0, f)),
                  pl.BlockSpec((1, D_MODEL, tf), lambda i, e, f: (e, 0, f)),
                  pl.BlockSpec((1, tf, D_MODEL), lambda i, e, f: (e, f, 0))],
        out_specs=pl.BlockSpec((tm, D_MODEL), lambda i, e, f: (i, 0)),
        out_shape=jax.ShapeDtypeStruct((rows, D_MODEL), F32),
        scratch_shapes=[pltpu.VMEM((tm, D_MODEL), BF16), pltpu.VMEM((tm, D_MODEL), F32)],
        compiler_params=_params("parallel", "arbitrary", "arbitrary"),
        name="ffn",
    )(x, g, gates, wg, wu, wd)


def _wkv_to_pairs(s):
    n = s.shape[0]
    assert n % SCAN_SEQS == 0
    s = s.reshape(n // SCAN_SEQS, SCAN_SEQS, 2, 2, A_HEAD_DIM, A_HEAD_DIM)
    return s.transpose(0, 4, 2, 1, 3, 5).reshape(n // SCAN_SEQS, A_HEAD_DIM, 2 * SCAN_SEQS, LANES)


def _wkv_from_pairs(s):
    nblk = s.shape[0]
    s = s.reshape(nblk, A_HEAD_DIM, 2, SCAN_SEQS, 2, A_HEAD_DIM)
    return s.transpose(0, 3, 2, 4, 1, 5).reshape(nblk * SCAN_SEQS, A_HEADS, A_HEAD_DIM, A_HEAD_DIM)


def _mixers(x, n, t, shift0, wkv0, conv0, attend, lp, c, lam_init):
    z, qn, kn, kb, vb = _in_proj(x, lp['norm1_g'].reshape(1, -1), lp['w_in_bf'], lp['qg'], lp['kg'], c['bd512'])
    z3 = z.reshape(n, t, IN_COLS)
    za3 = z3[:, :, :A_COLS]
    prev = jnp.concatenate([shift0[:, None, :], za3[:, :-1]], axis=1).reshape(n * t, A_COLS)
    sc, gb = _rwkv_pre(z, prev, lp, c['bd256'])
    sc_t = sc.reshape(n, t, 6 * A_WIDTH).transpose(1, 0, 2)
    y_t, s1 = _rwkv_scan(sc_t, _wkv_to_pairs(wkv0), c['bd128'], c['eye'])
    y = y_t.transpose(1, 0, 2).reshape(n * t, A_WIDTH)
    yb, conv1 = _conformer_conv(z, conv0, lp, lp['pw_bf'], n, t)
    yc = attend(qn, kn, kb, vb, z)
    x = _out_proj(x, y, gb, yb, yc, lp, c['bd256'], lp['w_out_bf'])
    k_rows = kn.reshape(n, t, C_HEADS, 2 * C_QK_DIM)
    v_rows = z3[:, :, V_OFF:].reshape(n, t, C_HEADS, C_V_DIM)
    return x, k_rows, v_rows, _wkv_from_pairs(s1), za3[:, -1], conv1


def kernel(x_prompt, x_sample, cache_k, cache_v, page_table, state_wkv, state_shift, state_conv, meta_tokens, norm1_g, norm2_g, w_in, w_out, a_mu, a_w0, a_w_up, a_a0, a_a_up, a_g_up, a_k_k, a_k_a, a_r_k, a_lnx_g, a_lnx_b, b_conv_w, b_conv_b, b_ln_g, b_ln_b, b_pw_w, b_pw_b, c_qn_g, c_kn_g, c_lam_q1, c_lam_k1, c_lam_q2, c_lam_k2, c_subln_g, ffn_w_gate, ffn_w_up, ffn_w_down, moe_router, moe_w_gate, moe_w_up, moe_w_down):
    depth = w_in.shape[0]
    b, seq, _ = x_prompt.shape
    n_s, t_s, _ = x_sample.shape
    t_p = seq + N_META
    n_pool = cache_k.shape[1]
    pcols = PAGE_SIZE * C_HEADS

    consts = {
        'bd512': _block_diag2(C_QK_COLS),
        'bd256': _block_diag2(A_WIDTH),
        'bd128': _block_diag2(LANES),
        'eye': jnp.broadcast_to(
            (jnp.arange(LANES)[None, None, :] % A_HEAD_DIM == jnp.arange(A_HEAD_DIM)[:, None, None]).astype(F32),
            (A_HEAD_DIM, 2 * SCAN_SEQS, LANES)),
    }
    meta = jnp.broadcast_to(meta_tokens.astype(F32)[None], (b, N_META, D_MODEL))
    xp = jnp.concatenate([meta, x_prompt], axis=1).reshape(b * t_p, D_MODEL)
    xs = x_sample.reshape(n_s * t_s, D_MODEL)
    ones_gate = None

    outs = [[] for _ in range(10)]
    for l in range(depth):
        lam_init = 0.8 - 0.6 * math.exp(-0.3 * l)
        lp = {
            'norm1_g': norm1_g[l], 'w_in_bf': w_in[l].astype(BF16), 'w_out_bf': w_out[l].astype(BF16),
            'a_mu': a_mu[l], 'a_w0': a_w0[l], 'a_w_up': a_w_up[l], 'a_a0': a_a0[l], 'a_a_up': a_a_up[l],
            'a_g_up': a_g_up[l], 'a_k_k': a_k_k[l], 'a_k_a': a_k_a[l], 'a_r_k': a_r_k[l],
            'a_lnx_g': a_lnx_g[l], 'a_lnx_b': a_lnx_b[l],
            'b_conv_w': b_conv_w[l], 'b_conv_b': b_conv_b[l], 'b_ln_g': b_ln_g[l], 'b_ln_b': b_ln_b[l],
            'pw_bf': b_pw_w[l].astype(BF16), 'b_pw_b': b_pw_b[l],
            'qg': jnp.tile(c_qn_g[l], 2 * C_HEADS).reshape(1, -1),
            'kg': jnp.tile(c_kn_g[l], 2 * C_HEADS).reshape(1, -1),
        }
        lamv = jnp.stack([c_lam_q1[l], c_lam_k1[l], c_lam_q2[l], c_lam_k2[l]]).astype(F32)
        sg = c_subln_g[l].reshape(1, -1)

        def attend_prompt(qn, kn, kb, vb, z, lamv=lamv, sg=sg, lam_init=lam_init):
            return _attn_prompt(qn, kb, vb, lamv, sg, b, t_p, lam_init)

        def attend_sample(qn, kn, kb, vb, z, l=l, lamv=lamv, sg=sg, lam_init=lam_init):
            rows = t_s * C_HEADS
            qn3 = qn.reshape(n_s, rows, LANES)
            kn3 = kn.reshape(n_s, rows, LANES)
            vn3 = z[:, V_OFF:].reshape(n_s, rows, LANES)
            ck = cache_k[l].reshape(n_pool, pcols, LANES)
            cv = cache_v[l].reshape(n_pool, pcols, LANES)
            o = _attn_sample(qn3, kn3, vn3, ck, cv, page_table, lamv, sg, lam_init)
            return o.reshape(n_s * t_s, C_WIDTH)

        xp, k_p, v_p, wkv_p, shift_p, conv_p = _mixers(
            xp, b, t_p, jnp.zeros((b, A_COLS), F32), jnp.zeros((b, A_HEADS, A_HEAD_DIM, A_HEAD_DIM), F32),
            jnp.zeros((b, CONV_HIST, B_WIDTH), F32), attend_prompt, lp, consts, lam_init)
        xs, k_s, v_s, wkv_s, shift_s, conv_s = _mixers(
            xs, n_s, t_s, state_shift[l], state_wkv[l], state_conv[l], attend_sample, lp, consts, lam_init)

        i = l // 2
        g2 = norm2_g[l].reshape(1, -1)
        if l % 2 == 0:
            wg = ffn_w_gate[i].astype(BF16)[None]
            wu = ffn_w_up[i].astype(BF16)[None]
            wd = ffn_w_down[i].astype(BF16)[None]
            xp = _ffn(xp, g2, jnp.ones((xp.shape[0], LANES), F32), wg, wu, wd)
            xs = _ffn(xs, g2, jnp.ones((xs.shape[0], LANES), F32), wg, wu, wd)
        else:
            wg = moe_w_gate[i].astype(BF16)
            wu = moe_w_up[i].astype(BF16)
            wd = moe_w_down[i].astype(BF16)
            router_pad = jnp.pad(moe_router[i], ((0, 0), (0, LANES - N_EXPERTS)))
            xp = _ffn(xp, g2, _router(xp, g2, router_pad), wg, wu, wd)
            xs = _ffn(xs, g2, _router(xs, g2, router_pad), wg, wu, wd)

        for lst, val in zip(outs, (k_p, v_p, k_s, v_s, wkv_p, wkv_s, shift_p, shift_s, conv_p, conv_s)):
            lst.append(val)

    y_prompt = xp.reshape(b, t_p, D_MODEL)[:, N_META:]
    y_sample = xs.reshape(n_s, t_s, D_MODEL)
    return (y_prompt, y_sample) + tuple(jnp.stack(o) for o in outs)
```

```python
import functools
import math

import jax
import jax.numpy as jnp
from jax import lax
from jax.experimental import pallas as pl
from jax.experimental.pallas import tpu as pltpu

F32 = jnp.float32
BF16 = jnp.bfloat16
HIGHEST = lax.Precision.HIGHEST

D_MODEL = 1024
EPS = 1e-6
N_META = 16
A_HEAD_DIM = 64
A_WIDTH = 256
A_HEADS = 4
A_W_LORA = 64
A_A_LORA = 64
A_G_LORA = 128
A_COLS = 1024
A_GN_EPS = 64e-5
B_WIDTH = 256
B_COLS = 512
CONV_WIDTH = 31
CONV_HIST = CONV_WIDTH - 1
C_QK_DIM = 64
C_V_DIM = 128
C_WIDTH = 512
C_HEADS = 4
C_QK_COLS = 512
IN_COLS = 3072
Q_OFF = A_COLS + B_COLS
K_OFF = Q_OFF + C_QK_COLS
V_OFF = K_OFF + C_QK_COLS
D_FF = 2816
N_EXPERTS = 8
PAGE_SIZE = 128
LANES = 128
NEG_BIG = -1e30
ALIBI_SLOPES = tuple(float((2.0 ** (-8.0 / C_HEADS)) ** (h + 1)) for h in range(C_HEADS))
EXP_NEG_HALF = math.exp(-0.5)
SCAN_SEQS = 8
SCAN_COLS = 8 * A_WIDTH
SCAN_SUB = 8
VMEM_LIMIT = 56 * 1024 * 1024


def _params(*sem):
    return pltpu.CompilerParams(dimension_semantics=sem, vmem_limit_bytes=VMEM_LIMIT)


def _pick_tile(n, target, mult=8):
    best = None
    for d in range(mult, min(n, target) + 1, mult):
        if n % d == 0:
            best = d
    return n if best is None else best


def _dot(a, b):
    return jnp.dot(a, b, preferred_element_type=F32)


def _dot_hi(a, b):
    return jnp.dot(a, b, preferred_element_type=F32, precision=HIGHEST)


def _dot_nt(a, b):
    return lax.dot_general(a, b, (((1,), (1,)), ((), ())), preferred_element_type=F32)


def _sigmoid(x):
    return 1.0 / (1.0 + jnp.exp(-x))


def _seg_sum(x, bd2):
    hi = x.astype(BF16)
    lo = (x - hi.astype(F32)).astype(BF16)
    return _dot(jnp.concatenate([hi, lo], axis=-1), bd2)


def _block_diag2(width, seg=64):
    r = jnp.arange(width) // seg
    bd = (r[:, None] == r[None, :]).astype(BF16)
    return jnp.concatenate([bd, bd], axis=0)


def _full(shape):
    nd = len(shape)
    return pl.BlockSpec(shape, lambda *_: (0,) * nd)


def _in_proj_kernel(x_ref, g_ref, w_ref, qg_ref, kg_ref, bd_ref, z_ref, qn_ref, kn_ref, kb_ref, vb_ref):
    x = x_ref[...]
    h = x * lax.rsqrt(jnp.mean(x * x, axis=-1, keepdims=True) + EPS) * g_ref[...]
    z = _dot(h.astype(BF16), w_ref[...])
    z_ref[...] = z
    q = z[:, Q_OFF:Q_OFF + C_QK_COLS]
    k = z[:, K_OFF:K_OFF + C_QK_COLS]
    bd = bd_ref[...]
    inv = 1.0 / C_QK_DIM
    qn = q * lax.rsqrt(_seg_sum(q * q, bd) * inv + EPS) * qg_ref[...]
    kn = k * lax.rsqrt(_seg_sum(k * k, bd) * inv + EPS) * kg_ref[...]
    qn_ref[...] = qn * (C_QK_DIM ** -0.5)
    kn_ref[...] = kn
    kb_ref[...] = kn.astype(BF16)
    vb_ref[...] = z[:, V_OFF:].astype(BF16)


def _in_proj(x, g, w_bf, qg, kg, bd512):
    rows = x.shape[0]
    tm = _pick_tile(rows, 256, 16)
    row = lambda c: pl.BlockSpec((tm, c), lambda i: (i, 0))
    return pl.pallas_call(
        _in_proj_kernel,
        grid=(rows // tm,),
        in_specs=[row(D_MODEL), _full((1, D_MODEL)), _full((D_MODEL, IN_COLS)),
                  _full((1, C_QK_COLS)), _full((1, C_QK_COLS)), _full((2 * C_QK_COLS, C_QK_COLS))],
        out_specs=[row(IN_COLS), row(C_QK_COLS), row(C_QK_COLS), row(C_QK_COLS), row(C_WIDTH)],
        out_shape=[jax.ShapeDtypeStruct((rows, IN_COLS), F32),
                   jax.ShapeDtypeStruct((rows, C_QK_COLS), F32),
                   jax.ShapeDtypeStruct((rows, C_QK_COLS), F32),
                   jax.ShapeDtypeStruct((rows, C_QK_COLS), BF16),
                   jax.ShapeDtypeStruct((rows, C_WIDTH), BF16)],
        compiler_params=_params("parallel"),
        name="in_proj",
    )(x, g, w_bf, qg, kg, bd512)


def _rwkv_pre_kernel(za_ref, prev_ref, mu_ref, w0_ref, wup_ref, a0_ref, aup_ref, gup_ref,
                     kk_ref, ka_ref, rk_ref, bd_ref, sc_ref, gb_ref):
    za = za_ref[...]
    zs = za + mu_ref[...] * (prev_ref[...] - za)
    w = A_WIDTH
    r = zs[:, 0:w]
    k = zs[:, w:2 * w]
    v = zs[:, 2 * w:3 * w]
    wd = zs[:, 3 * w:3 * w + A_W_LORA]
    ad = zs[:, 3 * w + A_W_LORA:3 * w + A_W_LORA + A_A_LORA]
    gd = zs[:, 3 * w + A_W_LORA + A_A_LORA:]
    lw = w0_ref[...] + _dot_hi(jnp.tanh(wd), wup_ref[...])
    decay = jnp.exp(-EXP_NEG_HALF * _sigmoid(lw))
    a = _sigmoid(a0_ref[...] + _dot_hi(ad, aup_ref[...]))
    g = _dot_hi(_sigmoid(gd), gup_ref[...])
    bd = bd_ref[...]
    kk = k * kk_ref[...]
    k2 = k * (1.0 + (a - 1.0) * ka_ref[...])
    kk = kk * lax.rsqrt(jnp.maximum(_seg_sum(kk * kk, bd), 1e-24))
    bonus = _seg_sum(r * k2 * rk_ref[...], bd) * v
    b = kk * a
    sc_ref[:, 0:w] = decay
    sc_ref[:, w:2 * w] = kk
    sc_ref[:, 2 * w:3 * w] = b
    sc_ref[:, 3 * w:4 * w] = k2
    sc_ref[:, 4 * w:5 * w] = v
    sc_ref[:, 5 * w:6 * w] = decay * r
    sc_ref[:, 6 * w:7 * w] = _seg_sum(b * r, bd)
    sc_ref[:, 7 * w:8 * w] = _seg_sum(k2 * r, bd)
    gb_ref[:, 0:w] = g
    gb_ref[:, w:2 * w] = bonus


def _rwkv_pre(z, prev, lp, bd256):
    rows = z.shape[0]
    tm = _pick_tile(rows, 512, 8)
    row = lambda c: pl.BlockSpec((tm, c), lambda i: (i, 0))
    vec = lambda a: a.reshape(1, -1)
    args = (z, prev, vec(lp['a_mu']), vec(lp['a_w0']), lp['a_w_up'], vec(lp['a_a0']), lp['a_a_up'],
            lp['a_g_up'], vec(lp['a_k_k']), vec(lp['a_k_a']), vec(lp['a_r_k']), bd256)
    in_specs = [row(A_COLS), row(A_COLS)] + [_full(a.shape) for a in args[2:]]
    return pl.pallas_call(
        _rwkv_pre_kernel,
        grid=(rows // tm,),
        in_specs=in_specs,
        out_specs=[row(SCAN_COLS), row(2 * A_WIDTH)],
        out_shape=[jax.ShapeDtypeStruct((rows, SCAN_COLS), F32),
                   jax.ShapeDtypeStruct((rows, 2 * A_WIDTH), F32)],
        compiler_params=_params("parallel"),
        name="rwkv_pre",
    )(*args)


def _rwkv_scan_kernel(x_ref, s0_ref, bd_ref, eye_ref, eyeb_ref, y_ref, s1_ref, s_scr, vb_scr, *, tb, sub):
    nseq = x_ref.shape[1]
    npair = 2 * nseq
    w = A_WIDTH
    rows = A_HEAD_DIM * npair

    @pl.when(pl.program_id(1) == 0)
    def _():
        s_scr[...] = s0_ref[0]

    bd2 = bd_ref[...]
    bd1 = bd2[0:LANES]
    eye = eye_ref[...]
    eye_bf = eyeb_ref[...]

    def rowvec(xt, c):
        return jnp.concatenate([xt[:, c:c + LANES], xt[:, c + LANES:c + 2 * LANES]], axis=0)

    def value_bcast(t_src, slot, row):
        v = rowvec(x_ref[t_src], 4 * w).astype(BF16)
        vb = _dot((eye_bf * v[None]).reshape(rows, LANES), bd1)
        vb_scr[slot, row] = vb.reshape(A_HEAD_DIM, npair, LANES)

    def seg1(a):
        return _dot(a.reshape(rows, LANES).astype(BF16), bd1).reshape(A_HEAD_DIM, npair, LANES)

    for tt in range(sub):
        value_bcast(tt, 0, tt)

    def step(t, carry):
        slot = (t // sub) % 2
        row = t % sub
        value_bcast(jnp.minimum(t + sub, tb - 1), 1 - slot, row)
        xt = x_ref[t]
        dec, kk, b, k, v, wr, br, kr = (rowvec(xt, i * w) for i in range(8))
        s = s_scr[...]
        sa = seg1(s * kk[None])
        yr = seg1(s * wr[None])
        s_scr[...] = s * dec[None] - sa * b[None] + vb_scr[slot, row] * k[None]
        y = jnp.sum(yr * eye, axis=0) - jnp.sum(sa * eye, axis=0) * br + v * kr
        y_ref[t] = jnp.concatenate([y[0:nseq], y[nseq:npair]], axis=-1)
        return carry

    lax.fori_loop(0, tb, step, 0)

    @pl.when(pl.program_id(1) == pl.num_programs(1) - 1)
    def _():
        s1_ref[0] = s_scr[...]


def _rwkv_scan(sc_t, s0, bd128, eye):
    t, n, _ = sc_t.shape
    nb = s0.shape[2] // 2
    tb = _pick_tile(t, 64, SCAN_SUB)
    assert tb % SCAN_SUB == 0
    state_spec = pl.BlockSpec((1, A_HEAD_DIM, 2 * nb, LANES), lambda i, j: (i, 0, 0, 0))
    return pl.pallas_call(
        functools.partial(_rwkv_scan_kernel, tb=tb, sub=SCAN_SUB),
        grid=(n // nb, t // tb),
        in_specs=[pl.BlockSpec((tb, nb, SCAN_COLS), lambda i, j: (j, i, 0)),
                  state_spec,
                  _full((2 * LANES, LANES)), _full((A_HEAD_DIM, 2 * nb, LANES)),
                  _full((A_HEAD_DIM, 2 * nb, LANES))],
        out_specs=[pl.BlockSpec((tb, nb, A_WIDTH), lambda i, j: (j, i, 0)), state_spec],
        out_shape=[jax.ShapeDtypeStruct((t, n, A_WIDTH), F32),
                   jax.ShapeDtypeStruct(s0.shape, F32)],
        scratch_shapes=[pltpu.VMEM((A_HEAD_DIM, 2 * nb, LANES), F32),
                        pltpu.VMEM((2, SCAN_SUB, A_HEAD_DIM, 2 * nb, LANES), F32)],
        compiler_params=_params("parallel", "arbitrary"),
        name="rwkv_scan",
    )(sc_t, s0, bd128, eye, eye.astype(BF16))


_CONV_PAD = 32


def _conv_kernel(zb_ref, c0_ref, cw_ref, cb_ref, lg_ref, lb_ref, pw_ref, pb_ref,
                 y_ref, c1_ref, ext_scr, h_scr, *, nb, t, tc):
    cw = cw_ref[...]
    for s in range(nb):
        zb = zb_ref[s * t:(s + 1) * t, :]
        u = zb[:, :B_WIDTH] * _sigmoid(zb[:, B_WIDTH:])
        ext_scr[s, _CONV_PAD - CONV_HIST:_CONV_PAD, :] = c0_ref[s]
        ext_scr[s, _CONV_PAD:_CONV_PAD + t, :] = u
        c1_ref[s] = ext_scr[s, t + _CONV_PAD - CONV_HIST:t + _CONV_PAD, :]

        def chunk(c, carry, s=s):
            base = pl.multiple_of(c * tc, 8)
            win = ext_scr[s, pl.ds(base, tc + _CONV_PAD), :]
            acc = jnp.zeros((tc, B_WIDTH), F32)
            for j in range(CONV_WIDTH):
                o = _CONV_PAD - CONV_HIST + j
                acc = acc + win[o:o + tc, :] * cw[j:j + 1, :]
            acc = acc + cb_ref[...]
            xc = acc - jnp.mean(acc, axis=-1, keepdims=True)
            hn = xc * lax.rsqrt(jnp.mean(xc * xc, axis=-1, keepdims=True) + EPS)
            hn = hn * lg_ref[...] + lb_ref[...]
            hn = hn * _sigmoid(hn)
            h_scr[pl.ds(pl.multiple_of(s * t + c * tc, 8), tc), :] = hn
            return carry

        lax.fori_loop(0, t // tc, chunk, 0)
    y_ref[...] = _dot(h_scr[...].astype(BF16), pw_ref[...]) + pb_ref[...]


def _conformer_conv(z, conv0, lp, pw_bf, n, t):
    nb = 1 if t >= 64 else _pick_tile(n, 16, 1)
    tc = _pick_tile(t, 48, 8)
    rows = nb * t
    vec = lambda a: a.reshape(1, -1)
    return pl.pallas_call(
        functools.partial(_conv_kernel, nb=nb, t=t, tc=tc),
        grid=(n // nb,),
        in_specs=[pl.BlockSpec((rows, B_COLS), lambda i: (i, A_COLS // B_COLS)),
                  pl.BlockSpec((nb, CONV_HIST, B_WIDTH), lambda i: (i, 0, 0)),
                  _full((CONV_WIDTH, B_WIDTH)), _full((1, B_WIDTH)), _full((1, B_WIDTH)),
                  _full((1, B_WIDTH)), _full((B_WIDTH, B_WIDTH)), _full((1, B_WIDTH))],
        out_specs=[pl.BlockSpec((rows, B_WIDTH), lambda i: (i, 0)),
                   pl.BlockSpec((nb, CONV_HIST, B_WIDTH), lambda i: (i, 0, 0))],
        out_shape=[jax.ShapeDtypeStruct((n * t, B_WIDTH), F32),
                   jax.ShapeDtypeStruct((n, CONV_HIST, B_WIDTH), F32)],
        scratch_shapes=[pltpu.VMEM((nb, t + _CONV_PAD, B_WIDTH), F32),
                        pltpu.VMEM((rows, B_WIDTH), F32)],
        compiler_params=_params("parallel"),
        name="conformer_conv",
    )(z, conv0, lp['b_conv_w'], vec(lp['b_conv_b']), vec(lp['b_ln_g']), vec(lp['b_ln_b']),
      pw_bf, vec(lp['b_pw_b']))


def _lambda_from(lamv_ref, lam_init):
    lv = lamv_ref[...]
    s1 = jnp.sum(lv[0:1] * lv[1:2], axis=-1, keepdims=True)
    s2 = jnp.sum(lv[2:3] * lv[3:4], axis=-1, keepdims=True)
    return jnp.exp(s1) - jnp.exp(s2) + lam_init


def _sub_ln(o, sg, lam_init):
    o = o * lax.rsqrt(jnp.mean(o * o, axis=-1, keepdims=True) + EPS) * sg
    return o * (1.0 - lam_init)


def _map_masks(rows):
    lane = lax.broadcasted_iota(jnp.int32, (rows, LANES), 1)
    return lane < C_QK_DIM, lane >= C_QK_DIM


def _attn_prompt_kernel(q_ref, k_ref, v_ref, lamv_ref, sg_ref, o_ref, m_scr, l_scr, acc_scr,
                        *, tq, lam_init):
    qi = pl.program_id(1)
    kj = pl.program_id(2)

    @pl.when(kj == 0)
    def _():
        m_scr[...] = jnp.full(m_scr.shape, NEG_BIG, F32)
        l_scr[...] = jnp.zeros(l_scr.shape, F32)
        acc_scr[...] = jnp.zeros(acc_scr.shape, F32)

    @pl.when(kj <= qi)
    def _():
        qpos = qi * tq + lax.broadcasted_iota(jnp.int32, (tq, tq), 0)
        kpos = kj * tq + lax.broadcasted_iota(jnp.int32, (tq, tq), 1)
        dist = (qpos - kpos).astype(F32)
        causal = kpos <= qpos
        masks = _map_masks(tq)
        for h in range(C_HEADS):
            qh = q_ref[:, h * LANES:(h + 1) * LANES]
            kh = k_ref[:, h * LANES:(h + 1) * LANES]
            vh = v_ref[:, h * LANES:(h + 1) * LANES]
            for m in range(2):
                i = 2 * h + m
                qm = jnp.where(masks[m], qh, 0.0).astype(BF16)
                s = _dot_nt(qm, kh) - ALIBI_SLOPES[h] * dist
                s = jnp.where(causal, s, NEG_BIG)
                m_prev = m_scr[i]
                m_new = jnp.maximum(m_prev, jnp.max(s, axis=-1, keepdims=True))
                alpha = jnp.exp(m_prev - m_new)
                p = jnp.exp(s - m_new)
                l_scr[i] = alpha * l_scr[i] + jnp.sum(p, axis=-1, keepdims=True)
                acc_scr[i] = alpha * acc_scr[i] + _dot(p.astype(BF16), vh)
                m_scr[i] = m_new

    @pl.when(kj == qi)
    def _():
        lam = _lambda_from(lamv_ref, lam_init)
        for h in range(C_HEADS):
            o = acc_scr[2 * h] / l_scr[2 * h] - lam * (acc_scr[2 * h + 1] / l_scr[2 * h + 1])
            o_ref[:, h * LANES:(h + 1) * LANES] = _sub_ln(o, sg_ref[...], lam_init)


def _attn_prompt(qn, kb, vb, lamv, sg, n, t, lam_init):
    tq = _pick_tile(t, 704, 16)
    nq = t // tq
    return pl.pallas_call(
        functools.partial(_attn_prompt_kernel, tq=tq, lam_init=lam_init),
        grid=(n, nq, nq),
        in_specs=[pl.BlockSpec((tq, C_QK_COLS), lambda b, i, j: (b * nq + i, 0)),
                  pl.BlockSpec((tq, C_QK_COLS), lambda b, i, j: (b * nq + jnp.minimum(i, j), 0)),
                  pl.BlockSpec((tq, C_WIDTH), lambda b, i, j: (b * nq + jnp.minimum(i, j), 0)),
                  _full((4, C_QK_DIM)), _full((1, C_V_DIM))],
        out_specs=pl.BlockSpec((tq, C_WIDTH), lambda b, i, j: (b * nq + i, 0)),
        out_shape=jax.ShapeDtypeStruct((n * t, C_WIDTH), F32),
        scratch_shapes=[pltpu.VMEM((2 * C_HEADS, tq, 1), F32),
                        pltpu.VMEM((2 * C_HEADS, tq, 1), F32),
                        pltpu.VMEM((2 * C_HEADS, tq, C_V_DIM), F32)],
        compiler_params=_params("parallel", "parallel", "arbitrary"),
        name="attn_prompt",
    )(qn, kb, vb, lamv, sg)


def _attn_sample_kernel(pt_ref, q_ref, kn_ref, vn_ref, lamv_ref, sg_ref, tb_ref, cj_ref, tn_ref, *rest,
                        n_pages, t_s, lam_init):
    k_refs = rest[:n_pages]
    v_refs = rest[n_pages:2 * n_pages]
    o_ref = rest[2 * n_pages]
    del pt_ref
    rows = t_s * C_HEADS
    q = q_ref[0]
    masks = _map_masks(rows)
    qcat = jnp.concatenate([jnp.where(masks[0], q, 0.0), jnp.where(masks[1], q, 0.0),
                            jnp.zeros((LANES - 2 * rows, LANES), F32)], axis=0).astype(BF16)

    m_run = jnp.full((1, LANES), NEG_BIG, F32)
    l_run = jnp.zeros((1, LANES), F32)
    acc = jnp.zeros((C_V_DIM, LANES), F32)

    def update(kblk, vblk, bias, m_run, l_run, acc):
        s = _dot_nt(kblk.astype(BF16), qcat) + bias
        m_new = jnp.maximum(m_run, jnp.max(s, axis=0, keepdims=True))
        alpha = jnp.exp(m_run - m_new)
        p = jnp.exp(s - m_new)
        l_new = alpha * l_run + jnp.sum(p, axis=0, keepdims=True)
        acc_new = alpha * acc + _dot(vblk.T.astype(BF16), p.astype(BF16))
        return m_new, l_new, acc_new

    tbias = tb_ref[...]
    for j in range(n_pages):
        bias = tbias + cj_ref[j:j + 1, :]
        m_run, l_run, acc = update(k_refs[j][0], v_refs[j][0], bias, m_run, l_run, acc)
    m_run, l_run, acc = update(kn_ref[0], vn_ref[0], tn_ref[...], m_run, l_run, acc)

    lam = _lambda_from(lamv_ref, lam_init)
    o_all = (acc / l_run).T
    o = o_all[0:rows] - lam * o_all[rows:2 * rows]
    o_ref[0] = _sub_ln(o, sg_ref[...], lam_init)


def _sample_bias_tables(n_pages, t_s):
    rows = t_s * C_HEADS
    past = n_pages * PAGE_SIZE
    col = jnp.arange(LANES)
    used = col < 2 * rows
    c_r = col % rows
    c_t = c_r // C_HEADS
    c_h = c_r % C_HEADS
    slope = jnp.where(used, jnp.asarray(ALIBI_SLOPES, F32)[c_h], 0.0)
    prow = jnp.arange(PAGE_SIZE * C_HEADS)
    p_tok = prow // C_HEADS
    p_h = prow % C_HEADS
    ok = (p_h[:, None] == c_h[None, :]) | ~used[None, :]
    tb = jnp.where(ok, slope[None, :] * p_tok[:, None].astype(F32), NEG_BIG)
    starts = jnp.arange(n_pages) * PAGE_SIZE
    cj = -slope[None, :] * (past + c_t[None, :] - starts[:, None]).astype(F32)
    nrow = jnp.arange(rows)
    n_t = nrow // C_HEADS
    n_h = nrow % C_HEADS
    okn = ((n_h[:, None] == c_h[None, :]) & (n_t[:, None] <= c_t[None, :])) | ~used[None, :]
    tn = jnp.where(okn, -slope[None, :] * (c_t[None, :] - n_t[:, None]).astype(F32), NEG_BIG)
    return tb.astype(F32), cj.astype(F32), tn.astype(F32)


def _attn_sample(qn3, kn3, vn3, ck, cv, page_table, page_base, lamv, sg, tables, lam_init):
    n, rows, _ = qn3.shape
    n_pages = page_table.shape[1]
    pcols = PAGE_SIZE * C_HEADS
    assert 2 * rows <= LANES
    seq = lambda: pl.BlockSpec((1, rows, LANES), lambda i, pt: (i, 0, 0))
    page = lambda j: pl.BlockSpec((1, pcols, LANES), lambda i, pt, j=j: (page_base + pt[i, j], 0, 0))
    const = lambda a: pl.BlockSpec(a.shape, lambda i, pt: (0, 0))
    grid_spec = pltpu.PrefetchScalarGridSpec(
        num_scalar_prefetch=1,
        grid=(n,),
        in_specs=[seq(), seq(), seq(), const(lamv), const(sg)] + [const(a) for a in tables]
                 + [page(j) for j in range(n_pages)] + [page(j) for j in range(n_pages)],
        out_specs=seq(),
    )
    return pl.pallas_call(
        functools.partial(_attn_sample_kernel, n_pages=n_pages, t_s=rows // C_HEADS, lam_init=lam_init),
        grid_spec=grid_spec,
        out_shape=jax.ShapeDtypeStruct((n, rows, LANES), F32),
        compiler_params=_params("parallel"),
        name="attn_sample",
    )(page_table, qn3, kn3, vn3, lamv, sg, *tables, *([ck] * n_pages), *([cv] * n_pages))


def _out_proj_kernel(x_ref, y_ref, gb_ref, yb_ref, yc_ref, lg_ref, lb_ref, bd_ref, w_ref, o_ref):
    y = y_ref[...]
    bd = bd_ref[...]
    inv = 1.0 / A_HEAD_DIM
    yc = y - _seg_sum(y, bd) * inv
    yn = yc * lax.rsqrt(_seg_sum(yc * yc, bd) * inv + A_GN_EPS)
    ya = (yn * lg_ref[...] + lb_ref[...] + gb_ref[:, A_WIDTH:]) * gb_ref[:, :A_WIDTH]
    acc = _dot(ya.astype(BF16), w_ref[0:A_WIDTH, :])
    acc += _dot(yb_ref[...].astype(BF16), w_ref[A_WIDTH:A_WIDTH + B_WIDTH, :])
    acc += _dot(yc_ref[...].astype(BF16), w_ref[A_WIDTH + B_WIDTH:, :])
    o_ref[...] = x_ref[...] + acc


def _out_proj(x, y, gb, yb, yc, lp, bd256, w_bf):
    rows = x.shape[0]
    tm = _pick_tile(rows, 512, 8)
    row = lambda c: pl.BlockSpec((tm, c), lambda i: (i, 0))
    vec = lambda a: a.reshape(1, -1)
    return pl.pallas_call(
        _out_proj_kernel,
        grid=(rows // tm,),
        in_specs=[row(D_MODEL), row(A_WIDTH), row(2 * A_WIDTH), row(B_WIDTH), row(C_WIDTH),
                  _full((1, A_WIDTH)), _full((1, A_WIDTH)), _full((2 * A_WIDTH, A_WIDTH)),
                  _full((D_MODEL, D_MODEL))],
        out_specs=row(D_MODEL),
        out_shape=jax.ShapeDtypeStruct((rows, D_MODEL), F32),
        compiler_params=_params("parallel"),
        name="out_proj",
    )(x, y, gb, yb, yc, vec(lp['a_lnx_g']), vec(lp['a_lnx_b']), bd256, w_bf)


def _router_kernel(x_ref, g_ref, r_ref, o_ref):
    x = x_ref[...]
    h = x * lax.rsqrt(jnp.mean(x * x, axis=-1, keepdims=True) + EPS) * g_ref[...]
    logits = _dot_hi(h, r_ref[...])
    lane = lax.broadcasted_iota(jnp.int32, logits.shape, 1).astype(F32)
    lg = jnp.where(lane < N_EXPERTS, logits, NEG_BIG)
    m1 = jnp.max(lg, axis=-1, keepdims=True)
    i1 = jnp.min(jnp.where(lg == m1, lane, float(LANES)), axis=-1, keepdims=True)
    lg2 = jnp.where(lane == i1, NEG_BIG, lg)
    m2 = jnp.max(lg2, axis=-1, keepdims=True)
    i2 = jnp.min(jnp.where(lg2 == m2, lane, float(LANES)), axis=-1, keepdims=True)
    e = jnp.exp(m2 - m1)
    g1 = 1.0 / (1.0 + e)
    o_ref[...] = jnp.where(lane == i1, g1, 0.0) + jnp.where(lane == i2, e * g1, 0.0)


def _router(x, g, router_pad):
    rows = x.shape[0]
    tm = _pick_tile(rows, 512, 8)
    return pl.pallas_call(
        _router_kernel,
        grid=(rows // tm,),
        in_specs=[pl.BlockSpec((tm, D_MODEL), lambda i: (i, 0)), _full((1, D_MODEL)),
                  _full((D_MODEL, LANES))],
        out_specs=pl.BlockSpec((tm, LANES), lambda i: (i, 0)),
        out_shape=jax.ShapeDtypeStruct((rows, LANES), F32),
        compiler_params=_params("parallel"),
        name="router",
    )(x, g, router_pad)


def _ffn_kernel(x_ref, g_ref, gate_ref, wg_ref, wu_ref, wd_ref, o_ref, h_scr, acc_scr):
    e = pl.program_id(1)
    f = pl.program_id(2)

    @pl.when((e == 0) & (f == 0))
    def _():
        x = x_ref[...]
        h = x * lax.rsqrt(jnp.mean(x * x, axis=-1, keepdims=True) + EPS) * g_ref[...]
        h_scr[...] = h.astype(BF16)
        acc_scr[...] = jnp.zeros(acc_scr.shape, F32)

    h = h_scr[...]
    a = _dot(h, wg_ref[0])
    u = _dot(h, wu_ref[0])
    act = (a * _sigmoid(a) * u).astype(BF16)
    lane = lax.broadcasted_iota(jnp.int32, gate_ref.shape, 1)
    gcol = jnp.sum(jnp.where(lane == e, gate_ref[...], 0.0), axis=-1, keepdims=True)
    acc_scr[...] += gcol * _dot(act, wd_ref[0])

    @pl.when((e == pl.num_programs(1) - 1) & (f == pl.num_programs(2) - 1))
    def _():
        o_ref[...] = x_ref[...] + acc_scr[...]


def _ffn(x, g, gates, wg, wu, wd):
    rows = x.shape[0]
    n_e = wg.shape[0]
    tm = _pick_tile(rows, 704, 16)
    tf = D_FF // 2
    return pl.pallas_call(
        _ffn_kernel,
        grid=(rows // tm, n_e, D_FF // tf),
        in_specs=[pl.BlockSpec((tm, D_MODEL), lambda i, e, f: (i, 0)),
                  pl.BlockSpec((1, D_MODEL), lambda i, e, f: (0, 0)),
                  pl.BlockSpec((tm, LANES), lambda i, e, f: (i, 0)),
                  pl.BlockSpec((1, D_MODEL, tf), lambda i, e, f: (e, 0, f)),
                  pl.BlockSpec((1, D_MODEL, tf), lambda i, e, f: (e, 0, f)),
                  pl.BlockSpec((1, tf, D_MODEL), lambda i, e, f: (e, f, 0))],
        out_specs=pl.BlockSpec((tm, D_MODEL), lambda i, e, f: (i, 0)),
        out_shape=jax.ShapeDtypeStruct((rows, D_MODEL), F32),
        scratch_shapes=[pltpu.VMEM((tm, D_MODEL), BF16), pltpu.VMEM((tm, D_MODEL), F32)],
        compiler_params=_params("parallel", "arbitrary", "arbitrary"),
        name="ffn",
    )(x, g, gates, wg, wu, wd)


def _wkv_to_pairs(s):
    n = s.shape[0]
    assert n % SCAN_SEQS == 0
    s = s.reshape(n // SCAN_SEQS, SCAN_SEQS, 2, 2, A_HEAD_DIM, A_HEAD_DIM)
    return s.transpose(0, 4, 2, 1, 3, 5).reshape(n // SCAN_SEQS, A_HEAD_DIM, 2 * SCAN_SEQS, LANES)


def _wkv_from_pairs(s):
    nblk = s.shape[0]
    s = s.reshape(nblk, A_HEAD_DIM, 2, SCAN_SEQS, 2, A_HEAD_DIM)
    return s.transpose(0, 3, 2, 4, 1, 5).reshape(nblk * SCAN_SEQS, A_HEADS, A_HEAD_DIM, A_HEAD_DIM)


def _mixers(x, n, t, shift0, wkv0, conv0, attend, lp, c, lam_init):
    z, qn, kn, kb, vb = _in_proj(x, lp['norm1_g'].reshape(1, -1), lp['w_in_bf'], lp['qg'], lp['kg'], c['bd512'])
    z3 = z.reshape(n, t, IN_COLS)
    za3 = z3[:, :, :A_COLS]
    prev = jnp.concatenate([shift0[:, None, :], za3[:, :-1]], axis=1).reshape(n * t, A_COLS)
    sc, gb = _rwkv_pre(z, prev, lp, c['bd256'])
    sc_t = sc.reshape(n, t, SCAN_COLS).transpose(1, 0, 2)
    y_t, s1 = _rwkv_scan(sc_t, _wkv_to_pairs(wkv0), c['bd128'], c['eye'])
    y = y_t.transpose(1, 0, 2).reshape(n * t, A_WIDTH)
    yb, conv1 = _conformer_conv(z, conv0, lp, lp['pw_bf'], n, t)
    yc = attend(qn, kn, kb, vb, z)
    x = _out_proj(x, y, gb, yb, yc, lp, c['bd256'], lp['w_out_bf'])
    k_rows = kn.reshape(n, t, C_HEADS, 2 * C_QK_DIM)
    v_rows = z3[:, :, V_OFF:].reshape(n, t, C_HEADS, C_V_DIM)
    return x, k_rows, v_rows, _wkv_from_pairs(s1), za3[:, -1], conv1


def kernel(x_prompt, x_sample, cache_k, cache_v, page_table, state_wkv, state_shift, state_conv, meta_tokens, norm1_g, norm2_g, w_in, w_out, a_mu, a_w0, a_w_up, a_a0, a_a_up, a_g_up, a_k_k, a_k_a, a_r_k, a_lnx_g, a_lnx_b, b_conv_w, b_conv_b, b_ln_g, b_ln_b, b_pw_w, b_pw_b, c_qn_g, c_kn_g, c_lam_q1, c_lam_k1, c_lam_q2, c_lam_k2, c_subln_g, ffn_w_gate, ffn_w_up, ffn_w_down, moe_router, moe_w_gate, moe_w_up, moe_w_down):
    depth = w_in.shape[0]
    b, seq, _ = x_prompt.shape
    n_s, t_s, _ = x_sample.shape
    t_p = seq + N_META
    n_pool = cache_k.shape[1]
    pcols = PAGE_SIZE * C_HEADS

    consts = {
        'bd512': _block_diag2(C_QK_COLS),
        'bd256': _block_diag2(A_WIDTH),
        'bd128': _block_diag2(LANES),
        'eye': jnp.broadcast_to(
            (jnp.arange(LANES)[None, None, :] % A_HEAD_DIM == jnp.arange(A_HEAD_DIM)[:, None, None]).astype(F32),
            (A_HEAD_DIM, 2 * SCAN_SEQS, LANES)),
    }
    meta = jnp.broadcast_to(meta_tokens.astype(F32)[None], (b, N_META, D_MODEL))
    xp = jnp.concatenate([meta, x_prompt], axis=1).reshape(b * t_p, D_MODEL)
    xs = x_sample.reshape(n_s * t_s, D_MODEL)
    ck_all = cache_k.reshape(depth * n_pool, pcols, LANES)
    cv_all = cache_v.reshape(depth * n_pool, pcols, LANES)
    sample_tables = _sample_bias_tables(page_table.shape[1], t_s)

    outs = [[] for _ in range(10)]
    for l in range(depth):
        lam_init = 0.8 - 0.6 * math.exp(-0.3 * l)
        lp = {
            'norm1_g': norm1_g[l], 'w_in_bf': w_in[l].astype(BF16), 'w_out_bf': w_out[l].astype(BF16),
            'a_mu': a_mu[l], 'a_w0': a_w0[l], 'a_w_up': a_w_up[l], 'a_a0': a_a0[l], 'a_a_up': a_a_up[l],
            'a_g_up': a_g_up[l], 'a_k_k': a_k_k[l], 'a_k_a': a_k_a[l], 'a_r_k': a_r_k[l],
            'a_lnx_g': a_lnx_g[l], 'a_lnx_b': a_lnx_b[l],
            'b_conv_w': b_conv_w[l], 'b_conv_b': b_conv_b[l], 'b_ln_g': b_ln_g[l], 'b_ln_b': b_ln_b[l],
            'pw_bf': b_pw_w[l].astype(BF16), 'b_pw_b': b_pw_b[l],
            'qg': jnp.tile(c_qn_g[l], 2 * C_HEADS).reshape(1, -1),
            'kg': jnp.tile(c_kn_g[l], 2 * C_HEADS).reshape(1, -1),
        }
        lamv = jnp.stack([c_lam_q1[l], c_lam_k1[l], c_lam_q2[l], c_lam_k2[l]]).astype(F32)
        sg = c_subln_g[l].reshape(1, -1)

        def attend_prompt(qn, kn, kb, vb, z, lamv=lamv, sg=sg, lam_init=lam_init):
            return _attn_prompt(qn, kb, vb, lamv, sg, b, t_p, lam_init)

        def attend_sample(qn, kn, kb, vb, z, l=l, lamv=lamv, sg=sg, lam_init=lam_init):
            rows = t_s * C_HEADS
            qn3 = qn.reshape(n_s, rows, LANES)
            kn3 = kn.reshape(n_s, rows, LANES)
            vn3 = z[:, V_OFF:].reshape(n_s, rows, LANES)
            o = _attn_sample(qn3, kn3, vn3, ck_all, cv_all, page_table, l * n_pool, lamv, sg, sample_tables,
                             lam_init)
            return o.reshape(n_s * t_s, C_WIDTH)

        xp, k_p, v_p, wkv_p, shift_p, conv_p = _mixers(
            xp, b, t_p, jnp.zeros((b, A_COLS), F32), jnp.zeros((b, A_HEADS, A_HEAD_DIM, A_HEAD_DIM), F32),
            jnp.zeros((b, CONV_HIST, B_WIDTH), F32), attend_prompt, lp, consts, lam_init)
        xs, k_s, v_s, wkv_s, shift_s, conv_s = _mixers(
            xs, n_s, t_s, state_shift[l], state_wkv[l], state_conv[l], attend_sample, lp, consts, lam_init)

        i = l // 2
        g2 = norm2_g[l].reshape(1, -1)
        if l % 2 == 0:
            wg = ffn_w_gate[i].astype(BF16)[None]
            wu = ffn_w_up[i].astype(BF16)[None]
            wd = ffn_w_down[i].astype(BF16)[None]
            xp = _ffn(xp, g2, jnp.ones((xp.shape[0], LANES), F32), wg, wu, wd)
            xs = _ffn(xs, g2, jnp.ones((xs.shape[0], LANES), F32), wg, wu, wd)
        else:
            wg = moe_w_gate[i].astype(BF16)
            wu = moe_w_up[i].astype(BF16)
            wd = moe_w_down[i].astype(BF16)
            router_pad = jnp.pad(moe_router[i], ((0, 0), (0, LANES - N_EXPERTS)))
            xp = _ffn(xp, g2, _router(xp, g2, router_pad), wg, wu, wd)
            xs = _ffn(xs, g2, _router(xs, g2, router_pad), wg, wu, wd)

        for lst, val in zip(outs, (k_p, v_p, k_s, v_s, wkv_p, wkv_s, shift_p, shift_s, conv_p, conv_s)):
            lst.append(val)

    y_prompt = xp.reshape(b, t_p, D_MODEL)[:, N_META:]
    y_sample = xs.reshape(n_s, t_s, D_MODEL)
    return (y_prompt, y_sample) + tuple(jnp.stack(o) for o in outs)
```

```python
import functools
import math

import jax
import jax.numpy as jnp
from jax import lax
from jax.experimental import pallas as pl
from jax.experimental.pallas import tpu as pltpu

F32 = jnp.float32
BF16 = jnp.bfloat16
HIGHEST = lax.Precision.HIGHEST

D_MODEL = 1024
EPS = 1e-6
N_META = 16
A_HEAD_DIM = 64
A_WIDTH = 256
A_HEADS = 4
A_W_LORA = 64
A_A_LORA = 64
A_G_LORA = 128
A_COLS = 1024
A_GN_EPS = 64e-5
B_WIDTH = 256
B_COLS = 512
CONV_WIDTH = 31
CONV_HIST = CONV_WIDTH - 1
C_QK_DIM = 64
C_V_DIM = 128
C_WIDTH = 512
C_HEADS = 4
C_QK_COLS = 512
IN_COLS = 3072
Q_OFF = A_COLS + B_COLS
K_OFF = Q_OFF + C_QK_COLS
V_OFF = K_OFF + C_QK_COLS
D_FF = 2816
N_EXPERTS = 8
PAGE_SIZE = 128
LANES = 128
NEG_BIG = -1e30
ALIBI_SLOPES = tuple(float((2.0 ** (-8.0 / C_HEADS)) ** (h + 1)) for h in range(C_HEADS))
EXP_NEG_HALF = math.exp(-0.5)
SCAN_SEQS = 8
SCAN_COLS = 8 * A_WIDTH
SCAN_SUB = 8
TOP_K = 2
MIN_TIME_MAJOR_T = 64
MOE_MIN_WINDOWS = 3
MOE_MAX_WINDOW = 704
MOE_CAP_SLACK_PCT = 111
VMEM_LIMIT = 56 * 1024 * 1024


def _params(*sem):
    return pltpu.CompilerParams(dimension_semantics=sem, vmem_limit_bytes=VMEM_LIMIT)


def _pick_tile(n, target, mult=8):
    best = None
    for d in range(mult, min(n, target) + 1, mult):
        if n % d == 0:
            best = d
    return n if best is None else best


def _dot(a, b):
    return jnp.dot(a, b, preferred_element_type=F32)


def _dot_hi(a, b):
    return jnp.dot(a, b, preferred_element_type=F32, precision=HIGHEST)


def _dot_nt(a, b):
    return lax.dot_general(a, b, (((1,), (1,)), ((), ())), preferred_element_type=F32)


def _sigmoid(x):
    return 1.0 / (1.0 + jnp.exp(-x))


def _seg_sum(x, bd2):
    hi = x.astype(BF16)
    lo = (x - hi.astype(F32)).astype(BF16)
    return _dot(jnp.concatenate([hi, lo], axis=-1), bd2)


def _block_diag2(width, seg=64):
    r = jnp.arange(width) // seg
    bd = (r[:, None] == r[None, :]).astype(BF16)
    return jnp.concatenate([bd, bd], axis=0)


def _full(shape):
    nd = len(shape)
    return pl.BlockSpec(shape, lambda *_: (0,) * nd)


def _in_proj_kernel(x_ref, g_ref, w_ref, qg_ref, kg_ref, bd_ref, z_ref, qn_ref, kn_ref, kb_ref, vb_ref):
    x = x_ref[...]
    h = x * lax.rsqrt(jnp.mean(x * x, axis=-1, keepdims=True) + EPS) * g_ref[...]
    z = _dot(h.astype(BF16), w_ref[...])
    z_ref[...] = z
    q = z[:, Q_OFF:Q_OFF + C_QK_COLS]
    k = z[:, K_OFF:K_OFF + C_QK_COLS]
    bd = bd_ref[...]
    inv = 1.0 / C_QK_DIM
    qn = q * lax.rsqrt(_seg_sum(q * q, bd) * inv + EPS) * qg_ref[...]
    kn = k * lax.rsqrt(_seg_sum(k * k, bd) * inv + EPS) * kg_ref[...]
    qn_ref[...] = qn * (C_QK_DIM ** -0.5)
    kn_ref[...] = kn
    kb_ref[...] = kn.astype(BF16)
    vb_ref[...] = z[:, V_OFF:].astype(BF16)


def _in_proj(x, g, w_bf, qg, kg, bd512):
    rows = x.shape[0]
    tm = _pick_tile(rows, 256, 16)
    row = lambda c: pl.BlockSpec((tm, c), lambda i: (i, 0))
    return pl.pallas_call(
        _in_proj_kernel,
        grid=(rows // tm,),
        in_specs=[row(D_MODEL), _full((1, D_MODEL)), _full((D_MODEL, IN_COLS)),
                  _full((1, C_QK_COLS)), _full((1, C_QK_COLS)), _full((2 * C_QK_COLS, C_QK_COLS))],
        out_specs=[row(IN_COLS), row(C_QK_COLS), row(C_QK_COLS), row(C_QK_COLS), row(C_WIDTH)],
        out_shape=[jax.ShapeDtypeStruct((rows, IN_COLS), F32),
                   jax.ShapeDtypeStruct((rows, C_QK_COLS), F32),
                   jax.ShapeDtypeStruct((rows, C_QK_COLS), F32),
                   jax.ShapeDtypeStruct((rows, C_QK_COLS), BF16),
                   jax.ShapeDtypeStruct((rows, C_WIDTH), BF16)],
        compiler_params=_params("parallel"),
        name="in_proj",
    )(x, g, w_bf, qg, kg, bd512)


def _rwkv_pre_kernel(za_ref, prev_ref, mu_ref, w0_ref, wup_ref, a0_ref, aup_ref, gup_ref,
                     kk_ref, ka_ref, rk_ref, bd_ref, sc_ref, gb_ref):
    za = za_ref[...]
    zs = za + mu_ref[...] * (prev_ref[...] - za)
    w = A_WIDTH
    r = zs[:, 0:w]
    k = zs[:, w:2 * w]
    v = zs[:, 2 * w:3 * w]
    wd = zs[:, 3 * w:3 * w + A_W_LORA]
    ad = zs[:, 3 * w + A_W_LORA:3 * w + A_W_LORA + A_A_LORA]
    gd = zs[:, 3 * w + A_W_LORA + A_A_LORA:]
    lw = w0_ref[...] + _dot_hi(jnp.tanh(wd), wup_ref[...])
    decay = jnp.exp(-EXP_NEG_HALF * _sigmoid(lw))
    a = _sigmoid(a0_ref[...] + _dot_hi(ad, aup_ref[...]))
    g = _dot_hi(_sigmoid(gd), gup_ref[...])
    bd = bd_ref[...]
    kk = k * kk_ref[...]
    k2 = k * (1.0 + (a - 1.0) * ka_ref[...])
    kk = kk * lax.rsqrt(jnp.maximum(_seg_sum(kk * kk, bd), 1e-24))
    bonus = _seg_sum(r * k2 * rk_ref[...], bd) * v
    b = kk * a
    sc_ref[:, 0:w] = decay
    sc_ref[:, w:2 * w] = kk
    sc_ref[:, 2 * w:3 * w] = b
    sc_ref[:, 3 * w:4 * w] = k2
    sc_ref[:, 4 * w:5 * w] = v
    sc_ref[:, 5 * w:6 * w] = decay * r
    sc_ref[:, 6 * w:7 * w] = _seg_sum(b * r, bd)
    sc_ref[:, 7 * w:8 * w] = _seg_sum(k2 * r, bd)
    gb_ref[:, 0:w] = g
    gb_ref[:, w:2 * w] = bonus


def _time_major_spec(tm, cols, tiles_per_seq):
    return pl.BlockSpec((tm, cols), lambda i: (i % tiles_per_seq, i // tiles_per_seq))


def _seq_tile(t):
    return _pick_tile(t, 512, 8) if t >= MIN_TIME_MAJOR_T else None


def _rwkv_pre(z, prev, lp, bd256, n, t):
    rows = z.shape[0]
    tm = _seq_tile(t) or _pick_tile(rows, 512, 8)
    row = lambda c: pl.BlockSpec((tm, c), lambda i: (i, 0))
    vec = lambda a: a.reshape(1, -1)
    args = (z, prev, vec(lp['a_mu']), vec(lp['a_w0']), lp['a_w_up'], vec(lp['a_a0']), lp['a_a_up'],
            lp['a_g_up'], vec(lp['a_k_k']), vec(lp['a_k_a']), vec(lp['a_r_k']), bd256)
    in_specs = [row(A_COLS), row(A_COLS)] + [_full(a.shape) for a in args[2:]]
    if _seq_tile(t):
        sc_spec, sc_shape = _time_major_spec(tm, SCAN_COLS, t // tm), (t, n * SCAN_COLS)
    else:
        sc_spec, sc_shape = row(SCAN_COLS), (rows, SCAN_COLS)
    sc, gb = pl.pallas_call(
        _rwkv_pre_kernel,
        grid=(rows // tm,),
        in_specs=in_specs,
        out_specs=[sc_spec, row(2 * A_WIDTH)],
        out_shape=[jax.ShapeDtypeStruct(sc_shape, F32),
                   jax.ShapeDtypeStruct((rows, 2 * A_WIDTH), F32)],
        compiler_params=_params("parallel"),
        name="rwkv_pre",
    )(*args)
    if _seq_tile(t):
        return sc.reshape(t, n, SCAN_COLS), gb
    return sc.reshape(n, t, SCAN_COLS).transpose(1, 0, 2), gb


def _rwkv_scan_kernel(x_ref, s0_ref, bd_ref, eye_ref, eyeb_ref, y_ref, s1_ref, s_scr, vb_scr, *, tb, sub):
    nseq = x_ref.shape[1]
    npair = 2 * nseq
    w = A_WIDTH
    rows = A_HEAD_DIM * npair

    @pl.when(pl.program_id(1) == 0)
    def _():
        s_scr[...] = s0_ref[0]

    bd2 = bd_ref[...]
    bd1 = bd2[0:LANES]
    eye = eye_ref[...]
    eye_bf = eyeb_ref[...]

    def rowvec(xt, c):
        return jnp.concatenate([xt[:, c:c + LANES], xt[:, c + LANES:c + 2 * LANES]], axis=0)

    def value_bcast(t_src, slot, row):
        v = rowvec(x_ref[t_src], 4 * w).astype(BF16)
        vb = _dot((eye_bf * v[None]).reshape(rows, LANES), bd1)
        vb_scr[slot, row] = vb.reshape(A_HEAD_DIM, npair, LANES)

    def seg1(a):
        return _dot(a.reshape(rows, LANES).astype(BF16), bd1).reshape(A_HEAD_DIM, npair, LANES)

    for tt in range(sub):
        value_bcast(tt, 0, tt)

    def step(t, carry):
        slot = (t // sub) % 2
        row = t % sub
        value_bcast(jnp.minimum(t + sub, tb - 1), 1 - slot, row)
        xt = x_ref[t]
        dec, kk, b, k, v, wr, br, kr = (rowvec(xt, i * w) for i in range(8))
        s = s_scr[...]
        sa = seg1(s * kk[None])
        yr = seg1(s * wr[None])
        s_scr[...] = s * dec[None] - sa * b[None] + vb_scr[slot, row] * k[None]
        y = jnp.sum(yr * eye, axis=0) - jnp.sum(sa * eye, axis=0) * br + v * kr
        y_ref[t] = jnp.concatenate([y[0:nseq], y[nseq:npair]], axis=-1)
        return carry

    lax.fori_loop(0, tb, step, 0)

    @pl.when(pl.program_id(1) == pl.num_programs(1) - 1)
    def _():
        s1_ref[0] = s_scr[...]


def _rwkv_scan(sc_t, s0, bd128, eye):
    t, n, _ = sc_t.shape
    nb = s0.shape[2] // 2
    tb = _pick_tile(t, 64, SCAN_SUB)
    assert tb % SCAN_SUB == 0
    state_spec = pl.BlockSpec((1, A_HEAD_DIM, 2 * nb, LANES), lambda i, j: (i, 0, 0, 0))
    return pl.pallas_call(
        functools.partial(_rwkv_scan_kernel, tb=tb, sub=SCAN_SUB),
        grid=(n // nb, t // tb),
        in_specs=[pl.BlockSpec((tb, nb, SCAN_COLS), lambda i, j: (j, i, 0)),
                  state_spec,
                  _full((2 * LANES, LANES)), _full((A_HEAD_DIM, 2 * nb, LANES)),
                  _full((A_HEAD_DIM, 2 * nb, LANES))],
        out_specs=[pl.BlockSpec((tb, nb, A_WIDTH), lambda i, j: (j, i, 0)), state_spec],
        out_shape=[jax.ShapeDtypeStruct((t, n, A_WIDTH), F32),
                   jax.ShapeDtypeStruct(s0.shape, F32)],
        scratch_shapes=[pltpu.VMEM((A_HEAD_DIM, 2 * nb, LANES), F32),
                        pltpu.VMEM((2, SCAN_SUB, A_HEAD_DIM, 2 * nb, LANES), F32)],
        compiler_params=_params("parallel", "arbitrary"),
        name="rwkv_scan",
    )(sc_t, s0, bd128, eye, eye.astype(BF16))


_CONV_PAD = 32


def _conv_kernel(zb_ref, c0_ref, cw_ref, cb_ref, lg_ref, lb_ref, pw_ref, pb_ref,
                 y_ref, c1_ref, ext_scr, h_scr, *, nb, t, tc):
    cw = cw_ref[...]
    for s in range(nb):
        zb = zb_ref[s * t:(s + 1) * t, :]
        u = zb[:, :B_WIDTH] * _sigmoid(zb[:, B_WIDTH:])
        ext_scr[s, _CONV_PAD - CONV_HIST:_CONV_PAD, :] = c0_ref[s]
        ext_scr[s, _CONV_PAD:_CONV_PAD + t, :] = u
        c1_ref[s] = ext_scr[s, t + _CONV_PAD - CONV_HIST:t + _CONV_PAD, :]

        def chunk(c, carry, s=s):
            base = pl.multiple_of(c * tc, 8)
            win = ext_scr[s, pl.ds(base, tc + _CONV_PAD), :]
            acc = jnp.zeros((tc, B_WIDTH), F32)
            for j in range(CONV_WIDTH):
                o = _CONV_PAD - CONV_HIST + j
                acc = acc + win[o:o + tc, :] * cw[j:j + 1, :]
            acc = acc + cb_ref[...]
            xc = acc - jnp.mean(acc, axis=-1, keepdims=True)
            hn = xc * lax.rsqrt(jnp.mean(xc * xc, axis=-1, keepdims=True) + EPS)
            hn = hn * lg_ref[...] + lb_ref[...]
            hn = hn * _sigmoid(hn)
            h_scr[pl.ds(pl.multiple_of(s * t + c * tc, 8), tc), :] = hn
            return carry

        lax.fori_loop(0, t // tc, chunk, 0)
    y_ref[...] = _dot(h_scr[...].astype(BF16), pw_ref[...]) + pb_ref[...]


def _conformer_conv(z, conv0, lp, pw_bf, n, t):
    nb = 1 if t >= 64 else _pick_tile(n, 16, 1)
    tc = _pick_tile(t, 48, 8)
    rows = nb * t
    vec = lambda a: a.reshape(1, -1)
    return pl.pallas_call(
        functools.partial(_conv_kernel, nb=nb, t=t, tc=tc),
        grid=(n // nb,),
        in_specs=[pl.BlockSpec((rows, B_COLS), lambda i: (i, A_COLS // B_COLS)),
                  pl.BlockSpec((nb, CONV_HIST, B_WIDTH), lambda i: (i, 0, 0)),
                  _full((CONV_WIDTH, B_WIDTH)), _full((1, B_WIDTH)), _full((1, B_WIDTH)),
                  _full((1, B_WIDTH)), _full((B_WIDTH, B_WIDTH)), _full((1, B_WIDTH))],
        out_specs=[pl.BlockSpec((rows, B_WIDTH), lambda i: (i, 0)),
                   pl.BlockSpec((nb, CONV_HIST, B_WIDTH), lambda i: (i, 0, 0))],
        out_shape=[jax.ShapeDtypeStruct((n * t, B_WIDTH), F32),
                   jax.ShapeDtypeStruct((n, CONV_HIST, B_WIDTH), F32)],
        scratch_shapes=[pltpu.VMEM((nb, t + _CONV_PAD, B_WIDTH), F32),
                        pltpu.VMEM((rows, B_WIDTH), F32)],
        compiler_params=_params("parallel"),
        name="conformer_conv",
    )(z, conv0, lp['b_conv_w'], vec(lp['b_conv_b']), vec(lp['b_ln_g']), vec(lp['b_ln_b']),
      pw_bf, vec(lp['b_pw_b']))


def _lambda_from(lamv_ref, lam_init):
    lv = lamv_ref[...]
    s1 = jnp.sum(lv[0:1] * lv[1:2], axis=-1, keepdims=True)
    s2 = jnp.sum(lv[2:3] * lv[3:4], axis=-1, keepdims=True)
    return jnp.exp(s1) - jnp.exp(s2) + lam_init


def _sub_ln(o, sg, lam_init):
    o = o * lax.rsqrt(jnp.mean(o * o, axis=-1, keepdims=True) + EPS) * sg
    return o * (1.0 - lam_init)


def _map_masks(rows):
    lane = lax.broadcasted_iota(jnp.int32, (rows, LANES), 1)
    return lane < C_QK_DIM, lane >= C_QK_DIM


def _attn_prompt_kernel(q_ref, k_ref, v_ref, lamv_ref, sg_ref, o_ref, m_scr, l_scr, acc_scr,
                        *, tq, lam_init):
    qi = pl.program_id(1)
    kj = pl.program_id(2)

    @pl.when(kj == 0)
    def _():
        m_scr[...] = jnp.full(m_scr.shape, NEG_BIG, F32)
        l_scr[...] = jnp.zeros(l_scr.shape, F32)
        acc_scr[...] = jnp.zeros(acc_scr.shape, F32)

    @pl.when(kj <= qi)
    def _():
        qpos = qi * tq + lax.broadcasted_iota(jnp.int32, (tq, tq), 0)
        kpos = kj * tq + lax.broadcasted_iota(jnp.int32, (tq, tq), 1)
        dist = (qpos - kpos).astype(F32)
        causal = kpos <= qpos
        masks = _map_masks(tq)
        for h in range(C_HEADS):
            qh = q_ref[:, h * LANES:(h + 1) * LANES]
            kh = k_ref[:, h * LANES:(h + 1) * LANES]
            vh = v_ref[:, h * LANES:(h + 1) * LANES]
            for m in range(2):
                i = 2 * h + m
                qm = jnp.where(masks[m], qh, 0.0).astype(BF16)
                s = _dot_nt(qm, kh) - ALIBI_SLOPES[h] * dist
                s = jnp.where(causal, s, NEG_BIG)
                m_prev = m_scr[i]
                m_new = jnp.maximum(m_prev, jnp.max(s, axis=-1, keepdims=True))
                alpha = jnp.exp(m_prev - m_new)
                p = jnp.exp(s - m_new)
                l_scr[i] = alpha * l_scr[i] + jnp.sum(p, axis=-1, keepdims=True)
                acc_scr[i] = alpha * acc_scr[i] + _dot(p.astype(BF16), vh)
                m_scr[i] = m_new

    @pl.when(kj == qi)
    def _():
        lam = _lambda_from(lamv_ref, lam_init)
        for h in range(C_HEADS):
            o = acc_scr[2 * h] / l_scr[2 * h] - lam * (acc_scr[2 * h + 1] / l_scr[2 * h + 1])
            o_ref[:, h * LANES:(h + 1) * LANES] = _sub_ln(o, sg_ref[...], lam_init)


def _attn_prompt(qn, kb, vb, lamv, sg, n, t, lam_init):
    tq = _pick_tile(t, 704, 16)
    nq = t // tq
    return pl.pallas_call(
        functools.partial(_attn_prompt_kernel, tq=tq, lam_init=lam_init),
        grid=(n, nq, nq),
        in_specs=[pl.BlockSpec((tq, C_QK_COLS), lambda b, i, j: (b * nq + i, 0)),
                  pl.BlockSpec((tq, C_QK_COLS), lambda b, i, j: (b * nq + jnp.minimum(i, j), 0)),
                  pl.BlockSpec((tq, C_WIDTH), lambda b, i, j: (b * nq + jnp.minimum(i, j), 0)),
                  _full((4, C_QK_DIM)), _full((1, C_V_DIM))],
        out_specs=pl.BlockSpec((tq, C_WIDTH), lambda b, i, j: (b * nq + i, 0)),
        out_shape=jax.ShapeDtypeStruct((n * t, C_WIDTH), F32),
        scratch_shapes=[pltpu.VMEM((2 * C_HEADS, tq, 1), F32),
                        pltpu.VMEM((2 * C_HEADS, tq, 1), F32),
                        pltpu.VMEM((2 * C_HEADS, tq, C_V_DIM), F32)],
        compiler_params=_params("parallel", "parallel", "arbitrary"),
        name="attn_prompt",
    )(qn, kb, vb, lamv, sg)


def _attn_sample_kernel(pt_ref, q_ref, kn_ref, vn_ref, lamv_ref, sg_ref, tb_ref, cj_ref, tn_ref, *rest,
                        n_pages, t_s, lam_init):
    k_refs = rest[:n_pages]
    v_refs = rest[n_pages:2 * n_pages]
    o_ref = rest[2 * n_pages]
    del pt_ref
    rows = t_s * C_HEADS
    q = q_ref[0]
    masks = _map_masks(rows)
    qcat = jnp.concatenate([jnp.where(masks[0], q, 0.0), jnp.where(masks[1], q, 0.0),
                            jnp.zeros((LANES - 2 * rows, LANES), F32)], axis=0).astype(BF16)

    m_run = jnp.full((1, LANES), NEG_BIG, F32)
    l_run = jnp.zeros((1, LANES), F32)
    acc = jnp.zeros((C_V_DIM, LANES), F32)

    def update(kblk, vblk, bias, m_run, l_run, acc):
        s = _dot_nt(kblk.astype(BF16), qcat) + bias
        m_new = jnp.maximum(m_run, jnp.max(s, axis=0, keepdims=True))
        alpha = jnp.exp(m_run - m_new)
        p = jnp.exp(s - m_new)
        l_new = alpha * l_run + jnp.sum(p, axis=0, keepdims=True)
        acc_new = alpha * acc + _dot(vblk.T.astype(BF16), p.astype(BF16))
        return m_new, l_new, acc_new

    tbias = tb_ref[...]
    for j in range(n_pages):
        bias = tbias + cj_ref[j:j + 1, :]
        m_run, l_run, acc = update(k_refs[j][0], v_refs[j][0], bias, m_run, l_run, acc)
    m_run, l_run, acc = update(kn_ref[0], vn_ref[0], tn_ref[...], m_run, l_run, acc)

    lam = _lambda_from(lamv_ref, lam_init)
    o_all = (acc / l_run).T
    o = o_all[0:rows] - lam * o_all[rows:2 * rows]
    o_ref[0] = _sub_ln(o, sg_ref[...], lam_init)


def _sample_bias_tables(n_pages, t_s):
    rows = t_s * C_HEADS
    past = n_pages * PAGE_SIZE
    col = jnp.arange(LANES)
    used = col < 2 * rows
    c_r = col % rows
    c_t = c_r // C_HEADS
    c_h = c_r % C_HEADS
    slope = jnp.where(used, jnp.asarray(ALIBI_SLOPES, F32)[c_h], 0.0)
    prow = jnp.arange(PAGE_SIZE * C_HEADS)
    p_tok = prow // C_HEADS
    p_h = prow % C_HEADS
    ok = (p_h[:, None] == c_h[None, :]) | ~used[None, :]
    tb = jnp.where(ok, slope[None, :] * p_tok[:, None].astype(F32), NEG_BIG)
    starts = jnp.arange(n_pages) * PAGE_SIZE
    cj = -slope[None, :] * (past + c_t[None, :] - starts[:, None]).astype(F32)
    nrow = jnp.arange(rows)
    n_t = nrow // C_HEADS
    n_h = nrow % C_HEADS
    okn = ((n_h[:, None] == c_h[None, :]) & (n_t[:, None] <= c_t[None, :])) | ~used[None, :]
    tn = jnp.where(okn, -slope[None, :] * (c_t[None, :] - n_t[:, None]).astype(F32), NEG_BIG)
    return tb.astype(F32), cj.astype(F32), tn.astype(F32)


def _attn_sample(qn3, kn3, vn3, ck, cv, page_table, page_base, lamv, sg, tables, lam_init):
    n, rows, _ = qn3.shape
    n_pages = page_table.shape[1]
    pcols = PAGE_SIZE * C_HEADS
    assert 2 * rows <= LANES
    seq = lambda: pl.BlockSpec((1, rows, LANES), lambda i, pt: (i, 0, 0))
    page = lambda j: pl.BlockSpec((1, pcols, LANES), lambda i, pt, j=j: (page_base + pt[i, j], 0, 0))
    const = lambda a: pl.BlockSpec(a.shape, lambda i, pt: (0, 0))
    grid_spec = pltpu.PrefetchScalarGridSpec(
        num_scalar_prefetch=1,
        grid=(n,),
        in_specs=[seq(), seq(), seq(), const(lamv), const(sg)] + [const(a) for a in tables]
                 + [page(j) for j in range(n_pages)] + [page(j) for j in range(n_pages)],
        out_specs=seq(),
    )
    return pl.pallas_call(
        functools.partial(_attn_sample_kernel, n_pages=n_pages, t_s=rows // C_HEADS, lam_init=lam_init),
        grid_spec=grid_spec,
        out_shape=jax.ShapeDtypeStruct((n, rows, LANES), F32),
        compiler_params=_params("parallel"),
        name="attn_sample",
    )(page_table, qn3, kn3, vn3, lamv, sg, *tables, *([ck] * n_pages), *([cv] * n_pages))


def _out_proj_kernel(x_ref, y_ref, gb_ref, yb_ref, yc_ref, lg_ref, lb_ref, bd_ref, w_ref, o_ref):
    y = y_ref[...]
    bd = bd_ref[...]
    inv = 1.0 / A_HEAD_DIM
    yc = y - _seg_sum(y, bd) * inv
    yn = yc * lax.rsqrt(_seg_sum(yc * yc, bd) * inv + A_GN_EPS)
    ya = (yn * lg_ref[...] + lb_ref[...] + gb_ref[:, A_WIDTH:]) * gb_ref[:, :A_WIDTH]
    acc = _dot(ya.astype(BF16), w_ref[0:A_WIDTH, :])
    acc += _dot(yb_ref[...].astype(BF16), w_ref[A_WIDTH:A_WIDTH + B_WIDTH, :])
    acc += _dot(yc_ref[...].astype(BF16), w_ref[A_WIDTH + B_WIDTH:, :])
    o_ref[...] = x_ref[...] + acc


def _out_proj(x, y_t, gb, yb, yc, lp, bd256, w_bf):
    rows = x.shape[0]
    t, n, _ = y_t.shape
    tm = _seq_tile(t) or _pick_tile(rows, 512, 8)
    row = lambda c: pl.BlockSpec((tm, c), lambda i: (i, 0))
    vec = lambda a: a.reshape(1, -1)
    if _seq_tile(t):
        y, y_spec = y_t.reshape(t, n * A_WIDTH), _time_major_spec(tm, A_WIDTH, t // tm)
    else:
        y, y_spec = y_t.transpose(1, 0, 2).reshape(rows, A_WIDTH), row(A_WIDTH)
    return pl.pallas_call(
        _out_proj_kernel,
        grid=(rows // tm,),
        in_specs=[row(D_MODEL), y_spec, row(2 * A_WIDTH), row(B_WIDTH), row(C_WIDTH),
                  _full((1, A_WIDTH)), _full((1, A_WIDTH)), _full((2 * A_WIDTH, A_WIDTH)),
                  _full((D_MODEL, D_MODEL))],
        out_specs=row(D_MODEL),
        out_shape=jax.ShapeDtypeStruct((rows, D_MODEL), F32),
        compiler_params=_params("parallel"),
        name="out_proj",
    )(x, y, gb, yb, yc, vec(lp['a_lnx_g']), vec(lp['a_lnx_b']), bd256, w_bf)


GATE_LANE = 0
FLAG_LANE = N_EXPERTS
RANK_LANE = 2 * N_EXPERTS


def _router_kernel(x_ref, g_ref, r_ref, tri_ref, eye_ref, h_ref, rec_ref, rect_ref):
    x = x_ref[...]
    h = x * lax.rsqrt(jnp.mean(x * x, axis=-1, keepdims=True) + EPS) * g_ref[...]
    h_ref[...] = h.astype(BF16)
    logits = _dot_hi(h, r_ref[...])
    lane = lax.broadcasted_iota(jnp.int32, logits.shape, 1).astype(F32)
    lg = jnp.where(lane < N_EXPERTS, logits, NEG_BIG)
    m1 = jnp.max(lg, axis=-1, keepdims=True)
    i1 = jnp.min(jnp.where(lg == m1, lane, float(LANES)), axis=-1, keepdims=True)
    lg2 = jnp.where(lane == i1, NEG_BIG, lg)
    m2 = jnp.max(lg2, axis=-1, keepdims=True)
    i2 = jnp.min(jnp.where(lg2 == m2, lane, float(LANES)), axis=-1, keepdims=True)
    e = jnp.exp(m2 - m1)
    g1 = 1.0 / (1.0 + e)
    gates = jnp.where(lane == i1, g1, 0.0) + jnp.where(lane == i2, e * g1, 0.0)
    chosen = lambda off: jnp.where((lane == i1 + off) | (lane == i2 + off), 1.0, 0.0)
    rank = _dot(tri_ref[...], chosen(float(RANK_LANE)).astype(BF16))
    rec = gates + chosen(float(FLAG_LANE)) + rank
    rec_ref[...] = rec
    rect_ref[0] = lax.dot_general(eye_ref[...], rec, (((1,), (1,)), ((), ())),
                                  preferred_element_type=F32, precision=HIGHEST)


def _router(x, g, router_pad, tw):
    rows = x.shape[0]
    nw = rows // tw
    tri = (jnp.arange(tw)[:, None] > jnp.arange(tw)[None, :]).astype(BF16)
    eye = jnp.eye(LANES, dtype=F32)
    return pl.pallas_call(
        _router_kernel,
        grid=(nw,),
        in_specs=[pl.BlockSpec((tw, D_MODEL), lambda i: (i, 0)), _full((1, D_MODEL)),
                  _full((D_MODEL, LANES)), _full((tw, tw)), _full((LANES, LANES))],
        out_specs=[pl.BlockSpec((tw, D_MODEL), lambda i: (i, 0)),
                   pl.BlockSpec((tw, LANES), lambda i: (i, 0)),
                   pl.BlockSpec((1, LANES, tw), lambda i: (i, 0, 0))],
        out_shape=[jax.ShapeDtypeStruct((rows, D_MODEL), BF16),
                   jax.ShapeDtypeStruct((rows, LANES), F32),
                   jax.ShapeDtypeStruct((nw, LANES, tw), F32)],
        compiler_params=_params("parallel"),
        name="router",
    )(x, g, router_pad, tri, eye)


def _moe_kernel(h_ref, rec_ref, rect_ref, acc_ref, wg_ref, wu_ref, wd_ref, o_ref, *, cap):
    e = pl.program_id(0)
    tw = h_ref.shape[0]
    rec = rec_ref[...]
    rect = rect_ref[0]
    lane = lax.broadcasted_iota(jnp.int32, rec.shape, 1)
    sub = lax.broadcasted_iota(jnp.int32, rect.shape, 0)
    col = lambda off: jnp.sum(jnp.where(lane == off + e, rec, 0.0), axis=-1, keepdims=True)
    row = lambda off: jnp.sum(jnp.where(sub == off + e, rect, 0.0), axis=0, keepdims=True)
    gate_c, flag_c, rank_c = col(GATE_LANE), col(FLAG_LANE), col(RANK_LANE)
    flag_r, rank_r = row(FLAG_LANE), row(RANK_LANE)
    o_ref[...] = acc_ref[...]
    count = jnp.max(rank_c + flag_c)
    n_pass = sum((count > float(k * cap)).astype(jnp.int32) for k in range(-(-tw // cap)))
    slot_r = lax.broadcasted_iota(jnp.int32, (cap, tw), 0).astype(F32)
    slot_c = lax.broadcasted_iota(jnp.int32, (tw, cap), 1).astype(F32)

    def one_pass(p, carry):
        base = (p * cap).astype(F32)
        sel = jnp.where((rank_r - base == slot_r) & (flag_r > 0.0), 1.0, 0.0).astype(BF16)
        sel_t = jnp.where((rank_c - base == slot_c) & (flag_c > 0.0), 1.0, 0.0).astype(BF16)
        xs = _dot(sel, h_ref[...]).astype(BF16)
        a = _dot(xs, wg_ref[0])
        u = _dot(xs, wu_ref[0])
        y = _dot((a * _sigmoid(a) * u).astype(BF16), wd_ref[0])
        o_ref[...] += gate_c * _dot(sel_t, y.astype(BF16))
        return carry

    lax.fori_loop(0, n_pass, one_pass, 0)


def _moe_window(rows):
    best = None
    for d in range(16, min(rows // MOE_MIN_WINDOWS, MOE_MAX_WINDOW) + 1, 16):
        if rows % d == 0:
            best = d
    assert best is not None, rows
    return best


def _moe(h, rec, rect, x, wg, wu, wd, tw):
    rows = x.shape[0]
    nw = rows // tw
    n_e = wg.shape[0]
    assert nw >= MOE_MIN_WINDOWS
    cap = -(-(tw * TOP_K * MOE_CAP_SLACK_PCT) // (N_EXPERTS * 100 * 16)) * 16
    win = lambda c: pl.BlockSpec((tw, c), lambda e, w: (w, 0))
    weight = lambda a: pl.BlockSpec((1,) + a.shape[1:], lambda e, w: (e, 0, 0), pipeline_mode=pl.Buffered(1))
    return pl.pallas_call(
        functools.partial(_moe_kernel, cap=cap),
        grid=(n_e, nw),
        in_specs=[win(D_MODEL), win(LANES), pl.BlockSpec((1, LANES, tw), lambda e, w: (w, 0, 0)),
                  win(D_MODEL), weight(wg), weight(wu), weight(wd)],
        out_specs=win(D_MODEL),
        out_shape=jax.ShapeDtypeStruct((rows, D_MODEL), F32),
        input_output_aliases={3: 0},
        compiler_params=_params("arbitrary", "arbitrary"),
        name="moe",
    )(h, rec, rect, x, wg, wu, wd)


def _ffn_kernel(x_ref, g_ref, gate_ref, wg_ref, wu_ref, wd_ref, o_ref, h_scr, acc_scr):
    e = pl.program_id(1)
    f = pl.program_id(2)

    @pl.when((e == 0) & (f == 0))
    def _():
        x = x_ref[...]
        h = x * lax.rsqrt(jnp.mean(x * x, axis=-1, keepdims=True) + EPS) * g_ref[...]
        h_scr[...] = h.astype(BF16)
        acc_scr[...] = jnp.zeros(acc_scr.shape, F32)

    h = h_scr[...]
    a = _dot(h, wg_ref[0])
    u = _dot(h, wu_ref[0])
    act = (a * _sigmoid(a) * u).astype(BF16)
    lane = lax.broadcasted_iota(jnp.int32, gate_ref.shape, 1)
    gcol = jnp.sum(jnp.where(lane == e, gate_ref[...], 0.0), axis=-1, keepdims=True)
    acc_scr[...] += gcol * _dot(act, wd_ref[0])

    @pl.when((e == pl.num_programs(1) - 1) & (f == pl.num_programs(2) - 1))
    def _():
        o_ref[...] = x_ref[...] + acc_scr[...]


def _ffn(x, g, gates, wg, wu, wd):
    rows = x.shape[0]
    n_e = wg.shape[0]
    tm = _pick_tile(rows, 704, 16)
    tf = D_FF // 2
    return pl.pallas_call(
        _ffn_kernel,
        grid=(rows // tm, n_e, D_FF // tf),
        in_specs=[pl.BlockSpec((tm, D_MODEL), lambda i, e, f: (i, 0)),
                  pl.BlockSpec((1, D_MODEL), lambda i, e, f: (0, 0)),
                  pl.BlockSpec((tm, LANES), lambda i, e, f: (i, 0)),
                  pl.BlockSpec((1, D_MODEL, tf), lambda i, e, f: (e, 0, f)),
                  pl.BlockSpec((1, D_MODEL, tf), lambda i, e, f: (e, 0, f)),
                  pl.BlockSpec((1, tf, D_MODEL), lambda i, e, f: (e, f, 0))],
        out_specs=pl.BlockSpec((tm, D_MODEL), lambda i, e, f: (i, 0)),
        out_shape=jax.ShapeDtypeStruct((rows, D_MODEL), F32),
        scratch_shapes=[pltpu.VMEM((tm, D_MODEL), BF16), pltpu.VMEM((tm, D_MODEL), F32)],
        compiler_params=_params("parallel", "arbitrary", "arbitrary"),
        name="ffn",
    )(x, g, gates, wg, wu, wd)


def _wkv_to_pairs(s):
    n = s.shape[0]
    assert n % SCAN_SEQS == 0
    s = s.reshape(n // SCAN_SEQS, SCAN_SEQS, 2, 2, A_HEAD_DIM, A_HEAD_DIM)
    return s.transpose(0, 4, 2, 1, 3, 5).reshape(n // SCAN_SEQS, A_HEAD_DIM, 2 * SCAN_SEQS, LANES)


def _wkv_from_pairs(s):
    nblk = s.shape[0]
    s = s.reshape(nblk, A_HEAD_DIM, 2, SCAN_SEQS, 2, A_HEAD_DIM)
    return s.transpose(0, 3, 2, 4, 1, 5).reshape(nblk * SCAN_SEQS, A_HEADS, A_HEAD_DIM, A_HEAD_DIM)


def _mixers(x, n, t, shift0, wkv0, conv0, attend, lp, c, lam_init):
    z, qn, kn, kb, vb = _in_proj(x, lp['norm1_g'].reshape(1, -1), lp['w_in_bf'], lp['qg'], lp['kg'], c['bd512'])
    z3 = z.reshape(n, t, IN_COLS)
    za3 = z3[:, :, :A_COLS]
    prev = jnp.concatenate([shift0[:, None, :], za3[:, :-1]], axis=1).reshape(n * t, A_COLS)
    sc_t, gb = _rwkv_pre(z, prev, lp, c['bd256'], n, t)
    y_t, s1 = _rwkv_scan(sc_t, _wkv_to_pairs(wkv0), c['bd128'], c['eye'])
    yb, conv1 = _conformer_conv(z, conv0, lp, lp['pw_bf'], n, t)
    yc = attend(qn, kn, kb, vb, z)
    x = _out_proj(x, y_t, gb, yb, yc, lp, c['bd256'], lp['w_out_bf'])
    k_rows = kn.reshape(n, t, C_HEADS, 2 * C_QK_DIM)
    v_rows = z3[:, :, V_OFF:].reshape(n, t, C_HEADS, C_V_DIM)
    return x, k_rows, v_rows, _wkv_from_pairs(s1), za3[:, -1], conv1


def kernel(x_prompt, x_sample, cache_k, cache_v, page_table, state_wkv, state_shift, state_conv, meta_tokens, norm1_g, norm2_g, w_in, w_out, a_mu, a_w0, a_w_up, a_a0, a_a_up, a_g_up, a_k_k, a_k_a, a_r_k, a_lnx_g, a_lnx_b, b_conv_w, b_conv_b, b_ln_g, b_ln_b, b_pw_w, b_pw_b, c_qn_g, c_kn_g, c_lam_q1, c_lam_k1, c_lam_q2, c_lam_k2, c_subln_g, ffn_w_gate, ffn_w_up, ffn_w_down, moe_router, moe_w_gate, moe_w_up, moe_w_down):
    depth = w_in.shape[0]
    b, seq, _ = x_prompt.shape
    n_s, t_s, _ = x_sample.shape
    t_p = seq + N_META
    n_pool = cache_k.shape[1]
    pcols = PAGE_SIZE * C_HEADS

    consts = {
        'bd512': _block_diag2(C_QK_COLS),
        'bd256': _block_diag2(A_WIDTH),
        'bd128': _block_diag2(LANES),
        'eye': jnp.broadcast_to(
            (jnp.arange(LANES)[None, None, :] % A_HEAD_DIM == jnp.arange(A_HEAD_DIM)[:, None, None]).astype(F32),
            (A_HEAD_DIM, 2 * SCAN_SEQS, LANES)),
    }
    meta = jnp.broadcast_to(meta_tokens.astype(F32)[None], (b, N_META, D_MODEL))
    xp = jnp.concatenate([meta, x_prompt], axis=1).reshape(b * t_p, D_MODEL)
    xs = x_sample.reshape(n_s * t_s, D_MODEL)
    ck_all = cache_k.reshape(depth * n_pool, pcols, LANES)
    cv_all = cache_v.reshape(depth * n_pool, pcols, LANES)
    sample_tables = _sample_bias_tables(page_table.shape[1], t_s)

    outs = [[] for _ in range(10)]
    for l in range(depth):
        lam_init = 0.8 - 0.6 * math.exp(-0.3 * l)
        lp = {
            'norm1_g': norm1_g[l], 'w_in_bf': w_in[l].astype(BF16), 'w_out_bf': w_out[l].astype(BF16),
            'a_mu': a_mu[l], 'a_w0': a_w0[l], 'a_w_up': a_w_up[l], 'a_a0': a_a0[l], 'a_a_up': a_a_up[l],
            'a_g_up': a_g_up[l], 'a_k_k': a_k_k[l], 'a_k_a': a_k_a[l], 'a_r_k': a_r_k[l],
            'a_lnx_g': a_lnx_g[l], 'a_lnx_b': a_lnx_b[l],
            'b_conv_w': b_conv_w[l], 'b_conv_b': b_conv_b[l], 'b_ln_g': b_ln_g[l], 'b_ln_b': b_ln_b[l],
            'pw_bf': b_pw_w[l].astype(BF16), 'b_pw_b': b_pw_b[l],
            'qg': jnp.tile(c_qn_g[l], 2 * C_HEADS).reshape(1, -1),
            'kg': jnp.tile(c_kn_g[l], 2 * C_HEADS).reshape(1, -1),
        }
        lamv = jnp.stack([c_lam_q1[l], c_lam_k1[l], c_lam_q2[l], c_lam_k2[l]]).astype(F32)
        sg = c_subln_g[l].reshape(1, -1)

        def attend_prompt(qn, kn, kb, vb, z, lamv=lamv, sg=sg, lam_init=lam_init):
            return _attn_prompt(qn, kb, vb, lamv, sg, b, t_p, lam_init)

        def attend_sample(qn, kn, kb, vb, z, l=l, lamv=lamv, sg=sg, lam_init=lam_init):
            rows = t_s * C_HEADS
            qn3 = qn.reshape(n_s, rows, LANES)
            kn3 = kn.reshape(n_s, rows, LANES)
            vn3 = z[:, V_OFF:].reshape(n_s, rows, LANES)
            o = _attn_sample(qn3, kn3, vn3, ck_all, cv_all, page_table, l * n_pool, lamv, sg, sample_tables,
                             lam_init)
            return o.reshape(n_s * t_s, C_WIDTH)

        xp, k_p, v_p, wkv_p, shift_p, conv_p = _mixers(
            xp, b, t_p, jnp.zeros((b, A_COLS), F32), jnp.zeros((b, A_HEADS, A_HEAD_DIM, A_HEAD_DIM), F32),
            jnp.zeros((b, CONV_HIST, B_WIDTH), F32), attend_prompt, lp, consts, lam_init)
        xs, k_s, v_s, wkv_s, shift_s, conv_s = _mixers(
            xs, n_s, t_s, state_shift[l], state_wkv[l], state_conv[l], attend_sample, lp, consts, lam_init)

        i = l // 2
        g2 = norm2_g[l].reshape(1, -1)
        if l % 2 == 0:
            wg = ffn_w_gate[i].astype(BF16)[None]
            wu = ffn_w_up[i].astype(BF16)[None]
            wd = ffn_w_down[i].astype(BF16)[None]
            xp = _ffn(xp, g2, jnp.ones((xp.shape[0], LANES), F32), wg, wu, wd)
            xs = _ffn(xs, g2, jnp.ones((xs.shape[0], LANES), F32), wg, wu, wd)
        else:
            wg = moe_w_gate[i].astype(BF16)
            wu = moe_w_up[i].astype(BF16)
            wd = moe_w_down[i].astype(BF16)
            router_pad = jnp.pad(moe_router[i], ((0, 0), (0, LANES - N_EXPERTS)))
            tw_p, tw_s = _moe_window(xp.shape[0]), _moe_window(xs.shape[0])
            xp = _moe(*_router(xp, g2, router_pad, tw_p), xp, wg, wu, wd, tw_p)
            xs = _moe(*_router(xs, g2, router_pad, tw_s), xs, wg, wu, wd, tw_s)

        for lst, val in zip(outs, (k_p, v_p, k_s, v_s, wkv_p, wkv_s, shift_p, shift_s, conv_p, conv_s)):
            lst.append(val)

    y_prompt = xp.reshape(b, t_p, D_MODEL)[:, N_META:]
    y_sample = xs.reshape(n_s, t_s, D_MODEL)
    return (y_prompt, y_sample) + tuple(jnp.stack(o) for o in outs)
```

```python
import functools
import math

import jax
import jax.numpy as jnp
from jax import lax
from jax.experimental import pallas as pl
from jax.experimental.pallas import tpu as pltpu

F32 = jnp.float32
BF16 = jnp.bfloat16
HIGHEST = lax.Precision.HIGHEST

D_MODEL = 1024
EPS = 1e-6
N_META = 16
A_HEAD_DIM = 64
A_WIDTH = 256
A_HEADS = 4
A_W_LORA = 64
A_A_LORA = 64
A_G_LORA = 128
A_COLS = 1024
A_GN_EPS = 64e-5
B_WIDTH = 256
B_COLS = 512
CONV_WIDTH = 31
CONV_HIST = CONV_WIDTH - 1
C_QK_DIM = 64
C_V_DIM = 128
C_WIDTH = 512
C_HEADS = 4
C_QK_COLS = 512
IN_COLS = 3072
Q_OFF = A_COLS + B_COLS
K_OFF = Q_OFF + C_QK_COLS
V_OFF = K_OFF + C_QK_COLS
D_FF = 2816
N_EXPERTS = 8
PAGE_SIZE = 128
LANES = 128
NEG_BIG = -1e30
ALIBI_SLOPES = tuple(float((2.0 ** (-8.0 / C_HEADS)) ** (h + 1)) for h in range(C_HEADS))
EXP_NEG_HALF = math.exp(-0.5)
SCAN_SEQS = 8
SCAN_COLS = 7 * A_WIDTH
SCAN_SUB = 8
TOP_K = 2
MIN_TIME_MAJOR_T = 64
MOE_MIN_WINDOWS = 3
MOE_MAX_WINDOW = 704
MOE_CAP_SLACK_PCT = 111
VMEM_LIMIT = 56 * 1024 * 1024


def _params(*sem):
    return pltpu.CompilerParams(dimension_semantics=sem, vmem_limit_bytes=VMEM_LIMIT)


def _pick_tile(n, target, mult=8):
    best = None
    for d in range(mult, min(n, target) + 1, mult):
        if n % d == 0:
            best = d
    return n if best is None else best


def _dot(a, b):
    return jnp.dot(a, b, preferred_element_type=F32)


def _dot_hi(a, b):
    return jnp.dot(a, b, preferred_element_type=F32, precision=HIGHEST)


def _dot_nt(a, b):
    return lax.dot_general(a, b, (((1,), (1,)), ((), ())), preferred_element_type=F32)


def _sigmoid(x):
    return 1.0 / (1.0 + jnp.exp(-x))


def _seg_sum(x, bd2):
    hi = x.astype(BF16)
    lo = (x - hi.astype(F32)).astype(BF16)
    return _dot(jnp.concatenate([hi, lo], axis=-1), bd2)


def _block_diag2(width, seg=64):
    r = jnp.arange(width) // seg
    bd = (r[:, None] == r[None, :]).astype(BF16)
    return jnp.concatenate([bd, bd], axis=0)


def _full(shape):
    nd = len(shape)
    return pl.BlockSpec(shape, lambda *_: (0,) * nd)


def _in_proj_kernel(x_ref, g_ref, w_ref, qg_ref, kg_ref, bd_ref, z_ref, qn_ref, kn_ref, kb_ref, vb_ref):
    x = x_ref[...]
    h = x * lax.rsqrt(jnp.mean(x * x, axis=-1, keepdims=True) + EPS) * g_ref[...]
    z = _dot(h.astype(BF16), w_ref[...])
    z_ref[...] = z
    q = z[:, Q_OFF:Q_OFF + C_QK_COLS]
    k = z[:, K_OFF:K_OFF + C_QK_COLS]
    bd = bd_ref[...]
    inv = 1.0 / C_QK_DIM
    qn = q * lax.rsqrt(_seg_sum(q * q, bd) * inv + EPS) * qg_ref[...]
    kn = k * lax.rsqrt(_seg_sum(k * k, bd) * inv + EPS) * kg_ref[...]
    qn_ref[...] = qn * (C_QK_DIM ** -0.5)
    kn_ref[...] = kn
    kb_ref[...] = kn.astype(BF16)
    vb_ref[...] = z[:, V_OFF:].astype(BF16)


def _in_proj(x, g, w_bf, qg, kg, bd512):
    rows = x.shape[0]
    tm = _pick_tile(rows, 256, 16)
    row = lambda c: pl.BlockSpec((tm, c), lambda i: (i, 0))
    return pl.pallas_call(
        _in_proj_kernel,
        grid=(rows // tm,),
        in_specs=[row(D_MODEL), _full((1, D_MODEL)), _full((D_MODEL, IN_COLS)),
                  _full((1, C_QK_COLS)), _full((1, C_QK_COLS)), _full((2 * C_QK_COLS, C_QK_COLS))],
        out_specs=[row(IN_COLS), row(C_QK_COLS), row(C_QK_COLS), row(C_QK_COLS), row(C_WIDTH)],
        out_shape=[jax.ShapeDtypeStruct((rows, IN_COLS), F32),
                   jax.ShapeDtypeStruct((rows, C_QK_COLS), F32),
                   jax.ShapeDtypeStruct((rows, C_QK_COLS), F32),
                   jax.ShapeDtypeStruct((rows, C_QK_COLS), BF16),
                   jax.ShapeDtypeStruct((rows, C_WIDTH), BF16)],
        compiler_params=_params("parallel"),
        name="in_proj",
    )(x, g, w_bf, qg, kg, bd512)


def _rwkv_pre_kernel(za_ref, prev_ref, mu_ref, w0_ref, wup_ref, a0_ref, aup_ref, gup_ref,
                     kk_ref, ka_ref, rk_ref, bd_ref, sc_ref, gb_ref):
    za = za_ref[...]
    zs = za + mu_ref[...] * (prev_ref[...] - za)
    w = A_WIDTH
    r = zs[:, 0:w]
    k = zs[:, w:2 * w]
    v = zs[:, 2 * w:3 * w]
    wd = zs[:, 3 * w:3 * w + A_W_LORA]
    ad = zs[:, 3 * w + A_W_LORA:3 * w + A_W_LORA + A_A_LORA]
    gd = zs[:, 3 * w + A_W_LORA + A_A_LORA:]
    lw = w0_ref[...] + _dot_hi(jnp.tanh(wd), wup_ref[...])
    decay = jnp.exp(-EXP_NEG_HALF * _sigmoid(lw))
    a = _sigmoid(a0_ref[...] + _dot_hi(ad, aup_ref[...]))
    g = _dot_hi(_sigmoid(gd), gup_ref[...])
    bd = bd_ref[...]
    kk = k * kk_ref[...]
    k2 = k * (1.0 + (a - 1.0) * ka_ref[...])
    kk = kk * lax.rsqrt(jnp.maximum(_seg_sum(kk * kk, bd), 1e-24))
    bonus = _seg_sum(r * k2 * rk_ref[...], bd) * v
    b = kk * a
    sc_ref[:, 0:w] = decay
    sc_ref[:, w:2 * w] = kk
    sc_ref[:, 2 * w:3 * w] = b
    sc_ref[:, 3 * w:4 * w] = k2
    sc_ref[:, 4 * w:5 * w] = v
    sc_ref[:, 5 * w:6 * w] = decay * r - kk * _seg_sum(b * r, bd)
    sc_ref[:, 6 * w:7 * w] = _seg_sum(k2 * r, bd)
    gb_ref[:, 0:w] = g
    gb_ref[:, w:2 * w] = bonus


def _time_major_spec(tm, cols, tiles_per_seq):
    return pl.BlockSpec((tm, cols), lambda i: (i % tiles_per_seq, i // tiles_per_seq))


def _seq_tile(t):
    return _pick_tile(t, 512, 8) if t >= MIN_TIME_MAJOR_T else None


def _rwkv_pre(z, prev, lp, bd256, n, t):
    rows = z.shape[0]
    tm = _seq_tile(t) or _pick_tile(rows, 512, 8)
    row = lambda c: pl.BlockSpec((tm, c), lambda i: (i, 0))
    vec = lambda a: a.reshape(1, -1)
    args = (z, prev, vec(lp['a_mu']), vec(lp['a_w0']), lp['a_w_up'], vec(lp['a_a0']), lp['a_a_up'],
            lp['a_g_up'], vec(lp['a_k_k']), vec(lp['a_k_a']), vec(lp['a_r_k']), bd256)
    in_specs = [row(A_COLS), row(A_COLS)] + [_full(a.shape) for a in args[2:]]
    if _seq_tile(t):
        sc_spec, sc_shape = _time_major_spec(tm, SCAN_COLS, t // tm), (t, n * SCAN_COLS)
    else:
        sc_spec, sc_shape = row(SCAN_COLS), (rows, SCAN_COLS)
    sc, gb = pl.pallas_call(
        _rwkv_pre_kernel,
        grid=(rows // tm,),
        in_specs=in_specs,
        out_specs=[sc_spec, row(2 * A_WIDTH)],
        out_shape=[jax.ShapeDtypeStruct(sc_shape, F32),
                   jax.ShapeDtypeStruct((rows, 2 * A_WIDTH), F32)],
        compiler_params=_params("parallel"),
        name="rwkv_pre",
    )(*args)
    if _seq_tile(t):
        return sc.reshape(t, n, SCAN_COLS), gb
    return sc.reshape(n, t, SCAN_COLS).transpose(1, 0, 2), gb


def _rwkv_scan_kernel(x_ref, s0_ref, bd_ref, eye_ref, eyeb_ref, y_ref, s1_ref, s_scr, vb_scr, *, tb, sub):
    nseq = x_ref.shape[1]
    npair = 2 * nseq
    w = A_WIDTH
    rows = A_HEAD_DIM * npair

    @pl.when(pl.program_id(1) == 0)
    def _():
        s_scr[...] = s0_ref[0]

    bd2 = bd_ref[...]
    bd1 = bd2[0:LANES]
    eye_bf = eyeb_ref[...]

    def rowvec(xt, c):
        return jnp.concatenate([xt[:, c:c + LANES], xt[:, c + LANES:c + 2 * LANES]], axis=0)

    def value_bcast(t_src, slot, row):
        v = rowvec(x_ref[t_src], 4 * w).astype(BF16)
        vb = _dot((eye_bf * v[None]).reshape(rows, LANES), bd1)
        vb_scr[slot, row] = vb.reshape(A_HEAD_DIM, npair, LANES)

    def seg1(a):
        nv = a.shape[0]
        return _dot(a.reshape(nv * npair, LANES).astype(BF16), bd1).reshape(nv, npair, LANES)

    for tt in range(sub):
        value_bcast(tt, 0, tt)

    def step(t, carry):
        slot = (t // sub) % 2
        row = t % sub
        value_bcast(jnp.minimum(t + sub, tb - 1), 1 - slot, row)
        xt = x_ref[t]
        dec, kk, b, k, v, rq, kr = (rowvec(xt, i * w) for i in range(7))
        s = s_scr[...]
        sa = seg1(s * kk[None])
        yq = seg1(s * rq[None])
        s_scr[...] = s * dec[None] - sa * b[None] + vb_scr[slot, row] * k[None]
        y = jnp.sum(yq * eye_ref[...], axis=0) + v * kr
        y_ref[t] = jnp.concatenate([y[0:nseq], y[nseq:npair]], axis=-1)
        return carry

    lax.fori_loop(0, tb, step, 0, unroll=2)

    @pl.when(pl.program_id(1) == pl.num_programs(1) - 1)
    def _():
        s1_ref[0] = s_scr[...]


def _rwkv_scan(sc_t, s0, bd128, eye):
    t, n, _ = sc_t.shape
    nb = s0.shape[2] // 2
    tb = _pick_tile(t, 64, SCAN_SUB)
    assert tb % SCAN_SUB == 0
    state_spec = pl.BlockSpec((1, A_HEAD_DIM, 2 * nb, LANES), lambda i, j: (i, 0, 0, 0))
    return pl.pallas_call(
        functools.partial(_rwkv_scan_kernel, tb=tb, sub=SCAN_SUB),
        grid=(n // nb, t // tb),
        in_specs=[pl.BlockSpec((tb, nb, SCAN_COLS), lambda i, j: (j, i, 0)),
                  state_spec,
                  _full((2 * LANES, LANES)), _full((A_HEAD_DIM, 2 * nb, LANES)),
                  _full((A_HEAD_DIM, 2 * nb, LANES))],
        out_specs=[pl.BlockSpec((tb, nb, A_WIDTH), lambda i, j: (j, i, 0)), state_spec],
        out_shape=[jax.ShapeDtypeStruct((t, n, A_WIDTH), F32),
                   jax.ShapeDtypeStruct(s0.shape, F32)],
        scratch_shapes=[pltpu.VMEM((A_HEAD_DIM, 2 * nb, LANES), F32),
                        pltpu.VMEM((2, SCAN_SUB, A_HEAD_DIM, 2 * nb, LANES), F32)],
        compiler_params=_params("parallel", "arbitrary"),
        name="rwkv_scan",
    )(sc_t, s0, bd128, eye, eye.astype(BF16))


_CONV_PAD = 32


def _conv_kernel(zb_ref, c0_ref, cw_ref, cb_ref, lg_ref, lb_ref, pw_ref, pb_ref,
                 y_ref, c1_ref, ext_scr, h_scr, *, nb, t, tc):
    cw = cw_ref[...]
    for s in range(nb):
        zb = zb_ref[s * t:(s + 1) * t, :]
        u = zb[:, :B_WIDTH] * _sigmoid(zb[:, B_WIDTH:])
        ext_scr[s, _CONV_PAD - CONV_HIST:_CONV_PAD, :] = c0_ref[s]
        ext_scr[s, _CONV_PAD:_CONV_PAD + t, :] = u
        c1_ref[s] = ext_scr[s, t + _CONV_PAD - CONV_HIST:t + _CONV_PAD, :]

        def chunk(c, carry, s=s):
            base = pl.multiple_of(c * tc, 8)
            win = ext_scr[s, pl.ds(base, tc + _CONV_PAD), :]
            acc = jnp.zeros((tc, B_WIDTH), F32)
            for j in range(CONV_WIDTH):
                o = _CONV_PAD - CONV_HIST + j
                acc = acc + win[o:o + tc, :] * cw[j:j + 1, :]
            acc = acc + cb_ref[...]
            xc = acc - jnp.mean(acc, axis=-1, keepdims=True)
            hn = xc * lax.rsqrt(jnp.mean(xc * xc, axis=-1, keepdims=True) + EPS)
            hn = hn * lg_ref[...] + lb_ref[...]
            hn = hn * _sigmoid(hn)
            h_scr[pl.ds(pl.multiple_of(s * t + c * tc, 8), tc), :] = hn
            return carry

        lax.fori_loop(0, t // tc, chunk, 0)
    y_ref[...] = _dot(h_scr[...].astype(BF16), pw_ref[...]) + pb_ref[...]


def _conformer_conv(z, conv0, lp, pw_bf, n, t):
    nb = 1 if t >= 64 else _pick_tile(n, 16, 1)
    tc = _pick_tile(t, 48, 8)
    rows = nb * t
    vec = lambda a: a.reshape(1, -1)
    return pl.pallas_call(
        functools.partial(_conv_kernel, nb=nb, t=t, tc=tc),
        grid=(n // nb,),
        in_specs=[pl.BlockSpec((rows, B_COLS), lambda i: (i, A_COLS // B_COLS)),
                  pl.BlockSpec((nb, CONV_HIST, B_WIDTH), lambda i: (i, 0, 0)),
                  _full((CONV_WIDTH, B_WIDTH)), _full((1, B_WIDTH)), _full((1, B_WIDTH)),
                  _full((1, B_WIDTH)), _full((B_WIDTH, B_WIDTH)), _full((1, B_WIDTH))],
        out_specs=[pl.BlockSpec((rows, B_WIDTH), lambda i: (i, 0)),
                   pl.BlockSpec((nb, CONV_HIST, B_WIDTH), lambda i: (i, 0, 0))],
        out_shape=[jax.ShapeDtypeStruct((n * t, B_WIDTH), F32),
                   jax.ShapeDtypeStruct((n, CONV_HIST, B_WIDTH), F32)],
        scratch_shapes=[pltpu.VMEM((nb, t + _CONV_PAD, B_WIDTH), F32),
                        pltpu.VMEM((rows, B_WIDTH), F32)],
        compiler_params=_params("parallel"),
        name="conformer_conv",
    )(z, conv0, lp['b_conv_w'], vec(lp['b_conv_b']), vec(lp['b_ln_g']), vec(lp['b_ln_b']),
      pw_bf, vec(lp['b_pw_b']))


def _lambda_from(lamv_ref, lam_init):
    lv = lamv_ref[...]
    s1 = jnp.sum(lv[0:1] * lv[1:2], axis=-1, keepdims=True)
    s2 = jnp.sum(lv[2:3] * lv[3:4], axis=-1, keepdims=True)
    return jnp.exp(s1) - jnp.exp(s2) + lam_init


def _sub_ln(o, sg, lam_init):
    o = o * lax.rsqrt(jnp.mean(o * o, axis=-1, keepdims=True) + EPS) * sg
    return o * (1.0 - lam_init)


def _map_masks(rows):
    lane = lax.broadcasted_iota(jnp.int32, (rows, LANES), 1)
    return lane < C_QK_DIM, lane >= C_QK_DIM


def _attn_prompt_kernel(q_ref, k_ref, v_ref, lamv_ref, sg_ref, o_ref, m_scr, l_scr, acc_scr,
                        *, tq, lam_init):
    qi = pl.program_id(1)
    kj = pl.program_id(2)

    @pl.when(kj == 0)
    def _():
        m_scr[...] = jnp.full(m_scr.shape, NEG_BIG, F32)
        l_scr[...] = jnp.zeros(l_scr.shape, F32)
        acc_scr[...] = jnp.zeros(acc_scr.shape, F32)

    @pl.when(kj <= qi)
    def _():
        qpos = qi * tq + lax.broadcasted_iota(jnp.int32, (tq, tq), 0)
        kpos = kj * tq + lax.broadcasted_iota(jnp.int32, (tq, tq), 1)
        dist = (qpos - kpos).astype(F32)
        causal = kpos <= qpos
        masks = _map_masks(tq)
        for h in range(C_HEADS):
            qh = q_ref[:, h * LANES:(h + 1) * LANES]
            kh = k_ref[:, h * LANES:(h + 1) * LANES]
            vh = v_ref[:, h * LANES:(h + 1) * LANES]
            for m in range(2):
                i = 2 * h + m
                qm = jnp.where(masks[m], qh, 0.0).astype(BF16)
                s = _dot_nt(qm, kh) - ALIBI_SLOPES[h] * dist
                s = jnp.where(causal, s, NEG_BIG)
                m_prev = m_scr[i]
                m_new = jnp.maximum(m_prev, jnp.max(s, axis=-1, keepdims=True))
                alpha = jnp.exp(m_prev - m_new)
                p = jnp.exp(s - m_new)
                l_scr[i] = alpha * l_scr[i] + jnp.sum(p, axis=-1, keepdims=True)
                acc_scr[i] = alpha * acc_scr[i] + _dot(p.astype(BF16), vh)
                m_scr[i] = m_new

    @pl.when(kj == qi)
    def _():
        lam = _lambda_from(lamv_ref, lam_init)
        for h in range(C_HEADS):
            o = acc_scr[2 * h] / l_scr[2 * h] - lam * (acc_scr[2 * h + 1] / l_scr[2 * h + 1])
            o_ref[:, h * LANES:(h + 1) * LANES] = _sub_ln(o, sg_ref[...], lam_init)


def _attn_prompt(qn, kb, vb, lamv, sg, n, t, lam_init):
    tq = _pick_tile(t, 704, 16)
    nq = t // tq
    return pl.pallas_call(
        functools.partial(_attn_prompt_kernel, tq=tq, lam_init=lam_init),
        grid=(n, nq, nq),
        in_specs=[pl.BlockSpec((tq, C_QK_COLS), lambda b, i, j: (b * nq + i, 0)),
                  pl.BlockSpec((tq, C_QK_COLS), lambda b, i, j: (b * nq + jnp.minimum(i, j), 0)),
                  pl.BlockSpec((tq, C_WIDTH), lambda b, i, j: (b * nq + jnp.minimum(i, j), 0)),
                  _full((4, C_QK_DIM)), _full((1, C_V_DIM))],
        out_specs=pl.BlockSpec((tq, C_WIDTH), lambda b, i, j: (b * nq + i, 0)),
        out_shape=jax.ShapeDtypeStruct((n * t, C_WIDTH), F32),
        scratch_shapes=[pltpu.VMEM((2 * C_HEADS, tq, 1), F32),
                        pltpu.VMEM((2 * C_HEADS, tq, 1), F32),
                        pltpu.VMEM((2 * C_HEADS, tq, C_V_DIM), F32)],
        compiler_params=_params("parallel", "parallel", "arbitrary"),
        name="attn_prompt",
    )(qn, kb, vb, lamv, sg)


def _attn_sample_kernel(pt_ref, q_ref, kn_ref, vn_ref, lamv_ref, sg_ref, tb_ref, cj_ref, tn_ref, *rest,
                        n_pages, t_s, lam_init):
    k_refs = rest[:n_pages]
    v_refs = rest[n_pages:2 * n_pages]
    o_ref = rest[2 * n_pages]
    del pt_ref
    rows = t_s * C_HEADS
    q = q_ref[0]
    masks = _map_masks(rows)
    qcat = jnp.concatenate([jnp.where(masks[0], q, 0.0), jnp.where(masks[1], q, 0.0),
                            jnp.zeros((LANES - 2 * rows, LANES), F32)], axis=0).astype(BF16)

    m_run = jnp.full((1, LANES), NEG_BIG, F32)
    l_run = jnp.zeros((1, LANES), F32)
    acc = jnp.zeros((C_V_DIM, LANES), F32)

    def update(kblk, vblk, bias, m_run, l_run, acc):
        s = _dot_nt(kblk.astype(BF16), qcat) + bias
        m_new = jnp.maximum(m_run, jnp.max(s, axis=0, keepdims=True))
        alpha = jnp.exp(m_run - m_new)
        p = jnp.exp(s - m_new)
        l_new = alpha * l_run + jnp.sum(p, axis=0, keepdims=True)
        acc_new = alpha * acc + _dot(vblk.T.astype(BF16), p.astype(BF16))
        return m_new, l_new, acc_new

    tbias = tb_ref[...]
    for j in range(n_pages):
        bias = tbias + cj_ref[j:j + 1, :]
        m_run, l_run, acc = update(k_refs[j][0], v_refs[j][0], bias, m_run, l_run, acc)
    m_run, l_run, acc = update(kn_ref[0], vn_ref[0], tn_ref[...], m_run, l_run, acc)

    lam = _lambda_from(lamv_ref, lam_init)
    o_all = (acc / l_run).T
    o = o_all[0:rows] - lam * o_all[rows:2 * rows]
    o_ref[0] = _sub_ln(o, sg_ref[...], lam_init)


def _sample_bias_tables(n_pages, t_s):
    rows = t_s * C_HEADS
    past = n_pages * PAGE_SIZE
    col = jnp.arange(LANES)
    used = col < 2 * rows
    c_r = col % rows
    c_t = c_r // C_HEADS
    c_h = c_r % C_HEADS
    slope = jnp.where(used, jnp.asarray(ALIBI_SLOPES, F32)[c_h], 0.0)
    prow = jnp.arange(PAGE_SIZE * C_HEADS)
    p_tok = prow // C_HEADS
    p_h = prow % C_HEADS
    ok = (p_h[:, None] == c_h[None, :]) | ~used[None, :]
    tb = jnp.where(ok, slope[None, :] * p_tok[:, None].astype(F32), NEG_BIG)
    starts = jnp.arange(n_pages) * PAGE_SIZE
    cj = -slope[None, :] * (past + c_t[None, :] - starts[:, None]).astype(F32)
    nrow = jnp.arange(rows)
    n_t = nrow // C_HEADS
    n_h = nrow % C_HEADS
    okn = ((n_h[:, None] == c_h[None, :]) & (n_t[:, None] <= c_t[None, :])) | ~used[None, :]
    tn = jnp.where(okn, -slope[None, :] * (c_t[None, :] - n_t[:, None]).astype(F32), NEG_BIG)
    return tb.astype(F32), cj.astype(F32), tn.astype(F32)


def _attn_sample(qn3, kn3, vn3, ck, cv, page_table, page_base, lamv, sg, tables, lam_init):
    n, rows, _ = qn3.shape
    n_pages = page_table.shape[1]
    pcols = PAGE_SIZE * C_HEADS
    assert 2 * rows <= LANES
    seq = lambda: pl.BlockSpec((1, rows, LANES), lambda i, pt: (i, 0, 0))
    page = lambda j: pl.BlockSpec((1, pcols, LANES), lambda i, pt, j=j: (page_base + pt[i, j], 0, 0))
    const = lambda a: pl.BlockSpec(a.shape, lambda i, pt: (0, 0))
    grid_spec = pltpu.PrefetchScalarGridSpec(
        num_scalar_prefetch=1,
        grid=(n,),
        in_specs=[seq(), seq(), seq(), const(lamv), const(sg)] + [const(a) for a in tables]
                 + [page(j) for j in range(n_pages)] + [page(j) for j in range(n_pages)],
        out_specs=seq(),
    )
    return pl.pallas_call(
        functools.partial(_attn_sample_kernel, n_pages=n_pages, t_s=rows // C_HEADS, lam_init=lam_init),
        grid_spec=grid_spec,
        out_shape=jax.ShapeDtypeStruct((n, rows, LANES), F32),
        compiler_params=_params("parallel"),
        name="attn_sample",
    )(page_table, qn3, kn3, vn3, lamv, sg, *tables, *([ck] * n_pages), *([cv] * n_pages))


def _out_proj_kernel(x_ref, y_ref, gb_ref, yb_ref, yc_ref, lg_ref, lb_ref, bd_ref, w_ref, o_ref):
    y = y_ref[...]
    bd = bd_ref[...]
    inv = 1.0 / A_HEAD_DIM
    yc = y - _seg_sum(y, bd) * inv
    yn = yc * lax.rsqrt(_seg_sum(yc * yc, bd) * inv + A_GN_EPS)
    ya = (yn * lg_ref[...] + lb_ref[...] + gb_ref[:, A_WIDTH:]) * gb_ref[:, :A_WIDTH]
    acc = _dot(ya.astype(BF16), w_ref[0:A_WIDTH, :])
    acc += _dot(yb_ref[...].astype(BF16), w_ref[A_WIDTH:A_WIDTH + B_WIDTH, :])
    acc += _dot(yc_ref[...].astype(BF16), w_ref[A_WIDTH + B_WIDTH:, :])
    o_ref[...] = x_ref[...] + acc


def _out_proj(x, y_t, gb, yb, yc, lp, bd256, w_bf):
    rows = x.shape[0]
    t, n, _ = y_t.shape
    tm = _seq_tile(t) or _pick_tile(rows, 512, 8)
    row = lambda c: pl.BlockSpec((tm, c), lambda i: (i, 0))
    vec = lambda a: a.reshape(1, -1)
    if _seq_tile(t):
        y, y_spec = y_t.reshape(t, n * A_WIDTH), _time_major_spec(tm, A_WIDTH, t // tm)
    else:
        y, y_spec = y_t.transpose(1, 0, 2).reshape(rows, A_WIDTH), row(A_WIDTH)
    return pl.pallas_call(
        _out_proj_kernel,
        grid=(rows // tm,),
        in_specs=[row(D_MODEL), y_spec, row(2 * A_WIDTH), row(B_WIDTH), row(C_WIDTH),
                  _full((1, A_WIDTH)), _full((1, A_WIDTH)), _full((2 * A_WIDTH, A_WIDTH)),
                  _full((D_MODEL, D_MODEL))],
        out_specs=row(D_MODEL),
        out_shape=jax.ShapeDtypeStruct((rows, D_MODEL), F32),
        compiler_params=_params("parallel"),
        name="out_proj",
    )(x, y, gb, yb, yc, vec(lp['a_lnx_g']), vec(lp['a_lnx_b']), bd256, w_bf)


GATE_LANE = 0
FLAG_LANE = N_EXPERTS
RANK_LANE = 2 * N_EXPERTS


def _router_kernel(x_ref, g_ref, r_ref, tri_ref, eye_ref, h_ref, rec_ref, rect_ref):
    x = x_ref[...]
    h = x * lax.rsqrt(jnp.mean(x * x, axis=-1, keepdims=True) + EPS) * g_ref[...]
    h_ref[...] = h.astype(BF16)
    logits = _dot_hi(h, r_ref[...])
    lane = lax.broadcasted_iota(jnp.int32, logits.shape, 1).astype(F32)
    lg = jnp.where(lane < N_EXPERTS, logits, NEG_BIG)
    m1 = jnp.max(lg, axis=-1, keepdims=True)
    i1 = jnp.min(jnp.where(lg == m1, lane, float(LANES)), axis=-1, keepdims=True)
    lg2 = jnp.where(lane == i1, NEG_BIG, lg)
    m2 = jnp.max(lg2, axis=-1, keepdims=True)
    i2 = jnp.min(jnp.where(lg2 == m2, lane, float(LANES)), axis=-1, keepdims=True)
    e = jnp.exp(m2 - m1)
    g1 = 1.0 / (1.0 + e)
    gates = jnp.where(lane == i1, g1, 0.0) + jnp.where(lane == i2, e * g1, 0.0)
    chosen = lambda off: jnp.where((lane == i1 + off) | (lane == i2 + off), 1.0, 0.0)
    rank = _dot(tri_ref[...], chosen(float(RANK_LANE)).astype(BF16))
    rec = gates + chosen(float(FLAG_LANE)) + rank
    rec_ref[...] = rec
    rect_ref[0] = lax.dot_general(eye_ref[...], rec, (((1,), (1,)), ((), ())),
                                  preferred_element_type=F32, precision=HIGHEST)


def _router(x, g, router_pad, tw):
    rows = x.shape[0]
    nw = rows // tw
    tri = (jnp.arange(tw)[:, None] > jnp.arange(tw)[None, :]).astype(BF16)
    eye = jnp.eye(LANES, dtype=F32)
    return pl.pallas_call(
        _router_kernel,
        grid=(nw,),
        in_specs=[pl.BlockSpec((tw, D_MODEL), lambda i: (i, 0)), _full((1, D_MODEL)),
                  _full((D_MODEL, LANES)), _full((tw, tw)), _full((LANES, LANES))],
        out_specs=[pl.BlockSpec((tw, D_MODEL), lambda i: (i, 0)),
                   pl.BlockSpec((tw, LANES), lambda i: (i, 0)),
                   pl.BlockSpec((1, LANES, tw), lambda i: (i, 0, 0))],
        out_shape=[jax.ShapeDtypeStruct((rows, D_MODEL), BF16),
                   jax.ShapeDtypeStruct((rows, LANES), F32),
                   jax.ShapeDtypeStruct((nw, LANES, tw), F32)],
        compiler_params=_params("parallel"),
        name="router",
    )(x, g, router_pad, tri, eye)


def _moe_kernel(h_ref, rec_ref, rect_ref, acc_ref, wg_ref, wu_ref, wd_ref, o_ref, *, e, cap):
    tw = h_ref.shape[0]
    rec = rec_ref[...]
    rect = rect_ref[0]
    lane = lax.broadcasted_iota(jnp.int32, rec.shape, 1)
    sub = lax.broadcasted_iota(jnp.int32, rect.shape, 0)
    col = lambda off: jnp.sum(jnp.where(lane == off + e, rec, 0.0), axis=-1, keepdims=True)
    row = lambda off: jnp.sum(jnp.where(sub == off + e, rect, 0.0), axis=0, keepdims=True)
    gate_c, flag_c, rank_c = col(GATE_LANE), col(FLAG_LANE), col(RANK_LANE)
    flag_r, rank_r = row(FLAG_LANE), row(RANK_LANE)
    o_ref[...] = acc_ref[...]
    count = jnp.max(rank_c + flag_c)
    n_pass = sum((count > float(k * cap)).astype(jnp.int32) for k in range(-(-tw // cap)))
    slot_r = lax.broadcasted_iota(jnp.int32, (cap, tw), 0).astype(F32)
    slot_c = lax.broadcasted_iota(jnp.int32, (tw, cap), 1).astype(F32)

    def one_pass(p, carry):
        base = (p * cap).astype(F32)
        sel = jnp.where((rank_r - base == slot_r) & (flag_r > 0.0), 1.0, 0.0).astype(BF16)
        sel_t = jnp.where((rank_c - base == slot_c) & (flag_c > 0.0), 1.0, 0.0).astype(BF16)
        xs = _dot(sel, h_ref[...]).astype(BF16)
        a = _dot(xs, wg_ref[0])
        u = _dot(xs, wu_ref[0])
        y = _dot((a * _sigmoid(a) * u).astype(BF16), wd_ref[0])
        o_ref[...] += gate_c * _dot(sel_t, y.astype(BF16))
        return carry

    lax.fori_loop(0, n_pass, one_pass, 0)


def _moe_window(rows):
    best = None
    for d in range(16, min(rows // MOE_MIN_WINDOWS, MOE_MAX_WINDOW) + 1, 16):
        if rows % d == 0:
            best = d
    assert best is not None, rows
    return best


def _moe(h, rec, rect, x, wg, wu, wd, tw):
    rows = x.shape[0]
    nw = rows // tw
    cap = -(-(tw * TOP_K * MOE_CAP_SLACK_PCT) // (N_EXPERTS * 100 * 16)) * 16
    win = lambda c: pl.BlockSpec((tw, c), lambda w: (w, 0))
    acc = x
    for e in range(wg.shape[0]):
        weight = lambda a, e=e: pl.BlockSpec((1,) + a.shape[1:], lambda w: (e, 0, 0),
                                             pipeline_mode=pl.Buffered(1))
        acc = pl.pallas_call(
            functools.partial(_moe_kernel, e=e, cap=cap),
            grid=(nw,),
            in_specs=[win(D_MODEL), win(LANES), pl.BlockSpec((1, LANES, tw), lambda w: (w, 0, 0)),
                      win(D_MODEL), weight(wg), weight(wu), weight(wd)],
            out_specs=win(D_MODEL),
            out_shape=jax.ShapeDtypeStruct((rows, D_MODEL), F32),
            compiler_params=_params("parallel"),
            name="moe",
        )(h, rec, rect, acc, wg, wu, wd)
    return acc


def _ffn_kernel(x_ref, g_ref, gate_ref, wg_ref, wu_ref, wd_ref, o_ref, h_scr, acc_scr):
    e = pl.program_id(1)
    f = pl.program_id(2)

    @pl.when((e == 0) & (f == 0))
    def _():
        x = x_ref[...]
        h = x * lax.rsqrt(jnp.mean(x * x, axis=-1, keepdims=True) + EPS) * g_ref[...]
        h_scr[...] = h.astype(BF16)
        acc_scr[...] = jnp.zeros(acc_scr.shape, F32)

    h = h_scr[...]
    a = _dot(h, wg_ref[0])
    u = _dot(h, wu_ref[0])
    act = (a * _sigmoid(a) * u).astype(BF16)
    lane = lax.broadcasted_iota(jnp.int32, gate_ref.shape, 1)
    gcol = jnp.sum(jnp.where(lane == e, gate_ref[...], 0.0), axis=-1, keepdims=True)
    acc_scr[...] += gcol * _dot(act, wd_ref[0])

    @pl.when((e == pl.num_programs(1) - 1) & (f == pl.num_programs(2) - 1))
    def _():
        o_ref[...] = x_ref[...] + acc_scr[...]


def _ffn(x, g, gates, wg, wu, wd):
    rows = x.shape[0]
    n_e = wg.shape[0]
    tm = _pick_tile(rows, 704, 16)
    tf = D_FF // 2
    return pl.pallas_call(
        _ffn_kernel,
        grid=(rows // tm, n_e, D_FF // tf),
        in_specs=[pl.BlockSpec((tm, D_MODEL), lambda i, e, f: (i, 0)),
                  pl.BlockSpec((1, D_MODEL), lambda i, e, f: (0, 0)),
                  pl.BlockSpec((tm, LANES), lambda i, e, f: (i, 0)),
                  pl.BlockSpec((1, D_MODEL, tf), lambda i, e, f: (e, 0, f)),
                  pl.BlockSpec((1, D_MODEL, tf), lambda i, e, f: (e, 0, f)),
                  pl.BlockSpec((1, tf, D_MODEL), lambda i, e, f: (e, f, 0))],
        out_specs=pl.BlockSpec((tm, D_MODEL), lambda i, e, f: (i, 0)),
        out_shape=jax.ShapeDtypeStruct((rows, D_MODEL), F32),
        scratch_shapes=[pltpu.VMEM((tm, D_MODEL), BF16), pltpu.VMEM((tm, D_MODEL), F32)],
        compiler_params=_params("parallel", "arbitrary", "arbitrary"),
        name="ffn",
    )(x, g, gates, wg, wu, wd)


def _wkv_to_pairs(s):
    n = s.shape[0]
    assert n % SCAN_SEQS == 0
    s = s.reshape(n // SCAN_SEQS, SCAN_SEQS, 2, 2, A_HEAD_DIM, A_HEAD_DIM)
    return s.transpose(0, 4, 2, 1, 3, 5).reshape(n // SCAN_SEQS, A_HEAD_DIM, 2 * SCAN_SEQS, LANES)


def _wkv_from_pairs(s):
    nblk = s.shape[0]
    s = s.reshape(nblk, A_HEAD_DIM, 2, SCAN_SEQS, 2, A_HEAD_DIM)
    return s.transpose(0, 3, 2, 4, 1, 5).reshape(nblk * SCAN_SEQS, A_HEADS, A_HEAD_DIM, A_HEAD_DIM)


def _mixers(x, n, t, shift0, wkv0, conv0, attend, lp, c, lam_init):
    z, qn, kn, kb, vb = _in_proj(x, lp['norm1_g'].reshape(1, -1), lp['w_in_bf'], lp['qg'], lp['kg'], c['bd512'])
    z3 = z.reshape(n, t, IN_COLS)
    za3 = z3[:, :, :A_COLS]
    prev = jnp.concatenate([shift0[:, None, :], za3[:, :-1]], axis=1).reshape(n * t, A_COLS)
    sc_t, gb = _rwkv_pre(z, prev, lp, c['bd256'], n, t)
    y_t, s1 = _rwkv_scan(sc_t, _wkv_to_pairs(wkv0), c['bd128'], c['eye'])
    yb, conv1 = _conformer_conv(z, conv0, lp, lp['pw_bf'], n, t)
    yc = attend(qn, kn, kb, vb, z)
    x = _out_proj(x, y_t, gb, yb, yc, lp, c['bd256'], lp['w_out_bf'])
    k_rows = kn.reshape(n, t, C_HEADS, 2 * C_QK_DIM)
    v_rows = z3[:, :, V_OFF:].reshape(n, t, C_HEADS, C_V_DIM)
    return x, k_rows, v_rows, _wkv_from_pairs(s1), za3[:, -1], conv1


def kernel(x_prompt, x_sample, cache_k, cache_v, page_table, state_wkv, state_shift, state_conv, meta_tokens, norm1_g, norm2_g, w_in, w_out, a_mu, a_w0, a_w_up, a_a0, a_a_up, a_g_up, a_k_k, a_k_a, a_r_k, a_lnx_g, a_lnx_b, b_conv_w, b_conv_b, b_ln_g, b_ln_b, b_pw_w, b_pw_b, c_qn_g, c_kn_g, c_lam_q1, c_lam_k1, c_lam_q2, c_lam_k2, c_subln_g, ffn_w_gate, ffn_w_up, ffn_w_down, moe_router, moe_w_gate, moe_w_up, moe_w_down):
    depth = w_in.shape[0]
    b, seq, _ = x_prompt.shape
    n_s, t_s, _ = x_sample.shape
    t_p = seq + N_META
    n_pool = cache_k.shape[1]
    pcols = PAGE_SIZE * C_HEADS

    consts = {
        'bd512': _block_diag2(C_QK_COLS),
        'bd256': _block_diag2(A_WIDTH),
        'bd128': _block_diag2(LANES),
        'eye': jnp.broadcast_to(
            (jnp.arange(LANES)[None, None, :] % A_HEAD_DIM == jnp.arange(A_HEAD_DIM)[:, None, None]).astype(F32),
            (A_HEAD_DIM, 2 * SCAN_SEQS, LANES)),
    }
    meta = jnp.broadcast_to(meta_tokens.astype(F32)[None], (b, N_META, D_MODEL))
    xp = jnp.concatenate([meta, x_prompt], axis=1).reshape(b * t_p, D_MODEL)
    xs = x_sample.reshape(n_s * t_s, D_MODEL)
    ck_all = cache_k.reshape(depth * n_pool, pcols, LANES)
    cv_all = cache_v.reshape(depth * n_pool, pcols, LANES)
    sample_tables = _sample_bias_tables(page_table.shape[1], t_s)

    outs = [[] for _ in range(10)]
    for l in range(depth):
        lam_init = 0.8 - 0.6 * math.exp(-0.3 * l)
        lp = {
            'norm1_g': norm1_g[l], 'w_in_bf': w_in[l].astype(BF16), 'w_out_bf': w_out[l].astype(BF16),
            'a_mu': a_mu[l], 'a_w0': a_w0[l], 'a_w_up': a_w_up[l], 'a_a0': a_a0[l], 'a_a_up': a_a_up[l],
            'a_g_up': a_g_up[l], 'a_k_k': a_k_k[l], 'a_k_a': a_k_a[l], 'a_r_k': a_r_k[l],
            'a_lnx_g': a_lnx_g[l], 'a_lnx_b': a_lnx_b[l],
            'b_conv_w': b_conv_w[l], 'b_conv_b': b_conv_b[l], 'b_ln_g': b_ln_g[l], 'b_ln_b': b_ln_b[l],
            'pw_bf': b_pw_w[l].astype(BF16), 'b_pw_b': b_pw_b[l],
            'qg': jnp.tile(c_qn_g[l], 2 * C_HEADS).reshape(1, -1),
            'kg': jnp.tile(c_kn_g[l], 2 * C_HEADS).reshape(1, -1),
        }
        lamv = jnp.stack([c_lam_q1[l], c_lam_k1[l], c_lam_q2[l], c_lam_k2[l]]).astype(F32)
        sg = c_subln_g[l].reshape(1, -1)

        def attend_prompt(qn, kn, kb, vb, z, lamv=lamv, sg=sg, lam_init=lam_init):
            return _attn_prompt(qn, kb, vb, lamv, sg, b, t_p, lam_init)

        def attend_sample(qn, kn, kb, vb, z, l=l, lamv=lamv, sg=sg, lam_init=lam_init):
            rows = t_s * C_HEADS
            qn3 = qn.reshape(n_s, rows, LANES)
            kn3 = kn.reshape(n_s, rows, LANES)
            vn3 = z[:, V_OFF:].reshape(n_s, rows, LANES)
            o = _attn_sample(qn3, kn3, vn3, ck_all, cv_all, page_table, l * n_pool, lamv, sg, sample_tables,
                             lam_init)
            return o.reshape(n_s * t_s, C_WIDTH)

        xp, k_p, v_p, wkv_p, shift_p, conv_p = _mixers(
            xp, b, t_p, jnp.zeros((b, A_COLS), F32), jnp.zeros((b, A_HEADS, A_HEAD_DIM, A_HEAD_DIM), F32),
            jnp.zeros((b, CONV_HIST, B_WIDTH), F32), attend_prompt, lp, consts, lam_init)
        xs, k_s, v_s, wkv_s, shift_s, conv_s = _mixers(
            xs, n_s, t_s, state_shift[l], state_wkv[l], state_conv[l], attend_sample, lp, consts, lam_init)

        i = l // 2
        g2 = norm2_g[l].reshape(1, -1)
        if l % 2 == 0:
            wg = ffn_w_gate[i].astype(BF16)[None]
            wu = ffn_w_up[i].astype(BF16)[None]
            wd = ffn_w_down[i].astype(BF16)[None]
            xp = _ffn(xp, g2, jnp.ones((xp.shape[0], LANES), F32), wg, wu, wd)
            xs = _ffn(xs, g2, jnp.ones((xs.shape[0], LANES), F32), wg, wu, wd)
        else:
            wg = moe_w_gate[i].astype(BF16)
            wu = moe_w_up[i].astype(BF16)
            wd = moe_w_down[i].astype(BF16)
            router_pad = jnp.pad(moe_router[i], ((0, 0), (0, LANES - N_EXPERTS)))
            tw_p, tw_s = _moe_window(xp.shape[0]), _moe_window(xs.shape[0])
            xp = _moe(*_router(xp, g2, router_pad, tw_p), xp, wg, wu, wd, tw_p)
            xs = _moe(*_router(xs, g2, router_pad, tw_s), xs, wg, wu, wd, tw_s)

        for lst, val in zip(outs, (k_p, v_p, k_s, v_s, wkv_p, wkv_s, shift_p, shift_s, conv_p, conv_s)):
            lst.append(val)

    y_prompt = xp.reshape(b, t_p, D_MODEL)[:, N_META:]
    y_sample = xs.reshape(n_s, t_s, D_MODEL)
    return (y_prompt, y_sample) + tuple(jnp.stack(o) for o in outs)
```

```python
import functools
import math

import jax
import jax.numpy as jnp
from jax import lax
from jax.experimental import pallas as pl
from jax.experimental.pallas import tpu as pltpu

F32 = jnp.float32
BF16 = jnp.bfloat16
HIGHEST = lax.Precision.HIGHEST

D_MODEL = 1024
EPS = 1e-6
N_META = 16
A_HEAD_DIM = 64
A_WIDTH = 256
A_HEADS = 4
A_W_LORA = 64
A_A_LORA = 64
A_G_LORA = 128
A_COLS = 1024
A_GN_EPS = 64e-5
B_WIDTH = 256
B_COLS = 512
CONV_WIDTH = 31
CONV_HIST = CONV_WIDTH - 1
C_QK_DIM = 64
C_V_DIM = 128
C_WIDTH = 512
C_HEADS = 4
C_QK_COLS = 512
IN_COLS = 3072
Q_OFF = A_COLS + B_COLS
K_OFF = Q_OFF + C_QK_COLS
V_OFF = K_OFF + C_QK_COLS
AB_COLS = A_COLS + B_COLS
D_FF = 2816
N_EXPERTS = 8
PAGE_SIZE = 128
LANES = 128
NEG_BIG = -1e30
ALIBI_SLOPES = tuple(float((2.0 ** (-8.0 / C_HEADS)) ** (h + 1)) for h in range(C_HEADS))
EXP_NEG_HALF = math.exp(-0.5)
SCAN_SEQS = 8
SCAN_COLS = 7 * A_WIDTH
SCAN_SUB = 8
TOP_K = 2
MIN_TIME_MAJOR_T = 64
MOE_MIN_WINDOWS = 3
MOE_MAX_WINDOW = 704
MOE_CAP_SLACK_PCT = 111
VMEM_LIMIT = 56 * 1024 * 1024


def _params(*sem):
    return pltpu.CompilerParams(dimension_semantics=sem, vmem_limit_bytes=VMEM_LIMIT)


def _pick_tile(n, target, mult=8):
    best = None
    for d in range(mult, min(n, target) + 1, mult):
        if n % d == 0:
            best = d
    return n if best is None else best


def _dot(a, b):
    return jnp.dot(a, b, preferred_element_type=F32)


def _dot_hi(a, b):
    return jnp.dot(a, b, preferred_element_type=F32, precision=HIGHEST)


def _dot_nt(a, b):
    return lax.dot_general(a, b, (((1,), (1,)), ((), ())), preferred_element_type=F32)


def _sigmoid(x):
    return 1.0 / (1.0 + jnp.exp(-x))


def _seg_sum(x, bd2):
    hi = x.astype(BF16)
    lo = (x - hi.astype(F32)).astype(BF16)
    return _dot(jnp.concatenate([hi, lo], axis=-1), bd2)


def _block_diag2(width, seg=64):
    r = jnp.arange(width) // seg
    bd = (r[:, None] == r[None, :]).astype(BF16)
    return jnp.concatenate([bd, bd], axis=0)


def _full(shape):
    nd = len(shape)
    return pl.BlockSpec(shape, lambda *_: (0,) * nd)


def _in_proj_kernel(x_ref, g_ref, w_ref, qg_ref, kg_ref, bd_ref, z_ref, qn_ref, kn_ref, kb_ref, v_ref, vb_ref):
    x = x_ref[...]
    h = x * lax.rsqrt(jnp.mean(x * x, axis=-1, keepdims=True) + EPS) * g_ref[...]
    z = _dot(h.astype(BF16), w_ref[...])
    z_ref[...] = z[:, :AB_COLS]
    v_ref[...] = z[:, V_OFF:]
    q = z[:, Q_OFF:Q_OFF + C_QK_COLS]
    k = z[:, K_OFF:K_OFF + C_QK_COLS]
    bd = bd_ref[...]
    inv = 1.0 / C_QK_DIM
    qn = q * lax.rsqrt(_seg_sum(q * q, bd) * inv + EPS) * qg_ref[...]
    kn = k * lax.rsqrt(_seg_sum(k * k, bd) * inv + EPS) * kg_ref[...]
    qn_ref[...] = qn * (C_QK_DIM ** -0.5)
    kn_ref[...] = kn
    kb_ref[...] = kn.astype(BF16)
    vb_ref[...] = z[:, V_OFF:].astype(BF16)


def _in_proj(x, g, w_bf, qg, kg, bd512):
    rows = x.shape[0]
    tm = _pick_tile(rows, 256, 16)
    row = lambda c: pl.BlockSpec((tm, c), lambda i: (i, 0))
    return pl.pallas_call(
        _in_proj_kernel,
        grid=(rows // tm,),
        in_specs=[row(D_MODEL), _full((1, D_MODEL)), _full((D_MODEL, IN_COLS)),
                  _full((1, C_QK_COLS)), _full((1, C_QK_COLS)), _full((2 * C_QK_COLS, C_QK_COLS))],
        out_specs=[row(AB_COLS), row(C_QK_COLS), row(C_QK_COLS), row(C_QK_COLS), row(C_WIDTH), row(C_WIDTH)],
        out_shape=[jax.ShapeDtypeStruct((rows, AB_COLS), F32),
                   jax.ShapeDtypeStruct((rows, C_QK_COLS), F32),
                   jax.ShapeDtypeStruct((rows, C_QK_COLS), F32),
                   jax.ShapeDtypeStruct((rows, C_QK_COLS), BF16),
                   jax.ShapeDtypeStruct((rows, C_WIDTH), F32),
                   jax.ShapeDtypeStruct((rows, C_WIDTH), BF16)],
        compiler_params=_params("parallel"),
        name="in_proj",
    )(x, g, w_bf, qg, kg, bd512)


def _rwkv_pre_kernel(za_ref, bnd_ref, mu_ref, w0_ref, wup_ref, a0_ref, aup_ref, gup_ref,
                     kk_ref, ka_ref, rk_ref, bd_ref, sc_ref, gb_ref, *, period):
    za = za_ref[...]
    rid = lax.broadcasted_iota(jnp.int32, za.shape, 0)
    at_boundary = (rid == 0) if period == za.shape[0] else (jnp.bitwise_and(rid, period - 1) == 0)
    prev = jnp.where(at_boundary, bnd_ref[...].reshape(-1, A_COLS), pltpu.roll(za, 1, 0))
    zs = za + mu_ref[...] * (prev - za)
    w = A_WIDTH
    r = zs[:, 0:w]
    k = zs[:, w:2 * w]
    v = zs[:, 2 * w:3 * w]
    wd = zs[:, 3 * w:3 * w + A_W_LORA]
    ad = zs[:, 3 * w + A_W_LORA:3 * w + A_W_LORA + A_A_LORA]
    gd = zs[:, 3 * w + A_W_LORA + A_A_LORA:]
    lora = lambda act, up_ref: _dot(act.astype(BF16), up_ref[...].astype(BF16))
    lw = w0_ref[...] + lora(jnp.tanh(wd), wup_ref)
    decay = jnp.exp(-EXP_NEG_HALF * _sigmoid(lw))
    a = _sigmoid(a0_ref[...] + lora(ad, aup_ref))
    g = lora(_sigmoid(gd), gup_ref)
    bd = bd_ref[...]
    kk = k * kk_ref[...]
    k2 = k * (1.0 + (a - 1.0) * ka_ref[...])
    kk = kk * lax.rsqrt(jnp.maximum(_seg_sum(kk * kk, bd), 1e-24))
    bonus = _seg_sum(r * k2 * rk_ref[...], bd) * v
    b = kk * a
    sc_ref[:, 0:w] = decay
    sc_ref[:, w:2 * w] = kk
    sc_ref[:, 2 * w:3 * w] = b
    sc_ref[:, 3 * w:4 * w] = k2
    sc_ref[:, 4 * w:5 * w] = v
    sc_ref[:, 5 * w:6 * w] = decay * r - kk * _seg_sum(b * r, bd)
    sc_ref[:, 6 * w:7 * w] = _seg_sum(k2 * r, bd)
    gb_ref[:, 0:w] = g
    gb_ref[:, w:2 * w] = bonus


def _time_major_spec(tm, cols, tiles_per_seq):
    return pl.BlockSpec((tm, cols), lambda i: (i % tiles_per_seq, i // tiles_per_seq))


def _seq_tile(t):
    return _pick_tile(t, 512, 8) if t >= MIN_TIME_MAJOR_T else None


def _rwkv_pre(z, shift0, lp, bd256, n, t):
    rows = z.shape[0]
    tm = _seq_tile(t) or _pick_tile(rows, 512, 8)
    row = lambda c: pl.BlockSpec((tm, c), lambda i: (i, 0))
    vec = lambda a: a.reshape(1, -1)
    if _seq_tile(t):
        tiles = t // tm
        last = z.reshape(n, t, AB_COLS)[:, tm - 1::tm, :A_COLS][:, :tiles - 1]
        bnd = jnp.concatenate([shift0[:, None, :], last], axis=1).reshape(n * tiles, 1, A_COLS)
        bnd_spec, period = pl.BlockSpec((1, 1, A_COLS), lambda i: (i, 0, 0)), tm
        sc_spec, sc_shape = _time_major_spec(tm, SCAN_COLS, tiles), (t, n * SCAN_COLS)
    else:
        assert t & (t - 1) == 0 and tm % t == 0
        bnd = jnp.repeat(shift0, t, axis=0)
        bnd_spec, period = row(A_COLS), t
        sc_spec, sc_shape = row(SCAN_COLS), (rows, SCAN_COLS)
    args = (z, bnd, vec(lp['a_mu']), vec(lp['a_w0']), lp['a_w_up'], vec(lp['a_a0']), lp['a_a_up'],
            lp['a_g_up'], vec(lp['a_k_k']), vec(lp['a_k_a']), vec(lp['a_r_k']), bd256)
    in_specs = [row(A_COLS), bnd_spec] + [_full(a.shape) for a in args[2:]]
    sc, gb = pl.pallas_call(
        functools.partial(_rwkv_pre_kernel, period=period),
        grid=(rows // tm,),
        in_specs=in_specs,
        out_specs=[sc_spec, row(2 * A_WIDTH)],
        out_shape=[jax.ShapeDtypeStruct(sc_shape, F32),
                   jax.ShapeDtypeStruct((rows, 2 * A_WIDTH), F32)],
        compiler_params=_params("parallel"),
        name="rwkv_pre",
    )(*args)
    if _seq_tile(t):
        return sc.reshape(t, n, SCAN_COLS), gb
    return sc.reshape(n, t, SCAN_COLS).transpose(1, 0, 2), gb


def _rwkv_scan_kernel(x_ref, s0_ref, bd_ref, eye_ref, eyeb_ref, y_ref, s1_ref, s_scr, vb_scr, *, tb, sub):
    nseq = x_ref.shape[1]
    npair = 2 * nseq
    w = A_WIDTH
    rows = A_HEAD_DIM * npair

    @pl.when(pl.program_id(1) == 0)
    def _():
        s_scr[...] = s0_ref[0]

    bd2 = bd_ref[...]
    bd1 = bd2[0:LANES]
    eye_bf = eyeb_ref[...]

    def rowvec(xt, c):
        return jnp.concatenate([xt[:, c:c + LANES], xt[:, c + LANES:c + 2 * LANES]], axis=0)

    def value_bcast(t_src, slot, row):
        v = rowvec(x_ref[t_src], 4 * w).astype(BF16)
        vb = _dot((eye_bf * v[None]).reshape(rows, LANES), bd1)
        vb_scr[slot, row] = vb.reshape(A_HEAD_DIM, npair, LANES)

    def seg1(a):
        nv = a.shape[0]
        return _dot(a.reshape(nv * npair, LANES).astype(BF16), bd1).reshape(nv, npair, LANES)

    for tt in range(sub):
        value_bcast(tt, 0, tt)

    def step(t, carry):
        slot = (t // sub) % 2
        row = t % sub
        value_bcast(jnp.minimum(t + sub, tb - 1), 1 - slot, row)
        xt = x_ref[t]
        dec, kk, b, k, v, rq, kr = (rowvec(xt, i * w) for i in range(7))
        s = s_scr[...]
        sa = seg1(s * kk[None])
        yq = seg1(s * rq[None])
        s_scr[...] = s * dec[None] - sa * b[None] + vb_scr[slot, row] * k[None]
        y = jnp.sum(yq * eye_ref[...], axis=0) + v * kr
        y_ref[t] = jnp.concatenate([y[0:nseq], y[nseq:npair]], axis=-1)
        return carry

    lax.fori_loop(0, tb, step, 0, unroll=2)

    @pl.when(pl.program_id(1) == pl.num_programs(1) - 1)
    def _():
        s1_ref[0] = s_scr[...]


def _rwkv_scan(sc_t, s0, bd128, eye):
    t, n, _ = sc_t.shape
    nb = s0.shape[2] // 2
    tb = _pick_tile(t, 64, SCAN_SUB)
    assert tb % SCAN_SUB == 0
    state_spec = pl.BlockSpec((1, A_HEAD_DIM, 2 * nb, LANES), lambda i, j: (i, 0, 0, 0))
    return pl.pallas_call(
        functools.partial(_rwkv_scan_kernel, tb=tb, sub=SCAN_SUB),
        grid=(n // nb, t // tb),
        in_specs=[pl.BlockSpec((tb, nb, SCAN_COLS), lambda i, j: (j, i, 0)),
                  state_spec,
                  _full((2 * LANES, LANES)), _full((A_HEAD_DIM, 2 * nb, LANES)),
                  _full((A_HEAD_DIM, 2 * nb, LANES))],
        out_specs=[pl.BlockSpec((tb, nb, A_WIDTH), lambda i, j: (j, i, 0)), state_spec],
        out_shape=[jax.ShapeDtypeStruct((t, n, A_WIDTH), F32),
                   jax.ShapeDtypeStruct(s0.shape, F32)],
        scratch_shapes=[pltpu.VMEM((A_HEAD_DIM, 2 * nb, LANES), F32),
                        pltpu.VMEM((2, SCAN_SUB, A_HEAD_DIM, 2 * nb, LANES), F32)],
        compiler_params=_params("parallel", "arbitrary"),
        name="rwkv_scan",
    )(sc_t, s0, bd128, eye, eye.astype(BF16))


_CONV_PAD = 32


def _conv_kernel(zb_ref, c0_ref, cw_ref, cb_ref, lg_ref, lb_ref, pw_ref, pb_ref,
                 y_ref, c1_ref, ext_scr, h_scr, *, nb, t, tc):
    cw = cw_ref[...]
    for s in range(nb):
        zb = zb_ref[s * t:(s + 1) * t, :]
        u = zb[:, :B_WIDTH] * _sigmoid(zb[:, B_WIDTH:])
        ext_scr[s, _CONV_PAD - CONV_HIST:_CONV_PAD, :] = c0_ref[s]
        ext_scr[s, _CONV_PAD:_CONV_PAD + t, :] = u
        c1_ref[s] = ext_scr[s, t + _CONV_PAD - CONV_HIST:t + _CONV_PAD, :]

        def chunk(c, carry, s=s):
            base = pl.multiple_of(c * tc, 8)
            win = ext_scr[s, pl.ds(base, tc + _CONV_PAD), :]
            acc = jnp.zeros((tc, B_WIDTH), F32)
            first = _CONV_PAD - CONV_HIST
            for b in range(8):
                taps = [j for j in range(CONV_WIDTH) if (first + j) % 8 == b]
                if not taps:
                    continue
                span = max(first + j - b for j in taps) + tc
                wb = win[b:b + span, :]
                for j in taps:
                    o = first + j - b
                    acc = acc + wb[o:o + tc, :] * cw[j:j + 1, :]
            acc = acc + cb_ref[...]
            xc = acc - jnp.mean(acc, axis=-1, keepdims=True)
            hn = xc * lax.rsqrt(jnp.mean(xc * xc, axis=-1, keepdims=True) + EPS)
            hn = hn * lg_ref[...] + lb_ref[...]
            hn = hn * _sigmoid(hn)
            h_scr[pl.ds(pl.multiple_of(s * t + c * tc, 8), tc), :] = hn
            return carry

        lax.fori_loop(0, t // tc, chunk, 0)
    y_ref[...] = _dot(h_scr[...].astype(BF16), pw_ref[...]) + pb_ref[...]


def _conformer_conv(z, conv0, lp, pw_bf, n, t):
    nb = 1 if t >= 64 else _pick_tile(n, 16, 1)
    tc = _pick_tile(t, 48, 8)
    rows = nb * t
    vec = lambda a: a.reshape(1, -1)
    return pl.pallas_call(
        functools.partial(_conv_kernel, nb=nb, t=t, tc=tc),
        grid=(n // nb,),
        in_specs=[pl.BlockSpec((rows, B_COLS), lambda i: (i, A_COLS // B_COLS)),
                  pl.BlockSpec((nb, CONV_HIST, B_WIDTH), lambda i: (i, 0, 0)),
                  _full((CONV_WIDTH, B_WIDTH)), _full((1, B_WIDTH)), _full((1, B_WIDTH)),
                  _full((1, B_WIDTH)), _full((B_WIDTH, B_WIDTH)), _full((1, B_WIDTH))],
        out_specs=[pl.BlockSpec((rows, B_WIDTH), lambda i: (i, 0)),
                   pl.BlockSpec((nb, CONV_HIST, B_WIDTH), lambda i: (i, 0, 0))],
        out_shape=[jax.ShapeDtypeStruct((n * t, B_WIDTH), F32),
                   jax.ShapeDtypeStruct((n, CONV_HIST, B_WIDTH), F32)],
        scratch_shapes=[pltpu.VMEM((nb, t + _CONV_PAD, B_WIDTH), F32),
                        pltpu.VMEM((rows, B_WIDTH), F32)],
        compiler_params=_params("parallel"),
        name="conformer_conv",
    )(z, conv0, lp['b_conv_w'], vec(lp['b_conv_b']), vec(lp['b_ln_g']), vec(lp['b_ln_b']),
      pw_bf, vec(lp['b_pw_b']))


def _lambda_from(lamv_ref, lam_init):
    lv = lamv_ref[...]
    s1 = jnp.sum(lv[0:1] * lv[1:2], axis=-1, keepdims=True)
    s2 = jnp.sum(lv[2:3] * lv[3:4], axis=-1, keepdims=True)
    return jnp.exp(s1) - jnp.exp(s2) + lam_init


def _sub_ln(o, sg, lam_init):
    o = o * lax.rsqrt(jnp.mean(o * o, axis=-1, keepdims=True) + EPS) * sg
    return o * (1.0 - lam_init)


def _map_masks(rows):
    lane = lax.broadcasted_iota(jnp.int32, (rows, LANES), 1)
    return lane < C_QK_DIM, lane >= C_QK_DIM


def _attn_prompt_kernel(q_ref, k_ref, v_ref, lamv_ref, sg_ref, o_ref, m_scr, l_scr, acc_scr,
                        *, tq, lam_init):
    qi = pl.program_id(1)
    kj = pl.program_id(2)

    @pl.when(kj == 0)
    def _():
        m_scr[...] = jnp.full(m_scr.shape, NEG_BIG, F32)
        l_scr[...] = jnp.zeros(l_scr.shape, F32)
        acc_scr[...] = jnp.zeros(acc_scr.shape, F32)

    @pl.when(kj <= qi)
    def _():
        qpos = qi * tq + lax.broadcasted_iota(jnp.int32, (tq, tq), 0)
        kpos = kj * tq + lax.broadcasted_iota(jnp.int32, (tq, tq), 1)
        dist = (qpos - kpos).astype(F32)
        causal = kpos <= qpos
        masks = _map_masks(tq)
        for h in range(C_HEADS):
            qh = q_ref[:, h * LANES:(h + 1) * LANES]
            kh = k_ref[:, h * LANES:(h + 1) * LANES]
            vh = v_ref[:, h * LANES:(h + 1) * LANES]
            for m in range(2):
                i = 2 * h + m
                qm = jnp.where(masks[m], qh, 0.0).astype(BF16)
                s = _dot_nt(qm, kh) - ALIBI_SLOPES[h] * dist
                s = jnp.where(causal, s, NEG_BIG)
                m_prev = m_scr[i]
                m_new = jnp.maximum(m_prev, jnp.max(s, axis=-1, keepdims=True))
                alpha = jnp.exp(m_prev - m_new)
                p = jnp.exp(s - m_new)
                l_scr[i] = alpha * l_scr[i] + jnp.sum(p, axis=-1, keepdims=True)
                acc_scr[i] = alpha * acc_scr[i] + _dot(p.astype(BF16), vh)
                m_scr[i] = m_new

    @pl.when(kj == qi)
    def _():
        lam = _lambda_from(lamv_ref, lam_init)
        for h in range(C_HEADS):
            o = acc_scr[2 * h] / l_scr[2 * h] - lam * (acc_scr[2 * h + 1] / l_scr[2 * h + 1])
            o_ref[:, h * LANES:(h + 1) * LANES] = _sub_ln(o, sg_ref[...], lam_init)


def _attn_prompt(qn, kb, vb, lamv, sg, n, t, lam_init):
    tq = _pick_tile(t, 704, 16)
    nq = t // tq
    return pl.pallas_call(
        functools.partial(_attn_prompt_kernel, tq=tq, lam_init=lam_init),
        grid=(n, nq, nq),
        in_specs=[pl.BlockSpec((tq, C_QK_COLS), lambda b, i, j: (b * nq + i, 0)),
                  pl.BlockSpec((tq, C_QK_COLS), lambda b, i, j: (b * nq + jnp.minimum(i, j), 0)),
                  pl.BlockSpec((tq, C_WIDTH), lambda b, i, j: (b * nq + jnp.minimum(i, j), 0)),
                  _full((4, C_QK_DIM)), _full((1, C_V_DIM))],
        out_specs=pl.BlockSpec((tq, C_WIDTH), lambda b, i, j: (b * nq + i, 0)),
        out_shape=jax.ShapeDtypeStruct((n * t, C_WIDTH), F32),
        scratch_shapes=[pltpu.VMEM((2 * C_HEADS, tq, 1), F32),
                        pltpu.VMEM((2 * C_HEADS, tq, 1), F32),
                        pltpu.VMEM((2 * C_HEADS, tq, C_V_DIM), F32)],
        compiler_params=_params("parallel", "parallel", "arbitrary"),
        name="attn_prompt",
    )(qn, kb, vb, lamv, sg)


def _attn_sample_kernel(pt_ref, q_ref, kn_ref, vn_ref, lamv_ref, sg_ref, tb_ref, cj_ref, tn_ref, *rest,
                        n_pages, t_s, lam_init):
    k_refs = rest[:n_pages]
    v_refs = rest[n_pages:2 * n_pages]
    o_ref = rest[2 * n_pages]
    del pt_ref
    rows = t_s * C_HEADS
    q = q_ref[0]
    masks = _map_masks(rows)
    qcat = jnp.concatenate([jnp.where(masks[0], q, 0.0), jnp.where(masks[1], q, 0.0),
                            jnp.zeros((LANES - 2 * rows, LANES), F32)], axis=0).astype(BF16)

    m_run = jnp.full((1, LANES), NEG_BIG, F32)
    l_run = jnp.zeros((1, LANES), F32)
    acc = jnp.zeros((C_V_DIM, LANES), F32)

    def update(kblk, vblk, bias, m_run, l_run, acc):
        s = _dot_nt(kblk.astype(BF16), qcat) + bias
        m_new = jnp.maximum(m_run, jnp.max(s, axis=0, keepdims=True))
        alpha = jnp.exp(m_run - m_new)
        p = jnp.exp(s - m_new)
        l_new = alpha * l_run + jnp.sum(p, axis=0, keepdims=True)
        acc_new = alpha * acc + _dot(vblk.T.astype(BF16), p.astype(BF16))
        return m_new, l_new, acc_new

    tbias = tb_ref[...]
    for j in range(n_pages):
        bias = tbias + cj_ref[j:j + 1, :]
        m_run, l_run, acc = update(k_refs[j][0], v_refs[j][0], bias, m_run, l_run, acc)
    m_run, l_run, acc = update(kn_ref[0], vn_ref[0], tn_ref[...], m_run, l_run, acc)

    lam = _lambda_from(lamv_ref, lam_init)
    o_all = (acc / l_run).T
    o = o_all[0:rows] - lam * o_all[rows:2 * rows]
    o_ref[0] = _sub_ln(o, sg_ref[...], lam_init)


def _sample_bias_tables(n_pages, t_s):
    rows = t_s * C_HEADS
    past = n_pages * PAGE_SIZE
    col = jnp.arange(LANES)
    used = col < 2 * rows
    c_r = col % rows
    c_t = c_r // C_HEADS
    c_h = c_r % C_HEADS
    slope = jnp.where(used, jnp.asarray(ALIBI_SLOPES, F32)[c_h], 0.0)
    prow = jnp.arange(PAGE_SIZE * C_HEADS)
    p_tok = prow // C_HEADS
    p_h = prow % C_HEADS
    ok = (p_h[:, None] == c_h[None, :]) | ~used[None, :]
    tb = jnp.where(ok, slope[None, :] * p_tok[:, None].astype(F32), NEG_BIG)
    starts = jnp.arange(n_pages) * PAGE_SIZE
    cj = -slope[None, :] * (past + c_t[None, :] - starts[:, None]).astype(F32)
    nrow = jnp.arange(rows)
    n_t = nrow // C_HEADS
    n_h = nrow % C_HEADS
    okn = ((n_h[:, None] == c_h[None, :]) & (n_t[:, None] <= c_t[None, :])) | ~used[None, :]
    tn = jnp.where(okn, -slope[None, :] * (c_t[None, :] - n_t[:, None]).astype(F32), NEG_BIG)
    return tb.astype(F32), cj.astype(F32), tn.astype(F32)


def _attn_sample(qn3, kn3, vn3, ck, cv, page_table, page_base, lamv, sg, tables, lam_init):
    n, rows, _ = qn3.shape
    n_pages = page_table.shape[1]
    pcols = PAGE_SIZE * C_HEADS
    assert 2 * rows <= LANES
    seq = lambda: pl.BlockSpec((1, rows, LANES), lambda i, pt: (i, 0, 0))
    page = lambda j: pl.BlockSpec((1, pcols, LANES), lambda i, pt, j=j: (page_base + pt[i, j], 0, 0))
    const = lambda a: pl.BlockSpec(a.shape, lambda i, pt: (0, 0))
    grid_spec = pltpu.PrefetchScalarGridSpec(
        num_scalar_prefetch=1,
        grid=(n,),
        in_specs=[seq(), seq(), seq(), const(lamv), const(sg)] + [const(a) for a in tables]
                 + [page(j) for j in range(n_pages)] + [page(j) for j in range(n_pages)],
        out_specs=seq(),
    )
    return pl.pallas_call(
        functools.partial(_attn_sample_kernel, n_pages=n_pages, t_s=rows // C_HEADS, lam_init=lam_init),
        grid_spec=grid_spec,
        out_shape=jax.ShapeDtypeStruct((n, rows, LANES), F32),
        compiler_params=_params("parallel"),
        name="attn_sample",
    )(page_table, qn3, kn3, vn3, lamv, sg, *tables, *([ck] * n_pages), *([cv] * n_pages))


def _out_proj_kernel(x_ref, y_ref, gb_ref, yb_ref, yc_ref, lg_ref, lb_ref, bd_ref, w_ref, o_ref):
    y = y_ref[...]
    bd = bd_ref[...]
    inv = 1.0 / A_HEAD_DIM
    yc = y - _seg_sum(y, bd) * inv
    yn = yc * lax.rsqrt(_seg_sum(yc * yc, bd) * inv + A_GN_EPS)
    ya = (yn * lg_ref[...] + lb_ref[...] + gb_ref[:, A_WIDTH:]) * gb_ref[:, :A_WIDTH]
    acc = _dot(ya.astype(BF16), w_ref[0:A_WIDTH, :])
    acc += _dot(yb_ref[...].astype(BF16), w_ref[A_WIDTH:A_WIDTH + B_WIDTH, :])
    acc += _dot(yc_ref[...].astype(BF16), w_ref[A_WIDTH + B_WIDTH:, :])
    o_ref[...] = x_ref[...] + acc


def _out_proj(x, y_t, gb, yb, yc, lp, bd256, w_bf):
    rows = x.shape[0]
    t, n, _ = y_t.shape
    tm = _seq_tile(t) or _pick_tile(rows, 512, 8)
    row = lambda c: pl.BlockSpec((tm, c), lambda i: (i, 0))
    vec = lambda a: a.reshape(1, -1)
    if _seq_tile(t):
        y, y_spec = y_t.reshape(t, n * A_WIDTH), _time_major_spec(tm, A_WIDTH, t // tm)
    else:
        y, y_spec = y_t.transpose(1, 0, 2).reshape(rows, A_WIDTH), row(A_WIDTH)
    return pl.pallas_call(
        _out_proj_kernel,
        grid=(rows // tm,),
        in_specs=[row(D_MODEL), y_spec, row(2 * A_WIDTH), row(B_WIDTH), row(C_WIDTH),
                  _full((1, A_WIDTH)), _full((1, A_WIDTH)), _full((2 * A_WIDTH, A_WIDTH)),
                  _full((D_MODEL, D_MODEL))],
        out_specs=row(D_MODEL),
        out_shape=jax.ShapeDtypeStruct((rows, D_MODEL), F32),
        compiler_params=_params("parallel"),
        name="out_proj",
    )(x, y, gb, yb, yc, vec(lp['a_lnx_g']), vec(lp['a_lnx_b']), bd256, w_bf)


GATE_LANE = 0
FLAG_LANE = N_EXPERTS
RANK_LANE = 2 * N_EXPERTS


def _router_kernel(x_ref, g_ref, r_ref, tri_ref, eye_ref, h_ref, rec_ref, rect_ref):
    x = x_ref[...]
    h = x * lax.rsqrt(jnp.mean(x * x, axis=-1, keepdims=True) + EPS) * g_ref[...]
    h_ref[...] = h.astype(BF16)
    logits = _dot_hi(h, r_ref[...])
    lane = lax.broadcasted_iota(jnp.int32, logits.shape, 1).astype(F32)
    lg = jnp.where(lane < N_EXPERTS, logits, NEG_BIG)
    m1 = jnp.max(lg, axis=-1, keepdims=True)
    i1 = jnp.min(jnp.where(lg == m1, lane, float(LANES)), axis=-1, keepdims=True)
    lg2 = jnp.where(lane == i1, NEG_BIG, lg)
    m2 = jnp.max(lg2, axis=-1, keepdims=True)
    i2 = jnp.min(jnp.where(lg2 == m2, lane, float(LANES)), axis=-1, keepdims=True)
    e = jnp.exp(m2 - m1)
    g1 = 1.0 / (1.0 + e)
    gates = jnp.where(lane == i1, g1, 0.0) + jnp.where(lane == i2, e * g1, 0.0)
    chosen = lambda off: jnp.where((lane == i1 + off) | (lane == i2 + off), 1.0, 0.0)
    rank = _dot(tri_ref[...], chosen(float(RANK_LANE)).astype(BF16))
    rec = gates + chosen(float(FLAG_LANE)) + rank
    rec_ref[...] = rec
    rect_ref[0] = lax.dot_general(eye_ref[...], rec, (((1,), (1,)), ((), ())),
                                  preferred_element_type=F32, precision=HIGHEST)


def _router(x, g, router_pad, tw):
    rows = x.shape[0]
    nw = rows // tw
    tri = (jnp.arange(tw)[:, None] > jnp.arange(tw)[None, :]).astype(BF16)
    eye = jnp.eye(LANES, dtype=F32)
    return pl.pallas_call(
        _router_kernel,
        grid=(nw,),
        in_specs=[pl.BlockSpec((tw, D_MODEL), lambda i: (i, 0)), _full((1, D_MODEL)),
                  _full((D_MODEL, LANES)), _full((tw, tw)), _full((LANES, LANES))],
        out_specs=[pl.BlockSpec((tw, D_MODEL), lambda i: (i, 0)),
                   pl.BlockSpec((tw, LANES), lambda i: (i, 0)),
                   pl.BlockSpec((1, LANES, tw), lambda i: (i, 0, 0))],
        out_shape=[jax.ShapeDtypeStruct((rows, D_MODEL), BF16),
                   jax.ShapeDtypeStruct((rows, LANES), F32),
                   jax.ShapeDtypeStruct((nw, LANES, tw), F32)],
        compiler_params=_params("parallel"),
        name="router",
    )(x, g, router_pad, tri, eye)


def _moe_kernel(h_ref, rec_ref, rect_ref, acc_ref, wg_ref, wu_ref, wd_ref, o_ref, *, e, cap, small):
    tw = h_ref.shape[0]
    rec = rec_ref[...]
    rect = rect_ref[0]
    lane = lax.broadcasted_iota(jnp.int32, rec.shape, 1)
    sub = lax.broadcasted_iota(jnp.int32, rect.shape, 0)
    col = lambda off: jnp.sum(jnp.where(lane == off + e, rec, 0.0), axis=-1, keepdims=True)
    row = lambda off: jnp.sum(jnp.where(sub == off + e, rect, 0.0), axis=0, keepdims=True)
    gate_c, flag_c, rank_c = col(GATE_LANE), col(FLAG_LANE), col(RANK_LANE)
    flag_r, rank_r = row(FLAG_LANE), row(RANK_LANE)
    o_ref[...] = acc_ref[...]
    count = jnp.max(rank_c + flag_c)

    def run_pass(base, size):
        slot_r = lax.broadcasted_iota(jnp.int32, (size, tw), 0).astype(F32)
        slot_c = lax.broadcasted_iota(jnp.int32, (tw, size), 1).astype(F32)
        sel = jnp.where((rank_r - base == slot_r) & (flag_r > 0.0), 1.0, 0.0).astype(BF16)
        sel_t = jnp.where((rank_c - base == slot_c) & (flag_c > 0.0), 1.0, 0.0).astype(BF16)
        xs = _dot(sel, h_ref[...]).astype(BF16)
        a = _dot(xs, wg_ref[0])
        u = _dot(xs, wu_ref[0])
        y = _dot((a * _sigmoid(a) * u).astype(BF16), wd_ref[0])
        o_ref[...] += gate_c * _dot(sel_t, y.astype(BF16))

    n_full = sum((count >= float((k + 1) * cap)).astype(jnp.int32) for k in range(tw // cap))

    def full_pass(p, carry):
        run_pass((p * cap).astype(F32), cap)
        return carry

    lax.fori_loop(0, n_full, full_pass, 0)
    done = (n_full * cap).astype(F32)
    rem = count - done

    @pl.when(rem > float(small))
    def _():
        run_pass(done, cap)

    @pl.when((rem > 0.0) & (rem <= float(small)))
    def _():
        run_pass(done, small)


def _moe_window(rows):
    best = None
    for d in range(16, min(rows // MOE_MIN_WINDOWS, MOE_MAX_WINDOW) + 1, 16):
        if rows % d == 0:
            best = d
    assert best is not None, rows
    return best


def _moe(h, rec, rect, x, wg, wu, wd, tw):
    rows = x.shape[0]
    nw = rows // tw
    cap = -(-(tw * TOP_K * MOE_CAP_SLACK_PCT) // (N_EXPERTS * 100 * 16)) * 16
    small = -(-(2 * cap) // (3 * 16)) * 16
    win = lambda c: pl.BlockSpec((tw, c), lambda w: (w, 0))
    acc = x
    for e in range(wg.shape[0]):
        weight = lambda a, e=e: pl.BlockSpec((1,) + a.shape[1:], lambda w: (e, 0, 0),
                                             pipeline_mode=pl.Buffered(1))
        acc = pl.pallas_call(
            functools.partial(_moe_kernel, e=e, cap=cap, small=small),
            grid=(nw,),
            in_specs=[win(D_MODEL), win(LANES), pl.BlockSpec((1, LANES, tw), lambda w: (w, 0, 0)),
                      win(D_MODEL), weight(wg), weight(wu), weight(wd)],
            out_specs=win(D_MODEL),
            out_shape=jax.ShapeDtypeStruct((rows, D_MODEL), F32),
            compiler_params=_params("parallel"),
            name="moe",
        )(h, rec, rect, acc, wg, wu, wd)
    return acc


def _ffn_kernel(x_ref, g_ref, gate_ref, wg_ref, wu_ref, wd_ref, o_ref, h_scr, acc_scr):
    e = pl.program_id(1)
    f = pl.program_id(2)

    @pl.when((e == 0) & (f == 0))
    def _():
        x = x_ref[...]
        h = x * lax.rsqrt(jnp.mean(x * x, axis=-1, keepdims=True) + EPS) * g_ref[...]
        h_scr[...] = h.astype(BF16)
        acc_scr[...] = jnp.zeros(acc_scr.shape, F32)

    h = h_scr[...]
    a = _dot(h, wg_ref[0])
    u = _dot(h, wu_ref[0])
    act = (a * _sigmoid(a) * u).astype(BF16)
    lane = lax.broadcasted_iota(jnp.int32, gate_ref.shape, 1)
    gcol = jnp.sum(jnp.where(lane == e, gate_ref[...], 0.0), axis=-1, keepdims=True)
    acc_scr[...] += gcol * _dot(act, wd_ref[0])

    @pl.when((e == pl.num_programs(1) - 1) & (f == pl.num_programs(2) - 1))
    def _():
        o_ref[...] = x_ref[...] + acc_scr[...]


def _ffn(x, g, gates, wg, wu, wd):
    rows = x.shape[0]
    n_e = wg.shape[0]
    tm = _pick_tile(rows, 704, 16)
    tf = D_FF // 2
    return pl.pallas_call(
        _ffn_kernel,
        grid=(rows // tm, n_e, D_FF // tf),
        in_specs=[pl.BlockSpec((tm, D_MODEL), lambda i, e, f: (i, 0)),
                  pl.BlockSpec((1, D_MODEL), lambda i, e, f: (0, 0)),
                  pl.BlockSpec((tm, LANES), lambda i, e, f: (i, 0)),
                  pl.BlockSpec((1, D_MODEL, tf), lambda i, e, f: (e, 0, f)),
                  pl.BlockSpec((1, D_MODEL, tf), lambda i, e, f: (e, 0, f)),
                  pl.BlockSpec((1, tf, D_MODEL), lambda i, e, f: (e, f, 0))],
        out_specs=pl.BlockSpec((tm, D_MODEL), lambda i, e, f: (i, 0)),
        out_shape=jax.ShapeDtypeStruct((rows, D_MODEL), F32),
        scratch_shapes=[pltpu.VMEM((tm, D_MODEL), BF16), pltpu.VMEM((tm, D_MODEL), F32)],
        compiler_params=_params("parallel", "arbitrary", "arbitrary"),
        name="ffn",
    )(x, g, gates, wg, wu, wd)


def _wkv_to_pairs(s):
    n = s.shape[0]
    assert n % SCAN_SEQS == 0
    s = s.reshape(n // SCAN_SEQS, SCAN_SEQS, 2, 2, A_HEAD_DIM, A_HEAD_DIM)
    return s.transpose(0, 4, 2, 1, 3, 5).reshape(n // SCAN_SEQS, A_HEAD_DIM, 2 * SCAN_SEQS, LANES)


def _wkv_from_pairs(s):
    nblk = s.shape[0]
    s = s.reshape(nblk, A_HEAD_DIM, 2, SCAN_SEQS, 2, A_HEAD_DIM)
    return s.transpose(0, 3, 2, 4, 1, 5).reshape(nblk * SCAN_SEQS, A_HEADS, A_HEAD_DIM, A_HEAD_DIM)


def _mixers(x, n, t, shift0, wkv0, conv0, attend, lp, c, lam_init):
    z, qn, kn, kb, v, vb = _in_proj(x, lp['norm1_g'].reshape(1, -1), lp['w_in_bf'], lp['qg'], lp['kg'],
                                    c['bd512'])
    sc_t, gb = _rwkv_pre(z, shift0, lp, c['bd256'], n, t)
    y_t, s1 = _rwkv_scan(sc_t, _wkv_to_pairs(wkv0), c['bd128'], c['eye'])
    yb, conv1 = _conformer_conv(z, conv0, lp, lp['pw_bf'], n, t)
    yc = attend(qn, kn, kb, vb, v)
    x = _out_proj(x, y_t, gb, yb, yc, lp, c['bd256'], lp['w_out_bf'])
    k_rows = kn.reshape(n, t, C_HEADS, 2 * C_QK_DIM)
    v_rows = v.reshape(n, t, C_HEADS, C_V_DIM)
    shift1 = z.reshape(n, t, AB_COLS)[:, -1, :A_COLS]
    return x, k_rows, v_rows, _wkv_from_pairs(s1), shift1, conv1


def kernel(x_prompt, x_sample, cache_k, cache_v, page_table, state_wkv, state_shift, state_conv, meta_tokens, norm1_g, norm2_g, w_in, w_out, a_mu, a_w0, a_w_up, a_a0, a_a_up, a_g_up, a_k_k, a_k_a, a_r_k, a_lnx_g, a_lnx_b, b_conv_w, b_conv_b, b_ln_g, b_ln_b, b_pw_w, b_pw_b, c_qn_g, c_kn_g, c_lam_q1, c_lam_k1, c_lam_q2, c_lam_k2, c_subln_g, ffn_w_gate, ffn_w_up, ffn_w_down, moe_router, moe_w_gate, moe_w_up, moe_w_down):
    depth = w_in.shape[0]
    b, seq, _ = x_prompt.shape
    n_s, t_s, _ = x_sample.shape
    t_p = seq + N_META
    n_pool = cache_k.shape[1]
    pcols = PAGE_SIZE * C_HEADS

    consts = {
        'bd512': _block_diag2(C_QK_COLS),
        'bd256': _block_diag2(A_WIDTH),
        'bd128': _block_diag2(LANES),
        'eye': jnp.broadcast_to(
            (jnp.arange(LANES)[None, None, :] % A_HEAD_DIM == jnp.arange(A_HEAD_DIM)[:, None, None]).astype(F32),
            (A_HEAD_DIM, 2 * SCAN_SEQS, LANES)),
    }
    meta = jnp.broadcast_to(meta_tokens.astype(F32)[None], (b, N_META, D_MODEL))
    xp = jnp.concatenate([meta, x_prompt], axis=1).reshape(b * t_p, D_MODEL)
    xs = x_sample.reshape(n_s * t_s, D_MODEL)
    ck_all = cache_k.reshape(depth * n_pool, pcols, LANES)
    cv_all = cache_v.reshape(depth * n_pool, pcols, LANES)
    sample_tables = _sample_bias_tables(page_table.shape[1], t_s)

    outs = [[] for _ in range(10)]
    for l in range(depth):
        lam_init = 0.8 - 0.6 * math.exp(-0.3 * l)
        lp = {
            'norm1_g': norm1_g[l], 'w_in_bf': w_in[l].astype(BF16), 'w_out_bf': w_out[l].astype(BF16),
            'a_mu': a_mu[l], 'a_w0': a_w0[l], 'a_w_up': a_w_up[l], 'a_a0': a_a0[l], 'a_a_up': a_a_up[l],
            'a_g_up': a_g_up[l], 'a_k_k': a_k_k[l], 'a_k_a': a_k_a[l], 'a_r_k': a_r_k[l],
            'a_lnx_g': a_lnx_g[l], 'a_lnx_b': a_lnx_b[l],
            'b_conv_w': b_conv_w[l], 'b_conv_b': b_conv_b[l], 'b_ln_g': b_ln_g[l], 'b_ln_b': b_ln_b[l],
            'pw_bf': b_pw_w[l].astype(BF16), 'b_pw_b': b_pw_b[l],
            'qg': jnp.tile(c_qn_g[l], 2 * C_HEADS).reshape(1, -1),
            'kg': jnp.tile(c_kn_g[l], 2 * C_HEADS).reshape(1, -1),
        }
        lamv = jnp.stack([c_lam_q1[l], c_lam_k1[l], c_lam_q2[l], c_lam_k2[l]]).astype(F32)
        sg = c_subln_g[l].reshape(1, -1)

        def attend_prompt(qn, kn, kb, vb, v, lamv=lamv, sg=sg, lam_init=lam_init):
            return _attn_prompt(qn, kb, vb, lamv, sg, b, t_p, lam_init)

        def attend_sample(qn, kn, kb, vb, v, l=l, lamv=lamv, sg=sg, lam_init=lam_init):
            rows = t_s * C_HEADS
            qn3 = qn.reshape(n_s, rows, LANES)
            kn3 = kn.reshape(n_s, rows, LANES)
            vn3 = v.reshape(n_s, rows, LANES)
            o = _attn_sample(qn3, kn3, vn3, ck_all, cv_all, page_table, l * n_pool, lamv, sg, sample_tables,
                             lam_init)
            return o.reshape(n_s * t_s, C_WIDTH)

        xp, k_p, v_p, wkv_p, shift_p, conv_p = _mixers(
            xp, b, t_p, jnp.zeros((b, A_COLS), F32), jnp.zeros((b, A_HEADS, A_HEAD_DIM, A_HEAD_DIM), F32),
            jnp.zeros((b, CONV_HIST, B_WIDTH), F32), attend_prompt, lp, consts, lam_init)
        xs, k_s, v_s, wkv_s, shift_s, conv_s = _mixers(
            xs, n_s, t_s, state_shift[l], state_wkv[l], state_conv[l], attend_sample, lp, consts, lam_init)

        i = l // 2
        g2 = norm2_g[l].reshape(1, -1)
        if l % 2 == 0:
            wg = ffn_w_gate[i].astype(BF16)[None]
            wu = ffn_w_up[i].astype(BF16)[None]
            wd = ffn_w_down[i].astype(BF16)[None]
            xp = _ffn(xp, g2, jnp.ones((xp.shape[0], LANES), F32), wg, wu, wd)
            xs = _ffn(xs, g2, jnp.ones((xs.shape[0], LANES), F32), wg, wu, wd)
        else:
            wg = moe_w_gate[i].astype(BF16)
            wu = moe_w_up[i].astype(BF16)
            wd = moe_w_down[i].astype(BF16)
            router_pad = jnp.pad(moe_router[i], ((0, 0), (0, LANES - N_EXPERTS)))
            tw_p, tw_s = _moe_window(xp.shape[0]), _moe_window(xs.shape[0])
            xp = _moe(*_router(xp, g2, router_pad, tw_p), xp, wg, wu, wd, tw_p)
            xs = _moe(*_router(xs, g2, router_pad, tw_s), xs, wg, wu, wd, tw_s)

        for lst, val in zip(outs, (k_p, v_p, k_s, v_s, wkv_p, wkv_s, shift_p, shift_s, conv_p, conv_s)):
            lst.append(val)

    y_prompt = xp.reshape(b, t_p, D_MODEL)[:, N_META:]
    y_sample = xs.reshape(n_s, t_s, D_MODEL)
    return (y_prompt, y_sample) + tuple(jnp.stack(o) for o in outs)
```

```python
import functools
import math

import jax
import jax.numpy as jnp
from jax import lax
from jax.experimental import pallas as pl
from jax.experimental.pallas import tpu as pltpu

F32 = jnp.float32
BF16 = jnp.bfloat16
HIGHEST = lax.Precision.HIGHEST

D_MODEL = 1024
EPS = 1e-6
N_META = 16
A_HEAD_DIM = 64
A_WIDTH = 256
A_HEADS = 4
A_W_LORA = 64
A_A_LORA = 64
A_G_LORA = 128
A_COLS = 1024
A_GN_EPS = 64e-5
B_WIDTH = 256
B_COLS = 512
CONV_WIDTH = 31
CONV_HIST = CONV_WIDTH - 1
C_QK_DIM = 64
C_V_DIM = 128
C_WIDTH = 512
C_HEADS = 4
C_QK_COLS = 512
IN_COLS = 3072
Q_OFF = A_COLS + B_COLS
K_OFF = Q_OFF + C_QK_COLS
V_OFF = K_OFF + C_QK_COLS
AB_COLS = A_COLS + B_COLS
D_FF = 2816
N_EXPERTS = 8
PAGE_SIZE = 128
LANES = 128
NEG_BIG = -1e30
ALIBI_SLOPES = tuple(float((2.0 ** (-8.0 / C_HEADS)) ** (h + 1)) for h in range(C_HEADS))
EXP_NEG_HALF = math.exp(-0.5)
SCAN_SEQS = 8
SCAN_COLS = 7 * A_WIDTH
SCAN_SUB = 8
TOP_K = 2
MIN_TIME_MAJOR_T = 64
MOE_MAX_WINDOW = 704
MOE_CAP_SLACK_PCT = 111
VMEM_LIMIT = 56 * 1024 * 1024


def _params(*sem):
    return pltpu.CompilerParams(dimension_semantics=sem, vmem_limit_bytes=VMEM_LIMIT)


def _pick_tile(n, target, mult=8):
    best = None
    for d in range(mult, min(n, target) + 1, mult):
        if n % d == 0:
            best = d
    return n if best is None else best


def _dot(a, b):
    return jnp.dot(a, b, preferred_element_type=F32)


def _dot_hi(a, b):
    return jnp.dot(a, b, preferred_element_type=F32, precision=HIGHEST)


def _dot_nt(a, b):
    return lax.dot_general(a, b, (((1,), (1,)), ((), ())), preferred_element_type=F32)


def _sigmoid(x):
    return 1.0 / (1.0 + jnp.exp(-x))


def _seg_sum(x, bd2):
    hi = x.astype(BF16)
    lo = (x - hi.astype(F32)).astype(BF16)
    return _dot(jnp.concatenate([hi, lo], axis=-1), bd2)


def _block_diag2(width, seg=64):
    r = jnp.arange(width) // seg
    bd = (r[:, None] == r[None, :]).astype(BF16)
    return jnp.concatenate([bd, bd], axis=0)


def _full(shape):
    nd = len(shape)
    return pl.BlockSpec(shape, lambda *_: (0,) * nd)


def _in_proj_kernel(x_ref, g_ref, w_ref, qg_ref, kg_ref, bd_ref, z_ref, qn_ref, kn_ref, kb_ref, v_ref, vb_ref):
    x = x_ref[...]
    h = x * lax.rsqrt(jnp.mean(x * x, axis=-1, keepdims=True) + EPS) * g_ref[...]
    z = _dot(h.astype(BF16), w_ref[...])
    z_ref[...] = z[:, :AB_COLS]
    v_ref[...] = z[:, V_OFF:]
    q = z[:, Q_OFF:Q_OFF + C_QK_COLS]
    k = z[:, K_OFF:K_OFF + C_QK_COLS]
    bd = bd_ref[...]
    inv = 1.0 / C_QK_DIM
    qn = q * lax.rsqrt(_seg_sum(q * q, bd) * inv + EPS) * qg_ref[...]
    kn = k * lax.rsqrt(_seg_sum(k * k, bd) * inv + EPS) * kg_ref[...]
    qn_ref[...] = qn * (C_QK_DIM ** -0.5)
    kn_ref[...] = kn
    kb_ref[...] = kn.astype(BF16)
    vb_ref[...] = z[:, V_OFF:].astype(BF16)


def _in_proj(x, g, w_bf, qg, kg, bd512):
    rows = x.shape[0]
    tm = _pick_tile(rows, 256, 16)
    row = lambda c: pl.BlockSpec((tm, c), lambda i: (i, 0))
    return pl.pallas_call(
        _in_proj_kernel,
        grid=(rows // tm,),
        in_specs=[row(D_MODEL), _full((1, D_MODEL)), _full((D_MODEL, IN_COLS)),
                  _full((1, C_QK_COLS)), _full((1, C_QK_COLS)), _full((2 * C_QK_COLS, C_QK_COLS))],
        out_specs=[row(AB_COLS), row(C_QK_COLS), row(C_QK_COLS), row(C_QK_COLS), row(C_WIDTH), row(C_WIDTH)],
        out_shape=[jax.ShapeDtypeStruct((rows, AB_COLS), F32),
                   jax.ShapeDtypeStruct((rows, C_QK_COLS), F32),
                   jax.ShapeDtypeStruct((rows, C_QK_COLS), F32),
                   jax.ShapeDtypeStruct((rows, C_QK_COLS), BF16),
                   jax.ShapeDtypeStruct((rows, C_WIDTH), F32),
                   jax.ShapeDtypeStruct((rows, C_WIDTH), BF16)],
        compiler_params=_params("parallel"),
        name="in_proj",
    )(x, g, w_bf, qg, kg, bd512)


def _rwkv_pre_kernel(za_ref, bnd_ref, mu_ref, w0_ref, wup_ref, a0_ref, aup_ref, gup_ref,
                     kk_ref, ka_ref, rk_ref, bd_ref, sc_ref, gb_ref, *, period):
    za = za_ref[...]
    rid = lax.broadcasted_iota(jnp.int32, za.shape, 0)
    at_boundary = (rid == 0) if period == za.shape[0] else (jnp.bitwise_and(rid, period - 1) == 0)
    prev = jnp.where(at_boundary, bnd_ref[...].reshape(-1, A_COLS), pltpu.roll(za, 1, 0))
    zs = za + mu_ref[...] * (prev - za)
    w = A_WIDTH
    r = zs[:, 0:w]
    k = zs[:, w:2 * w]
    v = zs[:, 2 * w:3 * w]
    wd = zs[:, 3 * w:3 * w + A_W_LORA]
    ad = zs[:, 3 * w + A_W_LORA:3 * w + A_W_LORA + A_A_LORA]
    gd = zs[:, 3 * w + A_W_LORA + A_A_LORA:]
    lw = w0_ref[...] + _dot_hi(jnp.tanh(wd), wup_ref[...])
    decay = jnp.exp(-EXP_NEG_HALF * _sigmoid(lw))
    a = _sigmoid(a0_ref[...] + _dot_hi(ad, aup_ref[...]))
    g = _dot(_sigmoid(gd).astype(BF16), gup_ref[...].astype(BF16))
    bd = bd_ref[...]
    kk = k * kk_ref[...]
    k2 = k * (1.0 + (a - 1.0) * ka_ref[...])
    kk = kk * lax.rsqrt(jnp.maximum(_seg_sum(kk * kk, bd), 1e-24))
    bonus = _seg_sum(r * k2 * rk_ref[...], bd) * v
    b = kk * a
    sc_ref[:, 0:w] = decay
    sc_ref[:, w:2 * w] = kk
    sc_ref[:, 2 * w:3 * w] = b
    sc_ref[:, 3 * w:4 * w] = k2
    sc_ref[:, 4 * w:5 * w] = v
    sc_ref[:, 5 * w:6 * w] = decay * r - kk * _seg_sum(b * r, bd)
    sc_ref[:, 6 * w:7 * w] = _seg_sum(k2 * r, bd)
    gb_ref[:, 0:w] = g
    gb_ref[:, w:2 * w] = bonus


def _time_major_spec(tm, cols, tiles_per_seq):
    return pl.BlockSpec((tm, cols), lambda i: (i % tiles_per_seq, i // tiles_per_seq))


def _seq_tile(t):
    return _pick_tile(t, 512, 8) if t >= MIN_TIME_MAJOR_T else None


def _rwkv_pre(z, shift0, lp, bd256, n, t):
    rows = z.shape[0]
    tm = _seq_tile(t) or _pick_tile(rows, 512, 8)
    row = lambda c: pl.BlockSpec((tm, c), lambda i: (i, 0))
    vec = lambda a: a.reshape(1, -1)
    if _seq_tile(t):
        tiles = t // tm
        last = z.reshape(n, t, AB_COLS)[:, tm - 1::tm, :A_COLS][:, :tiles - 1]
        bnd = jnp.concatenate([shift0[:, None, :], last], axis=1).reshape(n * tiles, 1, A_COLS)
        bnd_spec, period = pl.BlockSpec((1, 1, A_COLS), lambda i: (i, 0, 0)), tm
        sc_spec, sc_shape = _time_major_spec(tm, SCAN_COLS, tiles), (t, n * SCAN_COLS)
    else:
        assert t & (t - 1) == 0 and tm % t == 0
        bnd = jnp.repeat(shift0, t, axis=0)
        bnd_spec, period = row(A_COLS), t
        sc_spec, sc_shape = row(SCAN_COLS), (rows, SCAN_COLS)
    args = (z, bnd, vec(lp['a_mu']), vec(lp['a_w0']), lp['a_w_up'], vec(lp['a_a0']), lp['a_a_up'],
            lp['a_g_up'], vec(lp['a_k_k']), vec(lp['a_k_a']), vec(lp['a_r_k']), bd256)
    in_specs = [row(A_COLS), bnd_spec] + [_full(a.shape) for a in args[2:]]
    sc, gb = pl.pallas_call(
        functools.partial(_rwkv_pre_kernel, period=period),
        grid=(rows // tm,),
        in_specs=in_specs,
        out_specs=[sc_spec, row(2 * A_WIDTH)],
        out_shape=[jax.ShapeDtypeStruct(sc_shape, F32),
                   jax.ShapeDtypeStruct((rows, 2 * A_WIDTH), F32)],
        compiler_params=_params("parallel"),
        name="rwkv_pre",
    )(*args)
    if _seq_tile(t):
        return sc.reshape(t, n, SCAN_COLS), gb
    return sc.reshape(n, t, SCAN_COLS).transpose(1, 0, 2), gb


def _rwkv_scan_kernel(x_ref, s0_ref, bd_ref, eye_ref, eyeb_ref, y_ref, s1_ref, s_scr, vb_scr, *, tb, sub):
    nseq = x_ref.shape[1]
    npair = 2 * nseq
    w = A_WIDTH
    rows = A_HEAD_DIM * npair

    @pl.when(pl.program_id(1) == 0)
    def _():
        s_scr[...] = s0_ref[0]

    bd2 = bd_ref[...]
    bd1 = bd2[0:LANES]
    eye_bf = eyeb_ref[...]

    def rowvec(xt, c):
        return jnp.concatenate([xt[:, c:c + LANES], xt[:, c + LANES:c + 2 * LANES]], axis=0)

    def value_bcast(t_src, slot, row):
        v = rowvec(x_ref[t_src], 4 * w).astype(BF16)
        vb = _dot((eye_bf * v[None]).reshape(rows, LANES), bd1)
        vb_scr[slot, row] = vb.reshape(A_HEAD_DIM, npair, LANES)

    def seg1(a):
        nv = a.shape[0]
        return _dot(a.reshape(nv * npair, LANES).astype(BF16), bd1).reshape(nv, npair, LANES)

    for tt in range(sub):
        value_bcast(tt, 0, tt)

    def step(t, carry):
        slot = (t // sub) % 2
        row = t % sub
        value_bcast(jnp.minimum(t + sub, tb - 1), 1 - slot, row)
        xt = x_ref[t]
        dec, kk, b, k, v, rq, kr = (rowvec(xt, i * w) for i in range(7))
        s = s_scr[...]
        sa = seg1(s * kk[None])
        yq = seg1(s * rq[None])
        s_scr[...] = s * dec[None] - sa * b[None] + vb_scr[slot, row] * k[None]
        y = jnp.sum(yq * eye_ref[...], axis=0) + v * kr
        y_ref[t] = jnp.concatenate([y[0:nseq], y[nseq:npair]], axis=-1)
        return carry

    lax.fori_loop(0, tb, step, 0, unroll=2)

    @pl.when(pl.program_id(1) == pl.num_programs(1) - 1)
    def _():
        s1_ref[0] = s_scr[...]


def _rwkv_scan(sc_t, s0, bd128, eye):
    t, n, _ = sc_t.shape
    nb = s0.shape[2] // 2
    tb = _pick_tile(t, 64, SCAN_SUB)
    assert tb % SCAN_SUB == 0
    state_spec = pl.BlockSpec((1, A_HEAD_DIM, 2 * nb, LANES), lambda i, j: (i, 0, 0, 0))
    return pl.pallas_call(
        functools.partial(_rwkv_scan_kernel, tb=tb, sub=SCAN_SUB),
        grid=(n // nb, t // tb),
        in_specs=[pl.BlockSpec((tb, nb, SCAN_COLS), lambda i, j: (j, i, 0)),
                  state_spec,
                  _full((2 * LANES, LANES)), _full((A_HEAD_DIM, 2 * nb, LANES)),
                  _full((A_HEAD_DIM, 2 * nb, LANES))],
        out_specs=[pl.BlockSpec((tb, nb, A_WIDTH), lambda i, j: (j, i, 0)), state_spec],
        out_shape=[jax.ShapeDtypeStruct((t, n, A_WIDTH), F32),
                   jax.ShapeDtypeStruct(s0.shape, F32)],
        scratch_shapes=[pltpu.VMEM((A_HEAD_DIM, 2 * nb, LANES), F32),
                        pltpu.VMEM((2, SCAN_SUB, A_HEAD_DIM, 2 * nb, LANES), F32)],
        compiler_params=_params("parallel", "arbitrary"),
        name="rwkv_scan",
    )(sc_t, s0, bd128, eye, eye.astype(BF16))


_CONV_PAD = 32


def _conv_kernel(zb_ref, c0_ref, cw_ref, cb_ref, lg_ref, lb_ref, pw_ref, pb_ref,
                 y_ref, c1_ref, ext_scr, h_scr, *, nb, t, tc):
    cw = cw_ref[...]
    for s in range(nb):
        zb = zb_ref[s * t:(s + 1) * t, :]
        u = zb[:, :B_WIDTH] * _sigmoid(zb[:, B_WIDTH:])
        ext_scr[s, _CONV_PAD - CONV_HIST:_CONV_PAD, :] = c0_ref[s]
        ext_scr[s, _CONV_PAD:_CONV_PAD + t, :] = u
        c1_ref[s] = ext_scr[s, t + _CONV_PAD - CONV_HIST:t + _CONV_PAD, :]

        def chunk(c, carry, s=s):
            base = pl.multiple_of(c * tc, 8)
            win = ext_scr[s, pl.ds(base, tc + _CONV_PAD), :]
            acc = jnp.zeros((tc, B_WIDTH), F32)
            first = _CONV_PAD - CONV_HIST
            for b in range(8):
                taps = [j for j in range(CONV_WIDTH) if (first + j) % 8 == b]
                if not taps:
                    continue
                span = max(first + j - b for j in taps) + tc
                wb = win[b:b + span, :]
                for j in taps:
                    o = first + j - b
                    acc = acc + wb[o:o + tc, :] * cw[j:j + 1, :]
            acc = acc + cb_ref[...]
            xc = acc - jnp.mean(acc, axis=-1, keepdims=True)
            hn = xc * lax.rsqrt(jnp.mean(xc * xc, axis=-1, keepdims=True) + EPS)
            hn = hn * lg_ref[...] + lb_ref[...]
            hn = hn * _sigmoid(hn)
            h_scr[pl.ds(pl.multiple_of(s * t + c * tc, 8), tc), :] = hn
            return carry

        lax.fori_loop(0, t // tc, chunk, 0)
    y_ref[...] = _dot(h_scr[...].astype(BF16), pw_ref[...]) + pb_ref[...]


def _conformer_conv(z, conv0, lp, pw_bf, n, t):
    nb = 1 if t >= 64 else _pick_tile(n, 16, 1)
    tc = _pick_tile(t, 48, 8)
    rows = nb * t
    vec = lambda a: a.reshape(1, -1)
    return pl.pallas_call(
        functools.partial(_conv_kernel, nb=nb, t=t, tc=tc),
        grid=(n // nb,),
        in_specs=[pl.BlockSpec((rows, B_COLS), lambda i: (i, A_COLS // B_COLS)),
                  pl.BlockSpec((nb, CONV_HIST, B_WIDTH), lambda i: (i, 0, 0)),
                  _full((CONV_WIDTH, B_WIDTH)), _full((1, B_WIDTH)), _full((1, B_WIDTH)),
                  _full((1, B_WIDTH)), _full((B_WIDTH, B_WIDTH)), _full((1, B_WIDTH))],
        out_specs=[pl.BlockSpec((rows, B_WIDTH), lambda i: (i, 0)),
                   pl.BlockSpec((nb, CONV_HIST, B_WIDTH), lambda i: (i, 0, 0))],
        out_shape=[jax.ShapeDtypeStruct((n * t, B_WIDTH), F32),
                   jax.ShapeDtypeStruct((n, CONV_HIST, B_WIDTH), F32)],
        scratch_shapes=[pltpu.VMEM((nb, t + _CONV_PAD, B_WIDTH), F32),
                        pltpu.VMEM((rows, B_WIDTH), F32)],
        compiler_params=_params("parallel"),
        name="conformer_conv",
    )(z, conv0, lp['b_conv_w'], vec(lp['b_conv_b']), vec(lp['b_ln_g']), vec(lp['b_ln_b']),
      pw_bf, vec(lp['b_pw_b']))


def _lambda_from(lamv_ref, lam_init):
    lv = lamv_ref[...]
    s1 = jnp.sum(lv[0:1] * lv[1:2], axis=-1, keepdims=True)
    s2 = jnp.sum(lv[2:3] * lv[3:4], axis=-1, keepdims=True)
    return jnp.exp(s1) - jnp.exp(s2) + lam_init


def _sub_ln(o, sg, lam_init):
    o = o * lax.rsqrt(jnp.mean(o * o, axis=-1, keepdims=True) + EPS) * sg
    return o * (1.0 - lam_init)


def _map_masks(rows):
    lane = lax.broadcasted_iota(jnp.int32, (rows, LANES), 1)
    return lane < C_QK_DIM, lane >= C_QK_DIM


def _attn_prompt_kernel(q_ref, k_ref, v_ref, lamv_ref, sg_ref, o_ref, m_scr, l_scr, acc_scr,
                        *, tq, lam_init):
    qi = pl.program_id(1)
    kj = pl.program_id(2)

    @pl.when(kj == 0)
    def _():
        m_scr[...] = jnp.full(m_scr.shape, NEG_BIG, F32)
        l_scr[...] = jnp.zeros(l_scr.shape, F32)
        acc_scr[...] = jnp.zeros(acc_scr.shape, F32)

    @pl.when(kj <= qi)
    def _():
        qpos = qi * tq + lax.broadcasted_iota(jnp.int32, (tq, tq), 0)
        kpos = kj * tq + lax.broadcasted_iota(jnp.int32, (tq, tq), 1)
        dist = (qpos - kpos).astype(F32)
        causal = kpos <= qpos
        masks = _map_masks(tq)
        for h in range(C_HEADS):
            qh = q_ref[:, h * LANES:(h + 1) * LANES]
            kh = k_ref[:, h * LANES:(h + 1) * LANES]
            vh = v_ref[:, h * LANES:(h + 1) * LANES]
            for m in range(2):
                i = 2 * h + m
                qm = jnp.where(masks[m], qh, 0.0).astype(BF16)
                s = _dot_nt(qm, kh) - ALIBI_SLOPES[h] * dist
                s = jnp.where(causal, s, NEG_BIG)
                m_prev = m_scr[i]
                m_new = jnp.maximum(m_prev, jnp.max(s, axis=-1, keepdims=True))
                alpha = jnp.exp(m_prev - m_new)
                p = jnp.exp(s - m_new)
                l_scr[i] = alpha * l_scr[i] + jnp.sum(p, axis=-1, keepdims=True)
                acc_scr[i] = alpha * acc_scr[i] + _dot(p.astype(BF16), vh)
                m_scr[i] = m_new

    @pl.when(kj == qi)
    def _():
        lam = _lambda_from(lamv_ref, lam_init)
        for h in range(C_HEADS):
            o = acc_scr[2 * h] / l_scr[2 * h] - lam * (acc_scr[2 * h + 1] / l_scr[2 * h + 1])
            o_ref[:, h * LANES:(h + 1) * LANES] = _sub_ln(o, sg_ref[...], lam_init)


def _attn_prompt(qn, kb, vb, lamv, sg, n, t, lam_init):
    tq = _pick_tile(t, 704, 16)
    nq = t // tq
    return pl.pallas_call(
        functools.partial(_attn_prompt_kernel, tq=tq, lam_init=lam_init),
        grid=(n, nq, nq),
        in_specs=[pl.BlockSpec((tq, C_QK_COLS), lambda b, i, j: (b * nq + i, 0)),
                  pl.BlockSpec((tq, C_QK_COLS), lambda b, i, j: (b * nq + jnp.minimum(i, j), 0)),
                  pl.BlockSpec((tq, C_WIDTH), lambda b, i, j: (b * nq + jnp.minimum(i, j), 0)),
                  _full((4, C_QK_DIM)), _full((1, C_V_DIM))],
        out_specs=pl.BlockSpec((tq, C_WIDTH), lambda b, i, j: (b * nq + i, 0)),
        out_shape=jax.ShapeDtypeStruct((n * t, C_WIDTH), F32),
        scratch_shapes=[pltpu.VMEM((2 * C_HEADS, tq, 1), F32),
                        pltpu.VMEM((2 * C_HEADS, tq, 1), F32),
                        pltpu.VMEM((2 * C_HEADS, tq, C_V_DIM), F32)],
        compiler_params=_params("parallel", "parallel", "arbitrary"),
        name="attn_prompt",
    )(qn, kb, vb, lamv, sg)


def _attn_sample_kernel(pt_ref, q_ref, kn_ref, vn_ref, lamv_ref, sg_ref, tb_ref, cj_ref, tn_ref, *rest,
                        n_pages, t_s, lam_init):
    k_refs = rest[:n_pages]
    v_refs = rest[n_pages:2 * n_pages]
    o_ref = rest[2 * n_pages]
    del pt_ref
    rows = t_s * C_HEADS
    q = q_ref[0]
    masks = _map_masks(rows)
    qcat = jnp.concatenate([jnp.where(masks[0], q, 0.0), jnp.where(masks[1], q, 0.0),
                            jnp.zeros((LANES - 2 * rows, LANES), F32)], axis=0).astype(BF16)

    m_run = jnp.full((1, LANES), NEG_BIG, F32)
    l_run = jnp.zeros((1, LANES), F32)
    acc = jnp.zeros((C_V_DIM, LANES), F32)

    def update(kblk, vblk, bias, m_run, l_run, acc):
        s = _dot_nt(kblk.astype(BF16), qcat) + bias
        m_new = jnp.maximum(m_run, jnp.max(s, axis=0, keepdims=True))
        alpha = jnp.exp(m_run - m_new)
        p = jnp.exp(s - m_new)
        l_new = alpha * l_run + jnp.sum(p, axis=0, keepdims=True)
        acc_new = alpha * acc + _dot(vblk.T.astype(BF16), p.astype(BF16))
        return m_new, l_new, acc_new

    tbias = tb_ref[...]
    for j in range(n_pages):
        bias = tbias + cj_ref[j:j + 1, :]
        m_run, l_run, acc = update(k_refs[j][0], v_refs[j][0], bias, m_run, l_run, acc)
    m_run, l_run, acc = update(kn_ref[0], vn_ref[0], tn_ref[...], m_run, l_run, acc)

    lam = _lambda_from(lamv_ref, lam_init)
    o_all = (acc / l_run).T
    o = o_all[0:rows] - lam * o_all[rows:2 * rows]
    o_ref[0] = _sub_ln(o, sg_ref[...], lam_init)


def _sample_bias_tables(n_pages, t_s):
    rows = t_s * C_HEADS
    past = n_pages * PAGE_SIZE
    col = jnp.arange(LANES)
    used = col < 2 * rows
    c_r = col % rows
    c_t = c_r // C_HEADS
    c_h = c_r % C_HEADS
    slope = jnp.where(used, jnp.asarray(ALIBI_SLOPES, F32)[c_h], 0.0)
    prow = jnp.arange(PAGE_SIZE * C_HEADS)
    p_tok = prow // C_HEADS
    p_h = prow % C_HEADS
    ok = (p_h[:, None] == c_h[None, :]) | ~used[None, :]
    tb = jnp.where(ok, slope[None, :] * p_tok[:, None].astype(F32), NEG_BIG)
    starts = jnp.arange(n_pages) * PAGE_SIZE
    cj = -slope[None, :] * (past + c_t[None, :] - starts[:, None]).astype(F32)
    nrow = jnp.arange(rows)
    n_t = nrow // C_HEADS
    n_h = nrow % C_HEADS
    okn = ((n_h[:, None] == c_h[None, :]) & (n_t[:, None] <= c_t[None, :])) | ~used[None, :]
    tn = jnp.where(okn, -slope[None, :] * (c_t[None, :] - n_t[:, None]).astype(F32), NEG_BIG)
    return tb.astype(F32), cj.astype(F32), tn.astype(F32)


def _attn_sample(qn3, kn3, vn3, ck, cv, page_table, page_base, lamv, sg, tables, lam_init):
    n, rows, _ = qn3.shape
    n_pages = page_table.shape[1]
    pcols = PAGE_SIZE * C_HEADS
    assert 2 * rows <= LANES
    seq = lambda: pl.BlockSpec((1, rows, LANES), lambda i, pt: (i, 0, 0))
    page = lambda j: pl.BlockSpec((1, pcols, LANES), lambda i, pt, j=j: (page_base + pt[i, j], 0, 0))
    const = lambda a: pl.BlockSpec(a.shape, lambda i, pt: (0, 0))
    grid_spec = pltpu.PrefetchScalarGridSpec(
        num_scalar_prefetch=1,
        grid=(n,),
        in_specs=[seq(), seq(), seq(), const(lamv), const(sg)] + [const(a) for a in tables]
                 + [page(j) for j in range(n_pages)] + [page(j) for j in range(n_pages)],
        out_specs=seq(),
    )
    return pl.pallas_call(
        functools.partial(_attn_sample_kernel, n_pages=n_pages, t_s=rows // C_HEADS, lam_init=lam_init),
        grid_spec=grid_spec,
        out_shape=jax.ShapeDtypeStruct((n, rows, LANES), F32),
        compiler_params=_params("parallel"),
        name="attn_sample",
    )(page_table, qn3, kn3, vn3, lamv, sg, *tables, *([ck] * n_pages), *([cv] * n_pages))


def _out_proj_kernel(x_ref, y_ref, gb_ref, yb_ref, yc_ref, lg_ref, lb_ref, bd_ref, w_ref, o_ref):
    y = y_ref[...]
    bd = bd_ref[...]
    inv = 1.0 / A_HEAD_DIM
    yc = y - _seg_sum(y, bd) * inv
    yn = yc * lax.rsqrt(_seg_sum(yc * yc, bd) * inv + A_GN_EPS)
    ya = (yn * lg_ref[...] + lb_ref[...] + gb_ref[:, A_WIDTH:]) * gb_ref[:, :A_WIDTH]
    acc = _dot(ya.astype(BF16), w_ref[0:A_WIDTH, :])
    acc += _dot(yb_ref[...].astype(BF16), w_ref[A_WIDTH:A_WIDTH + B_WIDTH, :])
    acc += _dot(yc_ref[...].astype(BF16), w_ref[A_WIDTH + B_WIDTH:, :])
    o_ref[...] = x_ref[...] + acc


def _out_proj(x, y_t, gb, yb, yc, lp, bd256, w_bf):
    rows = x.shape[0]
    t, n, _ = y_t.shape
    tm = _seq_tile(t) or _pick_tile(rows, 512, 8)
    row = lambda c: pl.BlockSpec((tm, c), lambda i: (i, 0))
    vec = lambda a: a.reshape(1, -1)
    if _seq_tile(t):
        y, y_spec = y_t.reshape(t, n * A_WIDTH), _time_major_spec(tm, A_WIDTH, t // tm)
    else:
        y, y_spec = y_t.transpose(1, 0, 2).reshape(rows, A_WIDTH), row(A_WIDTH)
    return pl.pallas_call(
        _out_proj_kernel,
        grid=(rows // tm,),
        in_specs=[row(D_MODEL), y_spec, row(2 * A_WIDTH), row(B_WIDTH), row(C_WIDTH),
                  _full((1, A_WIDTH)), _full((1, A_WIDTH)), _full((2 * A_WIDTH, A_WIDTH)),
                  _full((D_MODEL, D_MODEL))],
        out_specs=row(D_MODEL),
        out_shape=jax.ShapeDtypeStruct((rows, D_MODEL), F32),
        compiler_params=_params("parallel"),
        name="out_proj",
    )(x, y, gb, yb, yc, vec(lp['a_lnx_g']), vec(lp['a_lnx_b']), bd256, w_bf)


GATE_LANE = 0
FLAG_LANE = N_EXPERTS
RANK_LANE = 2 * N_EXPERTS


def _router_kernel(x_ref, g_ref, r_ref, tri_ref, eye_ref, h_ref, rec_ref, rect_ref):
    x = x_ref[...]
    h = x * lax.rsqrt(jnp.mean(x * x, axis=-1, keepdims=True) + EPS) * g_ref[...]
    h_ref[...] = h.astype(BF16)
    logits = _dot_hi(h, r_ref[...])
    lane = lax.broadcasted_iota(jnp.int32, logits.shape, 1).astype(F32)
    lg = jnp.where(lane < N_EXPERTS, logits, NEG_BIG)
    m1 = jnp.max(lg, axis=-1, keepdims=True)
    i1 = jnp.min(jnp.where(lg == m1, lane, float(LANES)), axis=-1, keepdims=True)
    lg2 = jnp.where(lane == i1, NEG_BIG, lg)
    m2 = jnp.max(lg2, axis=-1, keepdims=True)
    i2 = jnp.min(jnp.where(lg2 == m2, lane, float(LANES)), axis=-1, keepdims=True)
    e = jnp.exp(m2 - m1)
    g1 = 1.0 / (1.0 + e)
    gates = jnp.where(lane == i1, g1, 0.0) + jnp.where(lane == i2, e * g1, 0.0)
    chosen = lambda off: jnp.where((lane == i1 + off) | (lane == i2 + off), 1.0, 0.0)
    rank = _dot(tri_ref[...], chosen(float(RANK_LANE)).astype(BF16))
    rec = gates + chosen(float(FLAG_LANE)) + rank
    rec_ref[...] = rec
    rect_ref[0] = lax.dot_general(eye_ref[...], rec, (((1,), (1,)), ((), ())),
                                  preferred_element_type=F32, precision=HIGHEST)


def _router(x, g, router_pad, tw):
    rows = x.shape[0]
    nw = rows // tw
    tri = (jnp.arange(tw)[:, None] > jnp.arange(tw)[None, :]).astype(BF16)
    eye = jnp.eye(LANES, dtype=F32)
    return pl.pallas_call(
        _router_kernel,
        grid=(nw,),
        in_specs=[pl.BlockSpec((tw, D_MODEL), lambda i: (i, 0)), _full((1, D_MODEL)),
                  _full((D_MODEL, LANES)), _full((tw, tw)), _full((LANES, LANES))],
        out_specs=[pl.BlockSpec((tw, D_MODEL), lambda i: (i, 0)),
                   pl.BlockSpec((tw, LANES), lambda i: (i, 0)),
                   pl.BlockSpec((1, LANES, tw), lambda i: (i, 0, 0))],
        out_shape=[jax.ShapeDtypeStruct((rows, D_MODEL), BF16),
                   jax.ShapeDtypeStruct((rows, LANES), F32),
                   jax.ShapeDtypeStruct((nw, LANES, tw), F32)],
        compiler_params=_params("parallel"),
        name="router",
    )(x, g, router_pad, tri, eye)


def _moe_kernel(h_ref, rec_ref, rect_ref, acc_ref, wg_ref, wu_ref, wd_ref, o_ref, *, e, cap, small):
    tw = h_ref.shape[0]
    rec = rec_ref[...]
    rect = rect_ref[0]
    lane = lax.broadcasted_iota(jnp.int32, rec.shape, 1)
    sub = lax.broadcasted_iota(jnp.int32, rect.shape, 0)
    col = lambda off: jnp.sum(jnp.where(lane == off + e, rec, 0.0), axis=-1, keepdims=True)
    row = lambda off: jnp.sum(jnp.where(sub == off + e, rect, 0.0), axis=0, keepdims=True)
    gate_c, flag_c, rank_c = col(GATE_LANE), col(FLAG_LANE), col(RANK_LANE)
    flag_r, rank_r = row(FLAG_LANE), row(RANK_LANE)
    o_ref[...] = acc_ref[...]
    count = jnp.max(rank_c + flag_c)

    def run_pass(base, size):
        slot_r = lax.broadcasted_iota(jnp.int32, (size, tw), 0).astype(F32)
        slot_c = lax.broadcasted_iota(jnp.int32, (tw, size), 1).astype(F32)
        sel = jnp.where((rank_r - base == slot_r) & (flag_r > 0.0), 1.0, 0.0).astype(BF16)
        sel_t = jnp.where((rank_c - base == slot_c) & (flag_c > 0.0), 1.0, 0.0).astype(BF16)
        xs = _dot(sel, h_ref[...]).astype(BF16)
        a = _dot(xs, wg_ref[0])
        u = _dot(xs, wu_ref[0])
        y = _dot((a * _sigmoid(a) * u).astype(BF16), wd_ref[0])
        o_ref[...] += gate_c * _dot(sel_t, y.astype(BF16))

    n_full = sum((count >= float((k + 1) * cap)).astype(jnp.int32) for k in range(tw // cap))

    def full_pass(p, carry):
        run_pass((p * cap).astype(F32), cap)
        return carry

    lax.fori_loop(0, n_full, full_pass, 0)
    done = (n_full * cap).astype(F32)
    rem = count - done

    @pl.when(rem > float(small))
    def _():
        run_pass(done, cap)

    @pl.when((rem > 0.0) & (rem <= float(small)))
    def _():
        run_pass(done, small)


def _moe_window(rows):
    best = None
    for d in range(16, min(rows, MOE_MAX_WINDOW) + 1, 16):
        if rows % d == 0:
            best = d
    assert best is not None, rows
    return best


def _moe(h, rec, rect, x, wg, wu, wd, tw):
    rows = x.shape[0]
    nw = rows // tw
    cap = -(-(tw * TOP_K * MOE_CAP_SLACK_PCT) // (N_EXPERTS * 100 * 16)) * 16
    small = -(-(2 * cap) // (3 * 16)) * 16
    win = lambda c: pl.BlockSpec((tw, c), lambda w: (w, 0))
    acc = x
    for e in range(wg.shape[0]):
        weight = lambda a, e=e: pl.BlockSpec((1,) + a.shape[1:], lambda w: (e, 0, 0),
                                             pipeline_mode=pl.Buffered(1))
        acc = pl.pallas_call(
            functools.partial(_moe_kernel, e=e, cap=cap, small=small),
            grid=(nw,),
            in_specs=[win(D_MODEL), win(LANES), pl.BlockSpec((1, LANES, tw), lambda w: (w, 0, 0)),
                      win(D_MODEL), weight(wg), weight(wu), weight(wd)],
            out_specs=win(D_MODEL),
            out_shape=jax.ShapeDtypeStruct((rows, D_MODEL), F32),
            compiler_params=_params("parallel"),
            name="moe",
        )(h, rec, rect, acc, wg, wu, wd)
    return acc


def _ffn_kernel(x_ref, g_ref, gate_ref, wg_ref, wu_ref, wd_ref, o_ref, h_scr, acc_scr):
    e = pl.program_id(1)
    f = pl.program_id(2)

    @pl.when((e == 0) & (f == 0))
    def _():
        x = x_ref[...]
        h = x * lax.rsqrt(jnp.mean(x * x, axis=-1, keepdims=True) + EPS) * g_ref[...]
        h_scr[...] = h.astype(BF16)
        acc_scr[...] = jnp.zeros(acc_scr.shape, F32)

    h = h_scr[...]
    a = _dot(h, wg_ref[0])
    u = _dot(h, wu_ref[0])
    act = (a * _sigmoid(a) * u).astype(BF16)
    lane = lax.broadcasted_iota(jnp.int32, gate_ref.shape, 1)
    gcol = jnp.sum(jnp.where(lane == e, gate_ref[...], 0.0), axis=-1, keepdims=True)
    acc_scr[...] += gcol * _dot(act, wd_ref[0])

    @pl.when((e == pl.num_programs(1) - 1) & (f == pl.num_programs(2) - 1))
    def _():
        o_ref[...] = x_ref[...] + acc_scr[...]


def _ffn(x, g, gates, wg, wu, wd):
    rows = x.shape[0]
    n_e = wg.shape[0]
    tm = _pick_tile(rows, 704, 16)
    tf = D_FF // 2
    return pl.pallas_call(
        _ffn_kernel,
        grid=(rows // tm, n_e, D_FF // tf),
        in_specs=[pl.BlockSpec((tm, D_MODEL), lambda i, e, f: (i, 0)),
                  pl.BlockSpec((1, D_MODEL), lambda i, e, f: (0, 0)),
                  pl.BlockSpec((tm, LANES), lambda i, e, f: (i, 0)),
                  pl.BlockSpec((1, D_MODEL, tf), lambda i, e, f: (e, 0, f)),
                  pl.BlockSpec((1, D_MODEL, tf), lambda i, e, f: (e, 0, f)),
                  pl.BlockSpec((1, tf, D_MODEL), lambda i, e, f: (e, f, 0))],
        out_specs=pl.BlockSpec((tm, D_MODEL), lambda i, e, f: (i, 0)),
        out_shape=jax.ShapeDtypeStruct((rows, D_MODEL), F32),
        scratch_shapes=[pltpu.VMEM((tm, D_MODEL), BF16), pltpu.VMEM((tm, D_MODEL), F32)],
        compiler_params=_params("parallel", "arbitrary", "arbitrary"),
        name="ffn",
    )(x, g, gates, wg, wu, wd)


def _wkv_to_pairs(s):
    n = s.shape[0]
    assert n % SCAN_SEQS == 0
    s = s.reshape(n // SCAN_SEQS, SCAN_SEQS, 2, 2, A_HEAD_DIM, A_HEAD_DIM)
    return s.transpose(0, 4, 2, 1, 3, 5).reshape(n // SCAN_SEQS, A_HEAD_DIM, 2 * SCAN_SEQS, LANES)


def _wkv_from_pairs(s):
    nblk = s.shape[0]
    s = s.reshape(nblk, A_HEAD_DIM, 2, SCAN_SEQS, 2, A_HEAD_DIM)
    return s.transpose(0, 3, 2, 4, 1, 5).reshape(nblk * SCAN_SEQS, A_HEADS, A_HEAD_DIM, A_HEAD_DIM)


def _mixers(x, n, t, shift0, wkv0, conv0, attend, lp, c, lam_init):
    z, qn, kn, kb, v, vb = _in_proj(x, lp['norm1_g'].reshape(1, -1), lp['w_in_bf'], lp['qg'], lp['kg'],
                                    c['bd512'])
    sc_t, gb = _rwkv_pre(z, shift0, lp, c['bd256'], n, t)
    y_t, s1 = _rwkv_scan(sc_t, _wkv_to_pairs(wkv0), c['bd128'], c['eye'])
    yb, conv1 = _conformer_conv(z, conv0, lp, lp['pw_bf'], n, t)
    yc = attend(qn, kn, kb, vb, v)
    x = _out_proj(x, y_t, gb, yb, yc, lp, c['bd256'], lp['w_out_bf'])
    k_rows = kn.reshape(n, t, C_HEADS, 2 * C_QK_DIM)
    v_rows = v.reshape(n, t, C_HEADS, C_V_DIM)
    shift1 = z.reshape(n, t, AB_COLS)[:, -1, :A_COLS]
    return x, k_rows, v_rows, _wkv_from_pairs(s1), shift1, conv1


def kernel(x_prompt, x_sample, cache_k, cache_v, page_table, state_wkv, state_shift, state_conv, meta_tokens, norm1_g, norm2_g, w_in, w_out, a_mu, a_w0, a_w_up, a_a0, a_a_up, a_g_up, a_k_k, a_k_a, a_r_k, a_lnx_g, a_lnx_b, b_conv_w, b_conv_b, b_ln_g, b_ln_b, b_pw_w, b_pw_b, c_qn_g, c_kn_g, c_lam_q1, c_lam_k1, c_lam_q2, c_lam_k2, c_subln_g, ffn_w_gate, ffn_w_up, ffn_w_down, moe_router, moe_w_gate, moe_w_up, moe_w_down):
    depth = w_in.shape[0]
    b, seq, _ = x_prompt.shape
    n_s, t_s, _ = x_sample.shape
    t_p = seq + N_META
    n_pool = cache_k.shape[1]
    pcols = PAGE_SIZE * C_HEADS

    consts = {
        'bd512': _block_diag2(C_QK_COLS),
        'bd256': _block_diag2(A_WIDTH),
        'bd128': _block_diag2(LANES),
        'eye': jnp.broadcast_to(
            (jnp.arange(LANES)[None, None, :] % A_HEAD_DIM == jnp.arange(A_HEAD_DIM)[:, None, None]).astype(F32),
            (A_HEAD_DIM, 2 * SCAN_SEQS, LANES)),
    }
    meta = jnp.broadcast_to(meta_tokens.astype(F32)[None], (b, N_META, D_MODEL))
    xp = jnp.concatenate([meta, x_prompt], axis=1).reshape(b * t_p, D_MODEL)
    xs = x_sample.reshape(n_s * t_s, D_MODEL)
    ck_all = cache_k.reshape(depth * n_pool, pcols, LANES)
    cv_all = cache_v.reshape(depth * n_pool, pcols, LANES)
    sample_tables = _sample_bias_tables(page_table.shape[1], t_s)

    outs = [[] for _ in range(10)]
    for l in range(depth):
        lam_init = 0.8 - 0.6 * math.exp(-0.3 * l)
        lp = {
            'norm1_g': norm1_g[l], 'w_in_bf': w_in[l].astype(BF16), 'w_out_bf': w_out[l].astype(BF16),
            'a_mu': a_mu[l], 'a_w0': a_w0[l], 'a_w_up': a_w_up[l], 'a_a0': a_a0[l], 'a_a_up': a_a_up[l],
            'a_g_up': a_g_up[l], 'a_k_k': a_k_k[l], 'a_k_a': a_k_a[l], 'a_r_k': a_r_k[l],
            'a_lnx_g': a_lnx_g[l], 'a_lnx_b': a_lnx_b[l],
            'b_conv_w': b_conv_w[l], 'b_conv_b': b_conv_b[l], 'b_ln_g': b_ln_g[l], 'b_ln_b': b_ln_b[l],
            'pw_bf': b_pw_w[l].astype(BF16), 'b_pw_b': b_pw_b[l],
            'qg': jnp.tile(c_qn_g[l], 2 * C_HEADS).reshape(1, -1),
            'kg': jnp.tile(c_kn_g[l], 2 * C_HEADS).reshape(1, -1),
        }
        lamv = jnp.stack([c_lam_q1[l], c_lam_k1[l], c_lam_q2[l], c_lam_k2[l]]).astype(F32)
        sg = c_subln_g[l].reshape(1, -1)

        def attend_prompt(qn, kn, kb, vb, v, lamv=lamv, sg=sg, lam_init=lam_init):
            return _attn_prompt(qn, kb, vb, lamv, sg, b, t_p, lam_init)

        def attend_sample(qn, kn, kb, vb, v, l=l, lamv=lamv, sg=sg, lam_init=lam_init):
            rows = t_s * C_HEADS
            qn3 = qn.reshape(n_s, rows, LANES)
            kn3 = kn.reshape(n_s, rows, LANES)
            vn3 = v.reshape(n_s, rows, LANES)
            o = _attn_sample(qn3, kn3, vn3, ck_all, cv_all, page_table, l * n_pool, lamv, sg, sample_tables,
                             lam_init)
            return o.reshape(n_s * t_s, C_WIDTH)

        xp, k_p, v_p, wkv_p, shift_p, conv_p = _mixers(
            xp, b, t_p, jnp.zeros((b, A_COLS), F32), jnp.zeros((b, A_HEADS, A_HEAD_DIM, A_HEAD_DIM), F32),
            jnp.zeros((b, CONV_HIST, B_WIDTH), F32), attend_prompt, lp, consts, lam_init)
        xs, k_s, v_s, wkv_s, shift_s, conv_s = _mixers(
            xs, n_s, t_s, state_shift[l], state_wkv[l], state_conv[l], attend_sample, lp, consts, lam_init)

        i = l // 2
        g2 = norm2_g[l].reshape(1, -1)
        if l % 2 == 0:
            wg = ffn_w_gate[i].astype(BF16)[None]
            wu = ffn_w_up[i].astype(BF16)[None]
            wd = ffn_w_down[i].astype(BF16)[None]
            xp = _ffn(xp, g2, jnp.ones((xp.shape[0], LANES), F32), wg, wu, wd)
            xs = _ffn(xs, g2, jnp.ones((xs.shape[0], LANES), F32), wg, wu, wd)
        else:
            wg = moe_w_gate[i].astype(BF16)
            wu = moe_w_up[i].astype(BF16)
            wd = moe_w_down[i].astype(BF16)
            router_pad = jnp.pad(moe_router[i], ((0, 0), (0, LANES - N_EXPERTS)))
            tw_p, tw_s = _moe_window(xp.shape[0]), _moe_window(xs.shape[0])
            xp = _moe(*_router(xp, g2, router_pad, tw_p), xp, wg, wu, wd, tw_p)
            xs = _moe(*_router(xs, g2, router_pad, tw_s), xs, wg, wu, wd, tw_s)

        for lst, val in zip(outs, (k_p, v_p, k_s, v_s, wkv_p, wkv_s, shift_p, shift_s, conv_p, conv_s)):
            lst.append(val)

    y_prompt = xp.reshape(b, t_p, D_MODEL)[:, N_META:]
    y_sample = xs.reshape(n_s, t_s, D_MODEL)
    return (y_prompt, y_sample) + tuple(jnp.stack(o) for o in outs)
```

```python
import functools
import math

import jax
import jax.numpy as jnp
from jax import lax
from jax.experimental import pallas as pl
from jax.experimental.pallas import tpu as pltpu

F32 = jnp.float32
BF16 = jnp.bfloat16
HIGHEST = lax.Precision.HIGHEST

D_MODEL = 1024
EPS = 1e-6
N_META = 16
A_HEAD_DIM = 64
A_WIDTH = 256
A_HEADS = 4
A_W_LORA = 64
A_A_LORA = 64
A_G_LORA = 128
A_COLS = 1024
A_GN_EPS = 64e-5
B_WIDTH = 256
B_COLS = 512
CONV_WIDTH = 31
CONV_HIST = CONV_WIDTH - 1
C_QK_DIM = 64
C_V_DIM = 128
C_WIDTH = 512
C_HEADS = 4
C_QK_COLS = 512
IN_COLS = 3072
Q_OFF = A_COLS + B_COLS
K_OFF = Q_OFF + C_QK_COLS
V_OFF = K_OFF + C_QK_COLS
AB_COLS = A_COLS + B_COLS
D_FF = 2816
N_EXPERTS = 8
PAGE_SIZE = 128
LANES = 128
NEG_BIG = -1e30
ALIBI_SLOPES = tuple(float((2.0 ** (-8.0 / C_HEADS)) ** (h + 1)) for h in range(C_HEADS))
EXP_NEG_HALF = math.exp(-0.5)
SCAN_SEQS = 8
SCAN_COLS = 7 * A_WIDTH
SCAN_SUB = 8
TOP_K = 2
MIN_TIME_MAJOR_T = 64
MOE_MAX_WINDOW = 704
MOE_CAP_SLACK_PCT = 111
VMEM_LIMIT = 56 * 1024 * 1024


def _params(*sem):
    return pltpu.CompilerParams(dimension_semantics=sem, vmem_limit_bytes=VMEM_LIMIT)


def _pick_tile(n, target, mult=8):
    best = None
    for d in range(mult, min(n, target) + 1, mult):
        if n % d == 0:
            best = d
    return n if best is None else best


def _dot(a, b):
    return jnp.dot(a, b, preferred_element_type=F32)


def _dot_hi(a, b):
    return jnp.dot(a, b, preferred_element_type=F32, precision=HIGHEST)


def _dot_nt(a, b):
    return lax.dot_general(a, b, (((1,), (1,)), ((), ())), preferred_element_type=F32)


def _sigmoid(x):
    return 1.0 / (1.0 + jnp.exp(-x))


def _seg_sum(x, bd2):
    hi = x.astype(BF16)
    lo = (x - hi.astype(F32)).astype(BF16)
    return _dot(jnp.concatenate([hi, lo], axis=-1), bd2)


def _seg_sum1(x, bd2):
    return _dot(x.astype(BF16), bd2[0:x.shape[-1]])


def _block_diag2(width, seg=64):
    r = jnp.arange(width) // seg
    bd = (r[:, None] == r[None, :]).astype(BF16)
    return jnp.concatenate([bd, bd], axis=0)


def _full(shape):
    nd = len(shape)
    return pl.BlockSpec(shape, lambda *_: (0,) * nd)


def _in_proj_kernel(x_ref, g_ref, w_ref, qg_ref, kg_ref, bd_ref, z_ref, qn_ref, kn_ref, kb_ref, v_ref, vb_ref):
    x = x_ref[...]
    h = x * lax.rsqrt(jnp.mean(x * x, axis=-1, keepdims=True) + EPS) * g_ref[...]
    z = _dot(h.astype(BF16), w_ref[...])
    z_ref[...] = z[:, :AB_COLS]
    v_ref[...] = z[:, V_OFF:]
    q = z[:, Q_OFF:Q_OFF + C_QK_COLS]
    k = z[:, K_OFF:K_OFF + C_QK_COLS]
    bd = bd_ref[...]
    inv = 1.0 / C_QK_DIM
    qn = q * lax.rsqrt(_seg_sum1(q * q, bd) * inv + EPS) * qg_ref[...]
    kn = k * lax.rsqrt(_seg_sum1(k * k, bd) * inv + EPS) * kg_ref[...]
    qn_ref[...] = qn * (C_QK_DIM ** -0.5)
    kn_ref[...] = kn
    kb_ref[...] = kn.astype(BF16)
    vb_ref[...] = z[:, V_OFF:].astype(BF16)


def _in_proj(x, g, w_bf, qg, kg, bd512):
    rows = x.shape[0]
    tm = _pick_tile(rows, 256, 16)
    row = lambda c: pl.BlockSpec((tm, c), lambda i: (i, 0))
    return pl.pallas_call(
        _in_proj_kernel,
        grid=(rows // tm,),
        in_specs=[row(D_MODEL), _full((1, D_MODEL)), _full((D_MODEL, IN_COLS)),
                  _full((1, C_QK_COLS)), _full((1, C_QK_COLS)), _full((2 * C_QK_COLS, C_QK_COLS))],
        out_specs=[row(AB_COLS), row(C_QK_COLS), row(C_QK_COLS), row(C_QK_COLS), row(C_WIDTH), row(C_WIDTH)],
        out_shape=[jax.ShapeDtypeStruct((rows, AB_COLS), F32),
                   jax.ShapeDtypeStruct((rows, C_QK_COLS), F32),
                   jax.ShapeDtypeStruct((rows, C_QK_COLS), F32),
                   jax.ShapeDtypeStruct((rows, C_QK_COLS), BF16),
                   jax.ShapeDtypeStruct((rows, C_WIDTH), F32),
                   jax.ShapeDtypeStruct((rows, C_WIDTH), BF16)],
        compiler_params=_params("parallel"),
        name="in_proj",
    )(x, g, w_bf, qg, kg, bd512)


def _rwkv_pre_kernel(za_ref, bnd_ref, mu_ref, w0_ref, wup_ref, a0_ref, aup_ref, gup_ref,
                     kk_ref, ka_ref, rk_ref, bd_ref, sc_ref, gb_ref, *, period):
    za = za_ref[...]
    rid = lax.broadcasted_iota(jnp.int32, za.shape, 0)
    at_boundary = (rid == 0) if period == za.shape[0] else (jnp.bitwise_and(rid, period - 1) == 0)
    prev = jnp.where(at_boundary, bnd_ref[...].reshape(-1, A_COLS), pltpu.roll(za, 1, 0))
    zs = za + mu_ref[...] * (prev - za)
    w = A_WIDTH
    r = zs[:, 0:w]
    k = zs[:, w:2 * w]
    v = zs[:, 2 * w:3 * w]
    wd = zs[:, 3 * w:3 * w + A_W_LORA]
    ad = zs[:, 3 * w + A_W_LORA:3 * w + A_W_LORA + A_A_LORA]
    gd = zs[:, 3 * w + A_W_LORA + A_A_LORA:]
    lw = w0_ref[...] + _dot_hi(jnp.tanh(wd), wup_ref[...])
    decay = jnp.exp(-EXP_NEG_HALF * _sigmoid(lw))
    a = _sigmoid(a0_ref[...] + _dot_hi(ad, aup_ref[...]))
    g = _dot(_sigmoid(gd).astype(BF16), gup_ref[...].astype(BF16))
    bd = bd_ref[...]
    kk = k * kk_ref[...]
    k2 = k * (1.0 + (a - 1.0) * ka_ref[...])
    kk = kk * lax.rsqrt(jnp.maximum(_seg_sum(kk * kk, bd), 1e-24))
    bonus = _seg_sum(r * k2 * rk_ref[...], bd) * v
    b = kk * a
    sc_ref[:, 0:w] = decay
    sc_ref[:, w:2 * w] = kk
    sc_ref[:, 2 * w:3 * w] = b
    sc_ref[:, 3 * w:4 * w] = k2
    sc_ref[:, 4 * w:5 * w] = v
    sc_ref[:, 5 * w:6 * w] = decay * r - kk * _seg_sum(b * r, bd)
    sc_ref[:, 6 * w:7 * w] = _seg_sum(k2 * r, bd)
    gb_ref[:, 0:w] = g
    gb_ref[:, w:2 * w] = bonus


def _time_major_spec(tm, cols, tiles_per_seq):
    return pl.BlockSpec((tm, cols), lambda i: (i % tiles_per_seq, i // tiles_per_seq))


def _seq_tile(t):
    return _pick_tile(t, 512, 8) if t >= MIN_TIME_MAJOR_T else None


def _rwkv_pre(z, shift0, lp, bd256, n, t):
    rows = z.shape[0]
    tm = _seq_tile(t) or _pick_tile(rows, 512, 8)
    row = lambda c: pl.BlockSpec((tm, c), lambda i: (i, 0))
    vec = lambda a: a.reshape(1, -1)
    if _seq_tile(t):
        tiles = t // tm
        last = z.reshape(n, t, AB_COLS)[:, tm - 1::tm, :A_COLS][:, :tiles - 1]
        bnd = jnp.concatenate([shift0[:, None, :], last], axis=1).reshape(n * tiles, 1, A_COLS)
        bnd_spec, period = pl.BlockSpec((1, 1, A_COLS), lambda i: (i, 0, 0)), tm
        sc_spec, sc_shape = _time_major_spec(tm, SCAN_COLS, tiles), (t, n * SCAN_COLS)
    else:
        assert t & (t - 1) == 0 and tm % t == 0
        bnd = jnp.repeat(shift0, t, axis=0)
        bnd_spec, period = row(A_COLS), t
        sc_spec, sc_shape = row(SCAN_COLS), (rows, SCAN_COLS)
    args = (z, bnd, vec(lp['a_mu']), vec(lp['a_w0']), lp['a_w_up'], vec(lp['a_a0']), lp['a_a_up'],
            lp['a_g_up'], vec(lp['a_k_k']), vec(lp['a_k_a']), vec(lp['a_r_k']), bd256)
    in_specs = [row(A_COLS), bnd_spec] + [_full(a.shape) for a in args[2:]]
    sc, gb = pl.pallas_call(
        functools.partial(_rwkv_pre_kernel, period=period),
        grid=(rows // tm,),
        in_specs=in_specs,
        out_specs=[sc_spec, row(2 * A_WIDTH)],
        out_shape=[jax.ShapeDtypeStruct(sc_shape, F32),
                   jax.ShapeDtypeStruct((rows, 2 * A_WIDTH), F32)],
        compiler_params=_params("parallel"),
        name="rwkv_pre",
    )(*args)
    if _seq_tile(t):
        return sc.reshape(t, n, SCAN_COLS), gb
    return sc.reshape(n, t, SCAN_COLS).transpose(1, 0, 2), gb


def _rwkv_scan_kernel(x_ref, s0_ref, bd_ref, eye_ref, eyeb_ref, y_ref, s1_ref, s_scr, vb_scr, *, tb, sub):
    nseq = x_ref.shape[1]
    npair = 2 * nseq
    w = A_WIDTH
    rows = A_HEAD_DIM * npair

    @pl.when(pl.program_id(1) == 0)
    def _():
        s_scr[...] = s0_ref[0]

    bd2 = bd_ref[...]
    bd1 = bd2[0:LANES]
    eye_bf = eyeb_ref[...]

    def rowvec(xt, c):
        return jnp.concatenate([xt[:, c:c + LANES], xt[:, c + LANES:c + 2 * LANES]], axis=0)

    def value_bcast(t_src, slot, row):
        v = rowvec(x_ref[t_src], 4 * w).astype(BF16)
        vb = _dot((eye_bf * v[None]).reshape(rows, LANES), bd1)
        vb_scr[slot, row] = vb.reshape(A_HEAD_DIM, npair, LANES)

    def seg1(a):
        nv = a.shape[0]
        return _dot(a.reshape(nv * npair, LANES).astype(BF16), bd1).reshape(nv, npair, LANES)

    for tt in range(sub):
        value_bcast(tt, 0, tt)

    def step(t, carry):
        slot = (t // sub) % 2
        row = t % sub
        value_bcast(jnp.minimum(t + sub, tb - 1), 1 - slot, row)
        xt = x_ref[t]
        dec, kk, b, k, v, rq, kr = (rowvec(xt, i * w) for i in range(7))
        s = s_scr[...]
        sa = seg1(s * kk[None])
        yq = seg1(s * rq[None])
        s_scr[...] = s * dec[None] - sa * b[None] + vb_scr[slot, row] * k[None]
        y = jnp.sum(yq * eye_ref[...], axis=0) + v * kr
        y_ref[t] = jnp.concatenate([y[0:nseq], y[nseq:npair]], axis=-1)
        return carry

    lax.fori_loop(0, tb, step, 0, unroll=2)

    @pl.when(pl.program_id(1) == pl.num_programs(1) - 1)
    def _():
        s1_ref[0] = s_scr[...]


def _rwkv_scan(sc_t, s0, bd128, eye):
    t, n, _ = sc_t.shape
    nb = s0.shape[2] // 2
    tb = _pick_tile(t, 64, SCAN_SUB)
    assert tb % SCAN_SUB == 0
    state_spec = pl.BlockSpec((1, A_HEAD_DIM, 2 * nb, LANES), lambda i, j: (i, 0, 0, 0))
    return pl.pallas_call(
        functools.partial(_rwkv_scan_kernel, tb=tb, sub=SCAN_SUB),
        grid=(n // nb, t // tb),
        in_specs=[pl.BlockSpec((tb, nb, SCAN_COLS), lambda i, j: (j, i, 0)),
                  state_spec,
                  _full((2 * LANES, LANES)), _full((A_HEAD_DIM, 2 * nb, LANES)),
                  _full((A_HEAD_DIM, 2 * nb, LANES))],
        out_specs=[pl.BlockSpec((tb, nb, A_WIDTH), lambda i, j: (j, i, 0)), state_spec],
        out_shape=[jax.ShapeDtypeStruct((t, n, A_WIDTH), F32),
                   jax.ShapeDtypeStruct(s0.shape, F32)],
        scratch_shapes=[pltpu.VMEM((A_HEAD_DIM, 2 * nb, LANES), F32),
                        pltpu.VMEM((2, SCAN_SUB, A_HEAD_DIM, 2 * nb, LANES), F32)],
        compiler_params=_params("parallel", "arbitrary"),
        name="rwkv_scan",
    )(sc_t, s0, bd128, eye, eye.astype(BF16))


_CONV_PAD = 32


def _conv_kernel(zb_ref, c0_ref, cw_ref, cb_ref, lg_ref, lb_ref, pw_ref, pb_ref,
                 y_ref, c1_ref, ext_scr, h_scr, *, nb, t, tc):
    cw = cw_ref[...]
    for s in range(nb):
        zb = zb_ref[s * t:(s + 1) * t, :]
        u = zb[:, :B_WIDTH] * _sigmoid(zb[:, B_WIDTH:])
        ext_scr[s, _CONV_PAD - CONV_HIST:_CONV_PAD, :] = c0_ref[s]
        ext_scr[s, _CONV_PAD:_CONV_PAD + t, :] = u
        c1_ref[s] = ext_scr[s, t + _CONV_PAD - CONV_HIST:t + _CONV_PAD, :]

        def chunk(c, carry, s=s):
            base = pl.multiple_of(c * tc, 8)
            win = ext_scr[s, pl.ds(base, tc + _CONV_PAD), :]
            acc = jnp.zeros((tc, B_WIDTH), F32)
            first = _CONV_PAD - CONV_HIST
            for b in range(8):
                taps = [j for j in range(CONV_WIDTH) if (first + j) % 8 == b]
                if not taps:
                    continue
                span = max(first + j - b for j in taps) + tc
                wb = win[b:b + span, :]
                for j in taps:
                    o = first + j - b
                    acc = acc + wb[o:o + tc, :] * cw[j:j + 1, :]
            acc = acc + cb_ref[...]
            xc = acc - jnp.mean(acc, axis=-1, keepdims=True)
            hn = xc * lax.rsqrt(jnp.mean(xc * xc, axis=-1, keepdims=True) + EPS)
            hn = hn * lg_ref[...] + lb_ref[...]
            hn = hn * _sigmoid(hn)
            h_scr[pl.ds(pl.multiple_of(s * t + c * tc, 8), tc), :] = hn
            return carry

        lax.fori_loop(0, t // tc, chunk, 0)
    y_ref[...] = _dot(h_scr[...].astype(BF16), pw_ref[...]) + pb_ref[...]


def _conformer_conv(z, conv0, lp, pw_bf, n, t):
    nb = 1 if t >= 64 else _pick_tile(n, 16, 1)
    tc = _pick_tile(t, 48, 8)
    rows = nb * t
    vec = lambda a: a.reshape(1, -1)
    return pl.pallas_call(
        functools.partial(_conv_kernel, nb=nb, t=t, tc=tc),
        grid=(n // nb,),
        in_specs=[pl.BlockSpec((rows, B_COLS), lambda i: (i, A_COLS // B_COLS)),
                  pl.BlockSpec((nb, CONV_HIST, B_WIDTH), lambda i: (i, 0, 0)),
                  _full((CONV_WIDTH, B_WIDTH)), _full((1, B_WIDTH)), _full((1, B_WIDTH)),
                  _full((1, B_WIDTH)), _full((B_WIDTH, B_WIDTH)), _full((1, B_WIDTH))],
        out_specs=[pl.BlockSpec((rows, B_WIDTH), lambda i: (i, 0)),
                   pl.BlockSpec((nb, CONV_HIST, B_WIDTH), lambda i: (i, 0, 0))],
        out_shape=[jax.ShapeDtypeStruct((n * t, B_WIDTH), F32),
                   jax.ShapeDtypeStruct((n, CONV_HIST, B_WIDTH), F32)],
        scratch_shapes=[pltpu.VMEM((nb, t + _CONV_PAD, B_WIDTH), F32),
                        pltpu.VMEM((rows, B_WIDTH), F32)],
        compiler_params=_params("parallel"),
        name="conformer_conv",
    )(z, conv0, lp['b_conv_w'], vec(lp['b_conv_b']), vec(lp['b_ln_g']), vec(lp['b_ln_b']),
      pw_bf, vec(lp['b_pw_b']))


def _lambda_from(lamv_ref, lam_init):
    lv = lamv_ref[...]
    s1 = jnp.sum(lv[0:1] * lv[1:2], axis=-1, keepdims=True)
    s2 = jnp.sum(lv[2:3] * lv[3:4], axis=-1, keepdims=True)
    return jnp.exp(s1) - jnp.exp(s2) + lam_init


def _sub_ln(o, sg, lam_init):
    o = o * lax.rsqrt(jnp.mean(o * o, axis=-1, keepdims=True) + EPS) * sg
    return o * (1.0 - lam_init)


def _map_masks(rows):
    lane = lax.broadcasted_iota(jnp.int32, (rows, LANES), 1)
    return lane < C_QK_DIM, lane >= C_QK_DIM


def _attn_prompt_kernel(q_ref, k_ref, v_ref, lamv_ref, sg_ref, o_ref, m_scr, l_scr, acc_scr,
                        *, tq, lam_init):
    qi = pl.program_id(1)
    kj = pl.program_id(2)

    @pl.when(kj == 0)
    def _():
        m_scr[...] = jnp.full(m_scr.shape, NEG_BIG, F32)
        l_scr[...] = jnp.zeros(l_scr.shape, F32)
        acc_scr[...] = jnp.zeros(acc_scr.shape, F32)

    @pl.when(kj <= qi)
    def _():
        qpos = qi * tq + lax.broadcasted_iota(jnp.int32, (tq, tq), 0)
        kpos = kj * tq + lax.broadcasted_iota(jnp.int32, (tq, tq), 1)
        dist = (qpos - kpos).astype(F32)
        causal = kpos <= qpos
        masks = _map_masks(tq)
        for h in range(C_HEADS):
            qh = q_ref[:, h * LANES:(h + 1) * LANES]
            kh = k_ref[:, h * LANES:(h + 1) * LANES]
            vh = v_ref[:, h * LANES:(h + 1) * LANES]
            for m in range(2):
                i = 2 * h + m
                qm = jnp.where(masks[m], qh, 0.0).astype(BF16)
                s = _dot_nt(qm, kh) - ALIBI_SLOPES[h] * dist
                s = jnp.where(causal, s, NEG_BIG)
                m_prev = m_scr[i]
                m_new = jnp.maximum(m_prev, jnp.max(s, axis=-1, keepdims=True))
                alpha = jnp.exp(m_prev - m_new)
                p = jnp.exp(s - m_new)
                l_scr[i] = alpha * l_scr[i] + jnp.sum(p, axis=-1, keepdims=True)
                acc_scr[i] = alpha * acc_scr[i] + _dot(p.astype(BF16), vh)
                m_scr[i] = m_new

    @pl.when(kj == qi)
    def _():
        lam = _lambda_from(lamv_ref, lam_init)
        for h in range(C_HEADS):
            o = acc_scr[2 * h] / l_scr[2 * h] - lam * (acc_scr[2 * h + 1] / l_scr[2 * h + 1])
            o_ref[:, h * LANES:(h + 1) * LANES] = _sub_ln(o, sg_ref[...], lam_init)


def _attn_prompt(qn, kb, vb, lamv, sg, n, t, lam_init):
    tq = _pick_tile(t, 704, 16)
    nq = t // tq
    return pl.pallas_call(
        functools.partial(_attn_prompt_kernel, tq=tq, lam_init=lam_init),
        grid=(n, nq, nq),
        in_specs=[pl.BlockSpec((tq, C_QK_COLS), lambda b, i, j: (b * nq + i, 0)),
                  pl.BlockSpec((tq, C_QK_COLS), lambda b, i, j: (b * nq + jnp.minimum(i, j), 0)),
                  pl.BlockSpec((tq, C_WIDTH), lambda b, i, j: (b * nq + jnp.minimum(i, j), 0)),
                  _full((4, C_QK_DIM)), _full((1, C_V_DIM))],
        out_specs=pl.BlockSpec((tq, C_WIDTH), lambda b, i, j: (b * nq + i, 0)),
        out_shape=jax.ShapeDtypeStruct((n * t, C_WIDTH), F32),
        scratch_shapes=[pltpu.VMEM((2 * C_HEADS, tq, 1), F32),
                        pltpu.VMEM((2 * C_HEADS, tq, 1), F32),
                        pltpu.VMEM((2 * C_HEADS, tq, C_V_DIM), F32)],
        compiler_params=_params("parallel", "parallel", "arbitrary"),
        name="attn_prompt",
    )(qn, kb, vb, lamv, sg)


def _attn_sample_kernel(pt_ref, q_ref, kn_ref, vn_ref, lamv_ref, sg_ref, tb_ref, cj_ref, tn_ref, *rest,
                        n_pages, t_s, lam_init):
    k_refs = rest[:n_pages]
    v_refs = rest[n_pages:2 * n_pages]
    o_ref = rest[2 * n_pages]
    del pt_ref
    rows = t_s * C_HEADS
    q = q_ref[0]
    masks = _map_masks(rows)
    qcat = jnp.concatenate([jnp.where(masks[0], q, 0.0), jnp.where(masks[1], q, 0.0),
                            jnp.zeros((LANES - 2 * rows, LANES), F32)], axis=0).astype(BF16)

    m_run = jnp.full((1, LANES), NEG_BIG, F32)
    l_run = jnp.zeros((1, LANES), F32)
    acc = jnp.zeros((C_V_DIM, LANES), F32)

    def update(kblk, vblk, bias, col_off, m_run, l_run, acc):
        s = _dot_nt(kblk.astype(BF16), qcat) + bias
        m_new = jnp.maximum(m_run, jnp.max(s, axis=0, keepdims=True) + col_off)
        alpha = jnp.exp(m_run - m_new)
        p = jnp.exp(s - (m_new - col_off))
        l_new = alpha * l_run + jnp.sum(p, axis=0, keepdims=True)
        acc_new = alpha * acc + _dot(vblk.T.astype(BF16), p.astype(BF16))
        return m_new, l_new, acc_new

    tbias = tb_ref[...]
    for j in range(n_pages):
        m_run, l_run, acc = update(k_refs[j][0], v_refs[j][0], tbias, cj_ref[j:j + 1, :], m_run, l_run, acc)
    m_run, l_run, acc = update(kn_ref[0], vn_ref[0], tn_ref[...], jnp.zeros((1, LANES), F32), m_run, l_run, acc)

    lam = _lambda_from(lamv_ref, lam_init)
    o_all = (acc / l_run).T
    o = o_all[0:rows] - lam * o_all[rows:2 * rows]
    o_ref[0] = _sub_ln(o, sg_ref[...], lam_init)


def _sample_bias_tables(n_pages, t_s):
    rows = t_s * C_HEADS
    past = n_pages * PAGE_SIZE
    col = jnp.arange(LANES)
    used = col < 2 * rows
    c_r = col % rows
    c_t = c_r // C_HEADS
    c_h = c_r % C_HEADS
    slope = jnp.where(used, jnp.asarray(ALIBI_SLOPES, F32)[c_h], 0.0)
    prow = jnp.arange(PAGE_SIZE * C_HEADS)
    p_tok = prow // C_HEADS
    p_h = prow % C_HEADS
    ok = (p_h[:, None] == c_h[None, :]) | ~used[None, :]
    tb = jnp.where(ok, slope[None, :] * p_tok[:, None].astype(F32), NEG_BIG)
    starts = jnp.arange(n_pages) * PAGE_SIZE
    cj = -slope[None, :] * (past + c_t[None, :] - starts[:, None]).astype(F32)
    nrow = jnp.arange(rows)
    n_t = nrow // C_HEADS
    n_h = nrow % C_HEADS
    okn = ((n_h[:, None] == c_h[None, :]) & (n_t[:, None] <= c_t[None, :])) | ~used[None, :]
    tn = jnp.where(okn, -slope[None, :] * (c_t[None, :] - n_t[:, None]).astype(F32), NEG_BIG)
    return tb.astype(F32), cj.astype(F32), tn.astype(F32)


def _attn_sample(qn3, kn3, vn3, ck, cv, page_table, page_base, lamv, sg, tables, lam_init):
    n, rows, _ = qn3.shape
    n_pages = page_table.shape[1]
    pcols = PAGE_SIZE * C_HEADS
    assert 2 * rows <= LANES
    seq = lambda: pl.BlockSpec((1, rows, LANES), lambda i, pt: (i, 0, 0))
    page = lambda j: pl.BlockSpec((1, pcols, LANES), lambda i, pt, j=j: (page_base + pt[i, j], 0, 0))
    const = lambda a: pl.BlockSpec(a.shape, lambda i, pt: (0, 0))
    grid_spec = pltpu.PrefetchScalarGridSpec(
        num_scalar_prefetch=1,
        grid=(n,),
        in_specs=[seq(), seq(), seq(), const(lamv), const(sg)] + [const(a) for a in tables]
                 + [page(j) for j in range(n_pages)] + [page(j) for j in range(n_pages)],
        out_specs=seq(),
    )
    return pl.pallas_call(
        functools.partial(_attn_sample_kernel, n_pages=n_pages, t_s=rows // C_HEADS, lam_init=lam_init),
        grid_spec=grid_spec,
        out_shape=jax.ShapeDtypeStruct((n, rows, LANES), F32),
        compiler_params=_params("parallel"),
        name="attn_sample",
    )(page_table, qn3, kn3, vn3, lamv, sg, *tables, *([ck] * n_pages), *([cv] * n_pages))


def _out_proj_kernel(x_ref, y_ref, gb_ref, yb_ref, yc_ref, lg_ref, lb_ref, bd_ref, w_ref, o_ref):
    y = y_ref[...]
    bd = bd_ref[...]
    inv = 1.0 / A_HEAD_DIM
    yc = y - _seg_sum(y, bd) * inv
    yn = yc * lax.rsqrt(_seg_sum1(yc * yc, bd) * inv + A_GN_EPS)
    ya = (yn * lg_ref[...] + lb_ref[...] + gb_ref[:, A_WIDTH:]) * gb_ref[:, :A_WIDTH]
    acc = _dot(ya.astype(BF16), w_ref[0:A_WIDTH, :])
    acc += _dot(yb_ref[...].astype(BF16), w_ref[A_WIDTH:A_WIDTH + B_WIDTH, :])
    acc += _dot(yc_ref[...].astype(BF16), w_ref[A_WIDTH + B_WIDTH:, :])
    o_ref[...] = x_ref[...] + acc


def _out_proj(x, y_t, gb, yb, yc, lp, bd256, w_bf):
    rows = x.shape[0]
    t, n, _ = y_t.shape
    tm = _seq_tile(t) or _pick_tile(rows, 512, 8)
    row = lambda c: pl.BlockSpec((tm, c), lambda i: (i, 0))
    vec = lambda a: a.reshape(1, -1)
    if _seq_tile(t):
        y, y_spec = y_t.reshape(t, n * A_WIDTH), _time_major_spec(tm, A_WIDTH, t // tm)
    else:
        y, y_spec = y_t.transpose(1, 0, 2).reshape(rows, A_WIDTH), row(A_WIDTH)
    return pl.pallas_call(
        _out_proj_kernel,
        grid=(rows // tm,),
        in_specs=[row(D_MODEL), y_spec, row(2 * A_WIDTH), row(B_WIDTH), row(C_WIDTH),
                  _full((1, A_WIDTH)), _full((1, A_WIDTH)), _full((2 * A_WIDTH, A_WIDTH)),
                  _full((D_MODEL, D_MODEL))],
        out_specs=row(D_MODEL),
        out_shape=jax.ShapeDtypeStruct((rows, D_MODEL), F32),
        compiler_params=_params("parallel"),
        name="out_proj",
    )(x, y, gb, yb, yc, vec(lp['a_lnx_g']), vec(lp['a_lnx_b']), bd256, w_bf)


GATE_LANE = 0
FLAG_LANE = N_EXPERTS
RANK_LANE = 2 * N_EXPERTS


def _router_kernel(x_ref, g_ref, r_ref, tri_ref, eye_ref, h_ref, rec_ref, rect_ref):
    x = x_ref[...]
    h = x * lax.rsqrt(jnp.mean(x * x, axis=-1, keepdims=True) + EPS) * g_ref[...]
    h_ref[...] = h.astype(BF16)
    logits = _dot_hi(h, r_ref[...])
    lane = lax.broadcasted_iota(jnp.int32, logits.shape, 1).astype(F32)
    lg = jnp.where(lane < N_EXPERTS, logits, NEG_BIG)
    m1 = jnp.max(lg, axis=-1, keepdims=True)
    i1 = jnp.min(jnp.where(lg == m1, lane, float(LANES)), axis=-1, keepdims=True)
    lg2 = jnp.where(lane == i1, NEG_BIG, lg)
    m2 = jnp.max(lg2, axis=-1, keepdims=True)
    i2 = jnp.min(jnp.where(lg2 == m2, lane, float(LANES)), axis=-1, keepdims=True)
    e = jnp.exp(m2 - m1)
    g1 = 1.0 / (1.0 + e)
    gates = jnp.where(lane == i1, g1, 0.0) + jnp.where(lane == i2, e * g1, 0.0)
    chosen = lambda off: jnp.where((lane == i1 + off) | (lane == i2 + off), 1.0, 0.0)
    rank = _dot(tri_ref[...], chosen(float(RANK_LANE)).astype(BF16))
    rec = gates + chosen(float(FLAG_LANE)) + rank
    rec_ref[...] = rec
    rect_ref[0] = lax.dot_general(eye_ref[...], rec, (((1,), (1,)), ((), ())),
                                  preferred_element_type=F32, precision=HIGHEST)


def _router(x, g, router_pad, tw):
    rows = x.shape[0]
    nw = rows // tw
    tri = (jnp.arange(tw)[:, None] > jnp.arange(tw)[None, :]).astype(BF16)
    eye = jnp.eye(LANES, dtype=F32)
    return pl.pallas_call(
        _router_kernel,
        grid=(nw,),
        in_specs=[pl.BlockSpec((tw, D_MODEL), lambda i: (i, 0)), _full((1, D_MODEL)),
                  _full((D_MODEL, LANES)), _full((tw, tw)), _full((LANES, LANES))],
        out_specs=[pl.BlockSpec((tw, D_MODEL), lambda i: (i, 0)),
                   pl.BlockSpec((tw, LANES), lambda i: (i, 0)),
                   pl.BlockSpec((1, LANES, tw), lambda i: (i, 0, 0))],
        out_shape=[jax.ShapeDtypeStruct((rows, D_MODEL), BF16),
                   jax.ShapeDtypeStruct((rows, LANES), F32),
                   jax.ShapeDtypeStruct((nw, LANES, tw), F32)],
        compiler_params=_params("parallel"),
        name="router",
    )(x, g, router_pad, tri, eye)


def _moe_kernel(h_ref, rec_ref, rect_ref, acc_ref, wg_ref, wu_ref, wd_ref, o_ref, *, e, cap, small):
    tw = h_ref.shape[0]
    rec = rec_ref[...]
    rect = rect_ref[0]
    lane = lax.broadcasted_iota(jnp.int32, rec.shape, 1)
    sub = lax.broadcasted_iota(jnp.int32, rect.shape, 0)
    col = lambda off: jnp.sum(jnp.where(lane == off + e, rec, 0.0), axis=-1, keepdims=True)
    row = lambda off: jnp.sum(jnp.where(sub == off + e, rect, 0.0), axis=0, keepdims=True)
    gate_c, flag_c, rank_c = col(GATE_LANE), col(FLAG_LANE), col(RANK_LANE)
    flag_r, rank_r = row(FLAG_LANE), row(RANK_LANE)
    o_ref[...] = acc_ref[...]
    count = jnp.max(rank_c + flag_c)

    def run_pass(base, size):
        slot_r = lax.broadcasted_iota(jnp.int32, (size, tw), 0).astype(F32)
        slot_c = lax.broadcasted_iota(jnp.int32, (tw, size), 1).astype(F32)
        sel = jnp.where((rank_r - base == slot_r) & (flag_r > 0.0), 1.0, 0.0).astype(BF16)
        sel_t = jnp.where((rank_c - base == slot_c) & (flag_c > 0.0), 1.0, 0.0).astype(BF16)
        xs = _dot(sel, h_ref[...]).astype(BF16)
        a = _dot(xs, wg_ref[0])
        u = _dot(xs, wu_ref[0])
        y = _dot((a * _sigmoid(a) * u).astype(BF16), wd_ref[0])
        o_ref[...] += gate_c * _dot(sel_t, y.astype(BF16))

    n_full = sum((count >= float((k + 1) * cap)).astype(jnp.int32) for k in range(tw // cap))

    def full_pass(p, carry):
        run_pass((p * cap).astype(F32), cap)
        return carry

    lax.fori_loop(0, n_full, full_pass, 0)
    done = (n_full * cap).astype(F32)
    rem = count - done

    @pl.when(rem > float(small))
    def _():
        run_pass(done, cap)

    @pl.when((rem > 0.0) & (rem <= float(small)))
    def _():
        run_pass(done, small)


def _moe_window(rows):
    best = None
    for d in range(16, min(rows, MOE_MAX_WINDOW) + 1, 16):
        if rows % d == 0:
            best = d
    assert best is not None, rows
    return best


def _moe(h, rec, rect, x, wg, wu, wd, tw):
    rows = x.shape[0]
    nw = rows // tw
    cap = -(-(tw * TOP_K * MOE_CAP_SLACK_PCT) // (N_EXPERTS * 100 * 16)) * 16
    small = -(-(2 * cap) // (3 * 16)) * 16
    win = lambda c: pl.BlockSpec((tw, c), lambda w: (w, 0))
    acc = x
    for e in range(wg.shape[0]):
        weight = lambda a, e=e: pl.BlockSpec((1,) + a.shape[1:], lambda w: (e, 0, 0),
                                             pipeline_mode=pl.Buffered(1))
        acc = pl.pallas_call(
            functools.partial(_moe_kernel, e=e, cap=cap, small=small),
            grid=(nw,),
            in_specs=[win(D_MODEL), win(LANES), pl.BlockSpec((1, LANES, tw), lambda w: (w, 0, 0)),
                      win(D_MODEL), weight(wg), weight(wu), weight(wd)],
            out_specs=win(D_MODEL),
            out_shape=jax.ShapeDtypeStruct((rows, D_MODEL), F32),
            compiler_params=_params("parallel"),
            name="moe",
        )(h, rec, rect, acc, wg, wu, wd)
    return acc


def _ffn_kernel(x_ref, g_ref, gate_ref, wg_ref, wu_ref, wd_ref, o_ref, h_scr, acc_scr):
    e = pl.program_id(1)
    f = pl.program_id(2)

    @pl.when((e == 0) & (f == 0))
    def _():
        x = x_ref[...]
        h = x * lax.rsqrt(jnp.mean(x * x, axis=-1, keepdims=True) + EPS) * g_ref[...]
        h_scr[...] = h.astype(BF16)
        acc_scr[...] = jnp.zeros(acc_scr.shape, F32)

    h = h_scr[...]
    a = _dot(h, wg_ref[0])
    u = _dot(h, wu_ref[0])
    act = (a * _sigmoid(a) * u).astype(BF16)
    lane = lax.broadcasted_iota(jnp.int32, gate_ref.shape, 1)
    gcol = jnp.sum(jnp.where(lane == e, gate_ref[...], 0.0), axis=-1, keepdims=True)
    acc_scr[...] += gcol * _dot(act, wd_ref[0])

    @pl.when((e == pl.num_programs(1) - 1) & (f == pl.num_programs(2) - 1))
    def _():
        o_ref[...] = x_ref[...] + acc_scr[...]


def _ffn(x, g, gates, wg, wu, wd):
    rows = x.shape[0]
    n_e = wg.shape[0]
    tm = _pick_tile(rows, 704, 16)
    tf = D_FF // 2
    return pl.pallas_call(
        _ffn_kernel,
        grid=(rows // tm, n_e, D_FF // tf),
        in_specs=[pl.BlockSpec((tm, D_MODEL), lambda i, e, f: (i, 0)),
                  pl.BlockSpec((1, D_MODEL), lambda i, e, f: (0, 0)),
                  pl.BlockSpec((tm, LANES), lambda i, e, f: (i, 0)),
                  pl.BlockSpec((1, D_MODEL, tf), lambda i, e, f: (e, 0, f)),
                  pl.BlockSpec((1, D_MODEL, tf), lambda i, e, f: (e, 0, f)),
                  pl.BlockSpec((1, tf, D_MODEL), lambda i, e, f: (e, f, 0))],
        out_specs=pl.BlockSpec((tm, D_MODEL), lambda i, e, f: (i, 0)),
        out_shape=jax.ShapeDtypeStruct((rows, D_MODEL), F32),
        scratch_shapes=[pltpu.VMEM((tm, D_MODEL), BF16), pltpu.VMEM((tm, D_MODEL), F32)],
        compiler_params=_params("parallel", "arbitrary", "arbitrary"),
        name="ffn",
    )(x, g, gates, wg, wu, wd)


def _wkv_to_pairs(s):
    n = s.shape[0]
    assert n % SCAN_SEQS == 0
    s = s.reshape(n // SCAN_SEQS, SCAN_SEQS, 2, 2, A_HEAD_DIM, A_HEAD_DIM)
    return s.transpose(0, 4, 2, 1, 3, 5).reshape(n // SCAN_SEQS, A_HEAD_DIM, 2 * SCAN_SEQS, LANES)


def _wkv_from_pairs(s):
    nblk = s.shape[0]
    s = s.reshape(nblk, A_HEAD_DIM, 2, SCAN_SEQS, 2, A_HEAD_DIM)
    return s.transpose(0, 3, 2, 4, 1, 5).reshape(nblk * SCAN_SEQS, A_HEADS, A_HEAD_DIM, A_HEAD_DIM)


def _mixers(x, n, t, shift0, wkv0, conv0, attend, lp, c, lam_init):
    z, qn, kn, kb, v, vb = _in_proj(x, lp['norm1_g'].reshape(1, -1), lp['w_in_bf'], lp['qg'], lp['kg'],
                                    c['bd512'])
    sc_t, gb = _rwkv_pre(z, shift0, lp, c['bd256'], n, t)
    y_t, s1 = _rwkv_scan(sc_t, _wkv_to_pairs(wkv0), c['bd128'], c['eye'])
    yb, conv1 = _conformer_conv(z, conv0, lp, lp['pw_bf'], n, t)
    yc = attend(qn, kn, kb, vb, v)
    x = _out_proj(x, y_t, gb, yb, yc, lp, c['bd256'], lp['w_out_bf'])
    k_rows = kn.reshape(n, t, C_HEADS, 2 * C_QK_DIM)
    v_rows = v.reshape(n, t, C_HEADS, C_V_DIM)
    shift1 = z.reshape(n, t, AB_COLS)[:, -1, :A_COLS]
    return x, k_rows, v_rows, _wkv_from_pairs(s1), shift1, conv1


def kernel(x_prompt, x_sample, cache_k, cache_v, page_table, state_wkv, state_shift, state_conv, meta_tokens, norm1_g, norm2_g, w_in, w_out, a_mu, a_w0, a_w_up, a_a0, a_a_up, a_g_up, a_k_k, a_k_a, a_r_k, a_lnx_g, a_lnx_b, b_conv_w, b_conv_b, b_ln_g, b_ln_b, b_pw_w, b_pw_b, c_qn_g, c_kn_g, c_lam_q1, c_lam_k1, c_lam_q2, c_lam_k2, c_subln_g, ffn_w_gate, ffn_w_up, ffn_w_down, moe_router, moe_w_gate, moe_w_up, moe_w_down):
    depth = w_in.shape[0]
    b, seq, _ = x_prompt.shape
    n_s, t_s, _ = x_sample.shape
    t_p = seq + N_META
    n_pool = cache_k.shape[1]
    pcols = PAGE_SIZE * C_HEADS

    consts = {
        'bd512': _block_diag2(C_QK_COLS),
        'bd256': _block_diag2(A_WIDTH),
        'bd128': _block_diag2(LANES),
        'eye': jnp.broadcast_to(
            (jnp.arange(LANES)[None, None, :] % A_HEAD_DIM == jnp.arange(A_HEAD_DIM)[:, None, None]).astype(F32),
            (A_HEAD_DIM, 2 * SCAN_SEQS, LANES)),
    }
    meta = jnp.broadcast_to(meta_tokens.astype(F32)[None], (b, N_META, D_MODEL))
    xp = jnp.concatenate([meta, x_prompt], axis=1).reshape(b * t_p, D_MODEL)
    xs = x_sample.reshape(n_s * t_s, D_MODEL)
    ck_all = cache_k.reshape(depth * n_pool, pcols, LANES)
    cv_all = cache_v.reshape(depth * n_pool, pcols, LANES)
    sample_tables = _sample_bias_tables(page_table.shape[1], t_s)

    outs = [[] for _ in range(10)]
    for l in range(depth):
        lam_init = 0.8 - 0.6 * math.exp(-0.3 * l)
        lp = {
            'norm1_g': norm1_g[l], 'w_in_bf': w_in[l].astype(BF16), 'w_out_bf': w_out[l].astype(BF16),
            'a_mu': a_mu[l], 'a_w0': a_w0[l], 'a_w_up': a_w_up[l], 'a_a0': a_a0[l], 'a_a_up': a_a_up[l],
            'a_g_up': a_g_up[l], 'a_k_k': a_k_k[l], 'a_k_a': a_k_a[l], 'a_r_k': a_r_k[l],
            'a_lnx_g': a_lnx_g[l], 'a_lnx_b': a_lnx_b[l],
            'b_conv_w': b_conv_w[l], 'b_conv_b': b_conv_b[l], 'b_ln_g': b_ln_g[l], 'b_ln_b': b_ln_b[l],
            'pw_bf': b_pw_w[l].astype(BF16), 'b_pw_b': b_pw_b[l],
            'qg': jnp.tile(c_qn_g[l], 2 * C_HEADS).reshape(1, -1),
            'kg': jnp.tile(c_kn_g[l], 2 * C_HEADS).reshape(1, -1),
        }
        lamv = jnp.stack([c_lam_q1[l], c_lam_k1[l], c_lam_q2[l], c_lam_k2[l]]).astype(F32)
        sg = c_subln_g[l].reshape(1, -1)

        def attend_prompt(qn, kn, kb, vb, v, lamv=lamv, sg=sg, lam_init=lam_init):
            return _attn_prompt(qn, kb, vb, lamv, sg, b, t_p, lam_init)

        def attend_sample(qn, kn, kb, vb, v, l=l, lamv=lamv, sg=sg, lam_init=lam_init):
            rows = t_s * C_HEADS
            qn3 = qn.reshape(n_s, rows, LANES)
            kn3 = kn.reshape(n_s, rows, LANES)
            vn3 = v.reshape(n_s, rows, LANES)
            o = _attn_sample(qn3, kn3, vn3, ck_all, cv_all, page_table, l * n_pool, lamv, sg, sample_tables,
                             lam_init)
            return o.reshape(n_s * t_s, C_WIDTH)

        xp, k_p, v_p, wkv_p, shift_p, conv_p = _mixers(
            xp, b, t_p, jnp.zeros((b, A_COLS), F32), jnp.zeros((b, A_HEADS, A_HEAD_DIM, A_HEAD_DIM), F32),
            jnp.zeros((b, CONV_HIST, B_WIDTH), F32), attend_prompt, lp, consts, lam_init)
        xs, k_s, v_s, wkv_s, shift_s, conv_s = _mixers(
            xs, n_s, t_s, state_shift[l], state_wkv[l], state_conv[l], attend_sample, lp, consts, lam_init)

        i = l // 2
        g2 = norm2_g[l].reshape(1, -1)
        if l % 2 == 0:
            wg = ffn_w_gate[i].astype(BF16)[None]
            wu = ffn_w_up[i].astype(BF16)[None]
            wd = ffn_w_down[i].astype(BF16)[None]
            xp = _ffn(xp, g2, jnp.ones((xp.shape[0], LANES), F32), wg, wu, wd)
            xs = _ffn(xs, g2, jnp.ones((xs.shape[0], LANES), F32), wg, wu, wd)
        else:
            wg = moe_w_gate[i].astype(BF16)
            wu = moe_w_up[i].astype(BF16)
            wd = moe_w_down[i].astype(BF16)
            router_pad = jnp.pad(moe_router[i], ((0, 0), (0, LANES - N_EXPERTS)))
            tw_p, tw_s = _moe_window(xp.shape[0]), _moe_window(xs.shape[0])
            xp = _moe(*_router(xp, g2, router_pad, tw_p), xp, wg, wu, wd, tw_p)
            xs = _moe(*_router(xs, g2, router_pad, tw_s), xs, wg, wu, wd, tw_s)

        for lst, val in zip(outs, (k_p, v_p, k_s, v_s, wkv_p, wkv_s, shift_p, shift_s, conv_p, conv_s)):
            lst.append(val)

    y_prompt = xp.reshape(b, t_p, D_MODEL)[:, N_META:]
    y_sample = xs.reshape(n_s, t_s, D_MODEL)
    return (y_prompt, y_sample) + tuple(jnp.stack(o) for o in outs)
```

```python
import functools
import math

import jax
import jax.numpy as jnp
from jax import lax
from jax.experimental import pallas as pl
from jax.experimental.pallas import tpu as pltpu

F32 = jnp.float32
BF16 = jnp.bfloat16
HIGHEST = lax.Precision.HIGHEST

D_MODEL = 1024
EPS = 1e-6
N_META = 16
A_HEAD_DIM = 64
A_WIDTH = 256
A_HEADS = 4
A_W_LORA = 64
A_A_LORA = 64
A_G_LORA = 128
A_COLS = 1024
A_GN_EPS = 64e-5
B_WIDTH = 256
B_COLS = 512
CONV_WIDTH = 31
CONV_HIST = CONV_WIDTH - 1
C_QK_DIM = 64
C_V_DIM = 128
C_WIDTH = 512
C_HEADS = 4
C_QK_COLS = 512
IN_COLS = 3072
Q_OFF = A_COLS + B_COLS
K_OFF = Q_OFF + C_QK_COLS
V_OFF = K_OFF + C_QK_COLS
AB_COLS = A_COLS + B_COLS
D_FF = 2816
N_EXPERTS = 8
PAGE_SIZE = 128
LANES = 128
NEG_BIG = -1e30
ALIBI_SLOPES = tuple(float((2.0 ** (-8.0 / C_HEADS)) ** (h + 1)) for h in range(C_HEADS))
EXP_NEG_HALF = math.exp(-0.5)
SCAN_SEQS = 8
SCAN_COLS = 7 * A_WIDTH
SCAN_SUB = 8
TOP_K = 2
MIN_TIME_MAJOR_T = 64
MOE_MAX_WINDOW = 704
MOE_CAP_SLACK_PCT = 111
VMEM_LIMIT = 56 * 1024 * 1024


def _params(*sem):
    return pltpu.CompilerParams(dimension_semantics=sem, vmem_limit_bytes=VMEM_LIMIT)


def _pick_tile(n, target, mult=8):
    best = None
    for d in range(mult, min(n, target) + 1, mult):
        if n % d == 0:
            best = d
    return n if best is None else best


def _dot(a, b):
    return jnp.dot(a, b, preferred_element_type=F32)


def _dot_hi(a, b):
    return jnp.dot(a, b, preferred_element_type=F32, precision=HIGHEST)


def _dot_nt(a, b):
    return lax.dot_general(a, b, (((1,), (1,)), ((), ())), preferred_element_type=F32)


def _sigmoid(x):
    return 1.0 / (1.0 + jnp.exp(-x))


def _seg_sum(x, bd2):
    hi = x.astype(BF16)
    lo = (x - hi.astype(F32)).astype(BF16)
    return _dot(jnp.concatenate([hi, lo], axis=-1), bd2)


def _seg_sum1(x, bd2):
    return _dot(x.astype(BF16), bd2[0:x.shape[-1]])


def _block_diag2(width, seg=64):
    r = jnp.arange(width) // seg
    bd = (r[:, None] == r[None, :]).astype(BF16)
    return jnp.concatenate([bd, bd], axis=0)


def _full(shape):
    nd = len(shape)
    return pl.BlockSpec(shape, lambda *_: (0,) * nd)


def _in_proj_kernel(x_ref, g_ref, w_ref, qg_ref, kg_ref, bd_ref, z_ref, qn_ref, kn_ref, kb_ref, v_ref, vb_ref):
    x = x_ref[...]
    h = x * lax.rsqrt(jnp.mean(x * x, axis=-1, keepdims=True) + EPS) * g_ref[...]
    z = _dot(h.astype(BF16), w_ref[...])
    z_ref[...] = z[:, :AB_COLS]
    v_ref[...] = z[:, V_OFF:]
    q = z[:, Q_OFF:Q_OFF + C_QK_COLS]
    k = z[:, K_OFF:K_OFF + C_QK_COLS]
    bd = bd_ref[...]
    inv = 1.0 / C_QK_DIM
    qn = q * lax.rsqrt(_seg_sum1(q * q, bd) * inv + EPS) * qg_ref[...]
    kn = k * lax.rsqrt(_seg_sum1(k * k, bd) * inv + EPS) * kg_ref[...]
    qn_ref[...] = qn * (C_QK_DIM ** -0.5)
    kn_ref[...] = kn
    kb_ref[...] = kn.astype(BF16)
    vb_ref[...] = z[:, V_OFF:].astype(BF16)


def _in_proj(x, g, w_bf, qg, kg, bd512):
    rows = x.shape[0]
    tm = _pick_tile(rows, 512, 16)
    row = lambda c: pl.BlockSpec((tm, c), lambda i: (i, 0))
    return pl.pallas_call(
        _in_proj_kernel,
        grid=(rows // tm,),
        in_specs=[row(D_MODEL), _full((1, D_MODEL)), _full((D_MODEL, IN_COLS)),
                  _full((1, C_QK_COLS)), _full((1, C_QK_COLS)), _full((2 * C_QK_COLS, C_QK_COLS))],
        out_specs=[row(AB_COLS), row(C_QK_COLS), row(C_QK_COLS), row(C_QK_COLS), row(C_WIDTH), row(C_WIDTH)],
        out_shape=[jax.ShapeDtypeStruct((rows, AB_COLS), F32),
                   jax.ShapeDtypeStruct((rows, C_QK_COLS), F32),
                   jax.ShapeDtypeStruct((rows, C_QK_COLS), F32),
                   jax.ShapeDtypeStruct((rows, C_QK_COLS), BF16),
                   jax.ShapeDtypeStruct((rows, C_WIDTH), F32),
                   jax.ShapeDtypeStruct((rows, C_WIDTH), BF16)],
        compiler_params=_params("parallel"),
        name="in_proj",
    )(x, g, w_bf, qg, kg, bd512)


def _rwkv_pre_kernel(za_ref, bnd_ref, mu_ref, w0_ref, wup_ref, a0_ref, aup_ref, gup_ref,
                     kk_ref, ka_ref, rk_ref, bd_ref, sc_ref, gb_ref, *, period):
    za = za_ref[...]
    rid = lax.broadcasted_iota(jnp.int32, za.shape, 0)
    at_boundary = (rid == 0) if period == za.shape[0] else (jnp.bitwise_and(rid, period - 1) == 0)
    prev = jnp.where(at_boundary, bnd_ref[...].reshape(-1, A_COLS), pltpu.roll(za, 1, 0))
    zs = za + mu_ref[...] * (prev - za)
    w = A_WIDTH
    r = zs[:, 0:w]
    k = zs[:, w:2 * w]
    v = zs[:, 2 * w:3 * w]
    wd = zs[:, 3 * w:3 * w + A_W_LORA]
    ad = zs[:, 3 * w + A_W_LORA:3 * w + A_W_LORA + A_A_LORA]
    gd = zs[:, 3 * w + A_W_LORA + A_A_LORA:]
    lw = w0_ref[...] + _dot_hi(jnp.tanh(wd), wup_ref[...])
    decay = jnp.exp(-EXP_NEG_HALF * _sigmoid(lw))
    a = _sigmoid(a0_ref[...] + _dot_hi(ad, aup_ref[...]))
    g = _dot(_sigmoid(gd).astype(BF16), gup_ref[...].astype(BF16))
    bd = bd_ref[...]
    kk = k * kk_ref[...]
    k2 = k * (1.0 + (a - 1.0) * ka_ref[...])
    kk = kk * lax.rsqrt(jnp.maximum(_seg_sum(kk * kk, bd), 1e-24))
    bonus = _seg_sum(r * k2 * rk_ref[...], bd) * v
    b = kk * a
    sc_ref[:, 0:w] = decay
    sc_ref[:, w:2 * w] = kk
    sc_ref[:, 2 * w:3 * w] = b
    sc_ref[:, 3 * w:4 * w] = k2
    sc_ref[:, 4 * w:5 * w] = v
    sc_ref[:, 5 * w:6 * w] = decay * r - kk * _seg_sum(b * r, bd)
    sc_ref[:, 6 * w:7 * w] = _seg_sum(k2 * r, bd)
    gb_ref[:, 0:w] = g
    gb_ref[:, w:2 * w] = bonus


def _time_major_spec(tm, cols, tiles_per_seq):
    return pl.BlockSpec((tm, cols), lambda i: (i % tiles_per_seq, i // tiles_per_seq))


def _seq_tile(t):
    return _pick_tile(t, 512, 8) if t >= MIN_TIME_MAJOR_T else None


def _rwkv_pre(z, shift0, lp, bd256, n, t):
    rows = z.shape[0]
    tm = _seq_tile(t) or _pick_tile(rows, 512, 8)
    row = lambda c: pl.BlockSpec((tm, c), lambda i: (i, 0))
    vec = lambda a: a.reshape(1, -1)
    if _seq_tile(t):
        tiles = t // tm
        last = z.reshape(n, t, AB_COLS)[:, tm - 1::tm, :A_COLS][:, :tiles - 1]
        bnd = jnp.concatenate([shift0[:, None, :], last], axis=1).reshape(n * tiles, 1, A_COLS)
        bnd_spec, period = pl.BlockSpec((1, 1, A_COLS), lambda i: (i, 0, 0)), tm
        sc_spec, sc_shape = _time_major_spec(tm, SCAN_COLS, tiles), (t, n * SCAN_COLS)
    else:
        assert t & (t - 1) == 0 and tm % t == 0
        bnd = jnp.repeat(shift0, t, axis=0)
        bnd_spec, period = row(A_COLS), t
        sc_spec, sc_shape = row(SCAN_COLS), (rows, SCAN_COLS)
    args = (z, bnd, vec(lp['a_mu']), vec(lp['a_w0']), lp['a_w_up'], vec(lp['a_a0']), lp['a_a_up'],
            lp['a_g_up'], vec(lp['a_k_k']), vec(lp['a_k_a']), vec(lp['a_r_k']), bd256)
    in_specs = [row(A_COLS), bnd_spec] + [_full(a.shape) for a in args[2:]]
    sc, gb = pl.pallas_call(
        functools.partial(_rwkv_pre_kernel, period=period),
        grid=(rows // tm,),
        in_specs=in_specs,
        out_specs=[sc_spec, row(2 * A_WIDTH)],
        out_shape=[jax.ShapeDtypeStruct(sc_shape, F32),
                   jax.ShapeDtypeStruct((rows, 2 * A_WIDTH), F32)],
        compiler_params=_params("parallel"),
        name="rwkv_pre",
    )(*args)
    if _seq_tile(t):
        return sc.reshape(t, n, SCAN_COLS), gb
    return sc.reshape(n, t, SCAN_COLS).transpose(1, 0, 2), gb


def _rwkv_scan_kernel(x_ref, s0_ref, bd_ref, eye_ref, eyeb_ref, y_ref, s1_ref, s_scr, vb_scr, *, tb, sub):
    nseq = x_ref.shape[1]
    npair = 2 * nseq
    w = A_WIDTH
    rows = A_HEAD_DIM * npair

    @pl.when(pl.program_id(1) == 0)
    def _():
        s_scr[...] = s0_ref[0]

    bd2 = bd_ref[...]
    bd1 = bd2[0:LANES]
    eye_bf = eyeb_ref[...]

    def rowvec(xt, c):
        return jnp.concatenate([xt[:, c:c + LANES], xt[:, c + LANES:c + 2 * LANES]], axis=0)

    def value_bcast(t_src, slot, row):
        v = rowvec(x_ref[t_src], 4 * w).astype(BF16)
        vb = _dot((eye_bf * v[None]).reshape(rows, LANES), bd1)
        vb_scr[slot, row] = vb.reshape(A_HEAD_DIM, npair, LANES)

    def seg1(a):
        nv = a.shape[0]
        return _dot(a.reshape(nv * npair, LANES).astype(BF16), bd1).reshape(nv, npair, LANES)

    for tt in range(sub):
        value_bcast(tt, 0, tt)

    def step(t, carry):
        slot = (t // sub) % 2
        row = t % sub
        value_bcast(jnp.minimum(t + sub, tb - 1), 1 - slot, row)
        xt = x_ref[t]
        dec, kk, b, k, v, rq, kr = (rowvec(xt, i * w) for i in range(7))
        s = s_scr[...]
        sa = seg1(s * kk[None])
        yq = seg1(s * rq[None])
        s_scr[...] = s * dec[None] - sa * b[None] + vb_scr[slot, row] * k[None]
        y = jnp.sum(yq * eye_ref[...], axis=0) + v * kr
        y_ref[t] = jnp.concatenate([y[0:nseq], y[nseq:npair]], axis=-1)
        return carry

    lax.fori_loop(0, tb, step, 0, unroll=2)

    @pl.when(pl.program_id(1) == pl.num_programs(1) - 1)
    def _():
        s1_ref[0] = s_scr[...]


def _rwkv_scan(sc_t, s0, bd128, eye):
    t, n, _ = sc_t.shape
    nb = s0.shape[2] // 2
    tb = _pick_tile(t, 64, SCAN_SUB)
    assert tb % SCAN_SUB == 0
    state_spec = pl.BlockSpec((1, A_HEAD_DIM, 2 * nb, LANES), lambda i, j: (i, 0, 0, 0))
    return pl.pallas_call(
        functools.partial(_rwkv_scan_kernel, tb=tb, sub=SCAN_SUB),
        grid=(n // nb, t // tb),
        in_specs=[pl.BlockSpec((tb, nb, SCAN_COLS), lambda i, j: (j, i, 0)),
                  state_spec,
                  _full((2 * LANES, LANES)), _full((A_HEAD_DIM, 2 * nb, LANES)),
                  _full((A_HEAD_DIM, 2 * nb, LANES))],
        out_specs=[pl.BlockSpec((tb, nb, A_WIDTH), lambda i, j: (j, i, 0)), state_spec],
        out_shape=[jax.ShapeDtypeStruct((t, n, A_WIDTH), F32),
                   jax.ShapeDtypeStruct(s0.shape, F32)],
        scratch_shapes=[pltpu.VMEM((A_HEAD_DIM, 2 * nb, LANES), F32),
                        pltpu.VMEM((2, SCAN_SUB, A_HEAD_DIM, 2 * nb, LANES), F32)],
        compiler_params=_params("parallel", "arbitrary"),
        name="rwkv_scan",
    )(sc_t, s0, bd128, eye, eye.astype(BF16))


_CONV_PAD = 32


def _conv_kernel(zb_ref, c0_ref, cw_ref, cb_ref, lg_ref, lb_ref, pw_ref, pb_ref,
                 y_ref, c1_ref, ext_scr, h_scr, *, nb, t, tc):
    cw = cw_ref[...]
    for s in range(nb):
        zb = zb_ref[s * t:(s + 1) * t, :]
        u = zb[:, :B_WIDTH] * _sigmoid(zb[:, B_WIDTH:])
        ext_scr[s, _CONV_PAD - CONV_HIST:_CONV_PAD, :] = c0_ref[s]
        ext_scr[s, _CONV_PAD:_CONV_PAD + t, :] = u
        c1_ref[s] = ext_scr[s, t + _CONV_PAD - CONV_HIST:t + _CONV_PAD, :]

        def chunk(c, carry, s=s):
            base = pl.multiple_of(c * tc, 8)
            win = ext_scr[s, pl.ds(base, tc + _CONV_PAD), :]
            acc = jnp.zeros((tc, B_WIDTH), F32)
            first = _CONV_PAD - CONV_HIST
            for b in range(8):
                taps = [j for j in range(CONV_WIDTH) if (first + j) % 8 == b]
                if not taps:
                    continue
                span = max(first + j - b for j in taps) + tc
                wb = win[b:b + span, :]
                for j in taps:
                    o = first + j - b
                    acc = acc + wb[o:o + tc, :] * cw[j:j + 1, :]
            acc = acc + cb_ref[...]
            xc = acc - jnp.mean(acc, axis=-1, keepdims=True)
            hn = xc * lax.rsqrt(jnp.mean(xc * xc, axis=-1, keepdims=True) + EPS)
            hn = hn * lg_ref[...] + lb_ref[...]
            hn = hn * _sigmoid(hn)
            h_scr[pl.ds(pl.multiple_of(s * t + c * tc, 8), tc), :] = hn
            return carry

        lax.fori_loop(0, t // tc, chunk, 0)
    y_ref[...] = _dot(h_scr[...].astype(BF16), pw_ref[...]) + pb_ref[...]


def _conformer_conv(z, conv0, lp, pw_bf, n, t):
    nb = 1 if t >= 64 else _pick_tile(n, 16, 1)
    tc = _pick_tile(t, 48, 8)
    rows = nb * t
    vec = lambda a: a.reshape(1, -1)
    return pl.pallas_call(
        functools.partial(_conv_kernel, nb=nb, t=t, tc=tc),
        grid=(n // nb,),
        in_specs=[pl.BlockSpec((rows, B_COLS), lambda i: (i, A_COLS // B_COLS)),
                  pl.BlockSpec((nb, CONV_HIST, B_WIDTH), lambda i: (i, 0, 0)),
                  _full((CONV_WIDTH, B_WIDTH)), _full((1, B_WIDTH)), _full((1, B_WIDTH)),
                  _full((1, B_WIDTH)), _full((B_WIDTH, B_WIDTH)), _full((1, B_WIDTH))],
        out_specs=[pl.BlockSpec((rows, B_WIDTH), lambda i: (i, 0)),
                   pl.BlockSpec((nb, CONV_HIST, B_WIDTH), lambda i: (i, 0, 0))],
        out_shape=[jax.ShapeDtypeStruct((n * t, B_WIDTH), F32),
                   jax.ShapeDtypeStruct((n, CONV_HIST, B_WIDTH), F32)],
        scratch_shapes=[pltpu.VMEM((nb, t + _CONV_PAD, B_WIDTH), F32),
                        pltpu.VMEM((rows, B_WIDTH), F32)],
        compiler_params=_params("parallel"),
        name="conformer_conv",
    )(z, conv0, lp['b_conv_w'], vec(lp['b_conv_b']), vec(lp['b_ln_g']), vec(lp['b_ln_b']),
      pw_bf, vec(lp['b_pw_b']))


def _lambda_from(lamv_ref, lam_init):
    lv = lamv_ref[...]
    s1 = jnp.sum(lv[0:1] * lv[1:2], axis=-1, keepdims=True)
    s2 = jnp.sum(lv[2:3] * lv[3:4], axis=-1, keepdims=True)
    return jnp.exp(s1) - jnp.exp(s2) + lam_init


def _sub_ln(o, sg, lam_init):
    o = o * lax.rsqrt(jnp.mean(o * o, axis=-1, keepdims=True) + EPS) * sg
    return o * (1.0 - lam_init)


def _map_masks(rows):
    lane = lax.broadcasted_iota(jnp.int32, (rows, LANES), 1)
    return lane < C_QK_DIM, lane >= C_QK_DIM


def _attn_prompt_kernel(q_ref, k_ref, v_ref, lamv_ref, sg_ref, o_ref, m_scr, l_scr, acc_scr,
                        *, tq, lam_init):
    qi = pl.program_id(1)
    kj = pl.program_id(2)

    @pl.when(kj == 0)
    def _():
        m_scr[...] = jnp.full(m_scr.shape, NEG_BIG, F32)
        l_scr[...] = jnp.zeros(l_scr.shape, F32)
        acc_scr[...] = jnp.zeros(acc_scr.shape, F32)

    @pl.when(kj <= qi)
    def _():
        qpos = qi * tq + lax.broadcasted_iota(jnp.int32, (tq, tq), 0)
        kpos = kj * tq + lax.broadcasted_iota(jnp.int32, (tq, tq), 1)
        dist = (qpos - kpos).astype(F32)
        causal = kpos <= qpos
        masks = _map_masks(tq)
        for h in range(C_HEADS):
            qh = q_ref[:, h * LANES:(h + 1) * LANES]
            kh = k_ref[:, h * LANES:(h + 1) * LANES]
            vh = v_ref[:, h * LANES:(h + 1) * LANES]
            for m in range(2):
                i = 2 * h + m
                qm = jnp.where(masks[m], qh, 0.0).astype(BF16)
                s = _dot_nt(qm, kh) - ALIBI_SLOPES[h] * dist
                s = jnp.where(causal, s, NEG_BIG)
                m_prev = m_scr[i]
                m_new = jnp.maximum(m_prev, jnp.max(s, axis=-1, keepdims=True))
                alpha = jnp.exp(m_prev - m_new)
                p = jnp.exp(s - m_new)
                l_scr[i] = alpha * l_scr[i] + jnp.sum(p, axis=-1, keepdims=True)
                acc_scr[i] = alpha * acc_scr[i] + _dot(p.astype(BF16), vh)
                m_scr[i] = m_new

    @pl.when(kj == qi)
    def _():
        lam = _lambda_from(lamv_ref, lam_init)
        for h in range(C_HEADS):
            o = acc_scr[2 * h] / l_scr[2 * h] - lam * (acc_scr[2 * h + 1] / l_scr[2 * h + 1])
            o_ref[:, h * LANES:(h + 1) * LANES] = _sub_ln(o, sg_ref[...], lam_init)


def _attn_prompt(qn, kb, vb, lamv, sg, n, t, lam_init):
    tq = _pick_tile(t, 704, 16)
    nq = t // tq
    return pl.pallas_call(
        functools.partial(_attn_prompt_kernel, tq=tq, lam_init=lam_init),
        grid=(n, nq, nq),
        in_specs=[pl.BlockSpec((tq, C_QK_COLS), lambda b, i, j: (b * nq + i, 0)),
                  pl.BlockSpec((tq, C_QK_COLS), lambda b, i, j: (b * nq + jnp.minimum(i, j), 0)),
                  pl.BlockSpec((tq, C_WIDTH), lambda b, i, j: (b * nq + jnp.minimum(i, j), 0)),
                  _full((4, C_QK_DIM)), _full((1, C_V_DIM))],
        out_specs=pl.BlockSpec((tq, C_WIDTH), lambda b, i, j: (b * nq + i, 0)),
        out_shape=jax.ShapeDtypeStruct((n * t, C_WIDTH), F32),
        scratch_shapes=[pltpu.VMEM((2 * C_HEADS, tq, 1), F32),
                        pltpu.VMEM((2 * C_HEADS, tq, 1), F32),
                        pltpu.VMEM((2 * C_HEADS, tq, C_V_DIM), F32)],
        compiler_params=_params("parallel", "parallel", "arbitrary"),
        name="attn_prompt",
    )(qn, kb, vb, lamv, sg)


def _attn_sample_kernel(pt_ref, q_ref, kn_ref, vn_ref, lamv_ref, sg_ref, tb_ref, cj_ref, tn_ref, *rest,
                        n_pages, t_s, lam_init):
    k_refs = rest[:n_pages]
    v_refs = rest[n_pages:2 * n_pages]
    o_ref = rest[2 * n_pages]
    del pt_ref
    rows = t_s * C_HEADS
    q = q_ref[0]
    masks = _map_masks(rows)
    qcat = jnp.concatenate([jnp.where(masks[0], q, 0.0), jnp.where(masks[1], q, 0.0),
                            jnp.zeros((LANES - 2 * rows, LANES), F32)], axis=0).astype(BF16)

    m_run = jnp.full((1, LANES), NEG_BIG, F32)
    l_run = jnp.zeros((1, LANES), F32)
    acc = jnp.zeros((C_V_DIM, LANES), F32)

    def update(kblk, vblk, bias, col_off, m_run, l_run, acc):
        s = _dot_nt(kblk.astype(BF16), qcat) + bias
        m_new = jnp.maximum(m_run, jnp.max(s, axis=0, keepdims=True) + col_off)
        alpha = jnp.exp(m_run - m_new)
        p = jnp.exp(s - (m_new - col_off))
        l_new = alpha * l_run + jnp.sum(p, axis=0, keepdims=True)
        acc_new = alpha * acc + _dot(vblk.T.astype(BF16), p.astype(BF16))
        return m_new, l_new, acc_new

    tbias = tb_ref[...]
    for j in range(n_pages):
        m_run, l_run, acc = update(k_refs[j][0], v_refs[j][0], tbias, cj_ref[j:j + 1, :], m_run, l_run, acc)
    m_run, l_run, acc = update(kn_ref[0], vn_ref[0], tn_ref[...], jnp.zeros((1, LANES), F32), m_run, l_run, acc)

    lam = _lambda_from(lamv_ref, lam_init)
    o_all = (acc / l_run).T
    o = o_all[0:rows] - lam * o_all[rows:2 * rows]
    o_ref[0] = _sub_ln(o, sg_ref[...], lam_init)


def _sample_bias_tables(n_pages, t_s):
    rows = t_s * C_HEADS
    past = n_pages * PAGE_SIZE
    col = jnp.arange(LANES)
    used = col < 2 * rows
    c_r = col % rows
    c_t = c_r // C_HEADS
    c_h = c_r % C_HEADS
    slope = jnp.where(used, jnp.asarray(ALIBI_SLOPES, F32)[c_h], 0.0)
    prow = jnp.arange(PAGE_SIZE * C_HEADS)
    p_tok = prow // C_HEADS
    p_h = prow % C_HEADS
    ok = (p_h[:, None] == c_h[None, :]) | ~used[None, :]
    tb = jnp.where(ok, slope[None, :] * p_tok[:, None].astype(F32), NEG_BIG)
    starts = jnp.arange(n_pages) * PAGE_SIZE
    cj = -slope[None, :] * (past + c_t[None, :] - starts[:, None]).astype(F32)
    nrow = jnp.arange(rows)
    n_t = nrow // C_HEADS
    n_h = nrow % C_HEADS
    okn = ((n_h[:, None] == c_h[None, :]) & (n_t[:, None] <= c_t[None, :])) | ~used[None, :]
    tn = jnp.where(okn, -slope[None, :] * (c_t[None, :] - n_t[:, None]).astype(F32), NEG_BIG)
    return tb.astype(F32), cj.astype(F32), tn.astype(F32)


def _attn_sample(qn3, kn3, vn3, ck, cv, page_table, page_base, lamv, sg, tables, lam_init):
    n, rows, _ = qn3.shape
    n_pages = page_table.shape[1]
    pcols = PAGE_SIZE * C_HEADS
    assert 2 * rows <= LANES
    seq = lambda: pl.BlockSpec((1, rows, LANES), lambda i, pt: (i, 0, 0))
    page = lambda j: pl.BlockSpec((1, pcols, LANES), lambda i, pt, j=j: (page_base + pt[i, j], 0, 0))
    const = lambda a: pl.BlockSpec(a.shape, lambda i, pt: (0, 0))
    grid_spec = pltpu.PrefetchScalarGridSpec(
        num_scalar_prefetch=1,
        grid=(n,),
        in_specs=[seq(), seq(), seq(), const(lamv), const(sg)] + [const(a) for a in tables]
                 + [page(j) for j in range(n_pages)] + [page(j) for j in range(n_pages)],
        out_specs=seq(),
    )
    return pl.pallas_call(
        functools.partial(_attn_sample_kernel, n_pages=n_pages, t_s=rows // C_HEADS, lam_init=lam_init),
        grid_spec=grid_spec,
        out_shape=jax.ShapeDtypeStruct((n, rows, LANES), F32),
        compiler_params=_params("parallel"),
        name="attn_sample",
    )(page_table, qn3, kn3, vn3, lamv, sg, *tables, *([ck] * n_pages), *([cv] * n_pages))


def _out_proj_kernel(x_ref, y_ref, gb_ref, yb_ref, yc_ref, lg_ref, lb_ref, bd_ref, w_ref, o_ref):
    y = y_ref[...]
    bd = bd_ref[...]
    inv = 1.0 / A_HEAD_DIM
    yc = y - _seg_sum(y, bd) * inv
    yn = yc * lax.rsqrt(_seg_sum1(yc * yc, bd) * inv + A_GN_EPS)
    ya = (yn * lg_ref[...] + lb_ref[...] + gb_ref[:, A_WIDTH:]) * gb_ref[:, :A_WIDTH]
    acc = _dot(ya.astype(BF16), w_ref[0:A_WIDTH, :])
    acc += _dot(yb_ref[...].astype(BF16), w_ref[A_WIDTH:A_WIDTH + B_WIDTH, :])
    acc += _dot(yc_ref[...].astype(BF16), w_ref[A_WIDTH + B_WIDTH:, :])
    o_ref[...] = x_ref[...] + acc


def _out_proj(x, y_t, gb, yb, yc, lp, bd256, w_bf):
    rows = x.shape[0]
    t, n, _ = y_t.shape
    tm = _seq_tile(t) or _pick_tile(rows, 512, 8)
    row = lambda c: pl.BlockSpec((tm, c), lambda i: (i, 0))
    vec = lambda a: a.reshape(1, -1)
    if _seq_tile(t):
        y, y_spec = y_t.reshape(t, n * A_WIDTH), _time_major_spec(tm, A_WIDTH, t // tm)
    else:
        y, y_spec = y_t.transpose(1, 0, 2).reshape(rows, A_WIDTH), row(A_WIDTH)
    return pl.pallas_call(
        _out_proj_kernel,
        grid=(rows // tm,),
        in_specs=[row(D_MODEL), y_spec, row(2 * A_WIDTH), row(B_WIDTH), row(C_WIDTH),
                  _full((1, A_WIDTH)), _full((1, A_WIDTH)), _full((2 * A_WIDTH, A_WIDTH)),
                  _full((D_MODEL, D_MODEL))],
        out_specs=row(D_MODEL),
        out_shape=jax.ShapeDtypeStruct((rows, D_MODEL), F32),
        compiler_params=_params("parallel"),
        name="out_proj",
    )(x, y, gb, yb, yc, vec(lp['a_lnx_g']), vec(lp['a_lnx_b']), bd256, w_bf)


GATE_LANE = 0
FLAG_LANE = N_EXPERTS
RANK_LANE = 2 * N_EXPERTS


def _router_kernel(x_ref, g_ref, r_ref, tri_ref, eye_ref, h_ref, rec_ref, rect_ref):
    x = x_ref[...]
    h = x * lax.rsqrt(jnp.mean(x * x, axis=-1, keepdims=True) + EPS) * g_ref[...]
    h_ref[...] = h.astype(BF16)
    logits = _dot_hi(h, r_ref[...])
    lane = lax.broadcasted_iota(jnp.int32, logits.shape, 1).astype(F32)
    lg = jnp.where(lane < N_EXPERTS, logits, NEG_BIG)
    m1 = jnp.max(lg, axis=-1, keepdims=True)
    i1 = jnp.min(jnp.where(lg == m1, lane, float(LANES)), axis=-1, keepdims=True)
    lg2 = jnp.where(lane == i1, NEG_BIG, lg)
    m2 = jnp.max(lg2, axis=-1, keepdims=True)
    i2 = jnp.min(jnp.where(lg2 == m2, lane, float(LANES)), axis=-1, keepdims=True)
    e = jnp.exp(m2 - m1)
    g1 = 1.0 / (1.0 + e)
    gates = jnp.where(lane == i1, g1, 0.0) + jnp.where(lane == i2, e * g1, 0.0)
    chosen = lambda off: jnp.where((lane == i1 + off) | (lane == i2 + off), 1.0, 0.0)
    rank = _dot(tri_ref[...], chosen(float(RANK_LANE)).astype(BF16))
    rec = gates + chosen(float(FLAG_LANE)) + rank
    rec_ref[...] = rec
    rect_ref[0] = lax.dot_general(eye_ref[...], rec, (((1,), (1,)), ((), ())),
                                  preferred_element_type=F32, precision=HIGHEST)


def _router(x, g, router_pad, tw):
    rows = x.shape[0]
    nw = rows // tw
    tri = (jnp.arange(tw)[:, None] > jnp.arange(tw)[None, :]).astype(BF16)
    eye = jnp.eye(LANES, dtype=F32)
    return pl.pallas_call(
        _router_kernel,
        grid=(nw,),
        in_specs=[pl.BlockSpec((tw, D_MODEL), lambda i: (i, 0)), _full((1, D_MODEL)),
                  _full((D_MODEL, LANES)), _full((tw, tw)), _full((LANES, LANES))],
        out_specs=[pl.BlockSpec((tw, D_MODEL), lambda i: (i, 0)),
                   pl.BlockSpec((tw, LANES), lambda i: (i, 0)),
                   pl.BlockSpec((1, LANES, tw), lambda i: (i, 0, 0))],
        out_shape=[jax.ShapeDtypeStruct((rows, D_MODEL), BF16),
                   jax.ShapeDtypeStruct((rows, LANES), F32),
                   jax.ShapeDtypeStruct((nw, LANES, tw), F32)],
        compiler_params=_params("parallel"),
        name="router",
    )(x, g, router_pad, tri, eye)


def _moe_kernel(h_ref, rec_ref, rect_ref, acc_ref, wg_ref, wu_ref, wd_ref, o_ref, *, e, cap, small):
    tw = h_ref.shape[0]
    rec = rec_ref[...]
    rect = rect_ref[0]
    lane = lax.broadcasted_iota(jnp.int32, rec.shape, 1)
    sub = lax.broadcasted_iota(jnp.int32, rect.shape, 0)
    col = lambda off: jnp.sum(jnp.where(lane == off + e, rec, 0.0), axis=-1, keepdims=True)
    row = lambda off: jnp.sum(jnp.where(sub == off + e, rect, 0.0), axis=0, keepdims=True)
    gate_c, flag_c, rank_c = col(GATE_LANE), col(FLAG_LANE), col(RANK_LANE)
    flag_r, rank_r = row(FLAG_LANE), row(RANK_LANE)
    o_ref[...] = acc_ref[...]
    count = jnp.max(rank_c + flag_c)

    def run_pass(base, size):
        slot_r = lax.broadcasted_iota(jnp.int32, (size, tw), 0).astype(F32)
        slot_c = lax.broadcasted_iota(jnp.int32, (tw, size), 1).astype(F32)
        sel = jnp.where((rank_r - base == slot_r) & (flag_r > 0.0), 1.0, 0.0).astype(BF16)
        sel_t = jnp.where((rank_c - base == slot_c) & (flag_c > 0.0), 1.0, 0.0).astype(BF16)
        xs = _dot(sel, h_ref[...]).astype(BF16)
        a = _dot(xs, wg_ref[0])
        u = _dot(xs, wu_ref[0])
        y = _dot((a * _sigmoid(a) * u).astype(BF16), wd_ref[0])
        o_ref[...] += gate_c * _dot(sel_t, y.astype(BF16))

    n_full = sum((count >= float((k + 1) * cap)).astype(jnp.int32) for k in range(tw // cap))

    def full_pass(p, carry):
        run_pass((p * cap).astype(F32), cap)
        return carry

    lax.fori_loop(0, n_full, full_pass, 0)
    done = (n_full * cap).astype(F32)
    rem = count - done

    @pl.when(rem > float(small))
    def _():
        run_pass(done, cap)

    @pl.when((rem > 0.0) & (rem <= float(small)))
    def _():
        run_pass(done, small)


def _moe_window(rows):
    best = None
    for d in range(16, min(rows, MOE_MAX_WINDOW) + 1, 16):
        if rows % d == 0:
            best = d
    assert best is not None, rows
    return best


def _moe(h, rec, rect, x, wg, wu, wd, tw):
    rows = x.shape[0]
    nw = rows // tw
    cap = -(-(tw * TOP_K * MOE_CAP_SLACK_PCT) // (N_EXPERTS * 100 * 16)) * 16
    small = -(-(2 * cap) // (3 * 16)) * 16
    win = lambda c: pl.BlockSpec((tw, c), lambda w: (w, 0))
    acc = x
    for e in range(wg.shape[0]):
        weight = lambda a, e=e: pl.BlockSpec((1,) + a.shape[1:], lambda w: (e, 0, 0),
                                             pipeline_mode=pl.Buffered(1))
        acc = pl.pallas_call(
            functools.partial(_moe_kernel, e=e, cap=cap, small=small),
            grid=(nw,),
            in_specs=[win(D_MODEL), win(LANES), pl.BlockSpec((1, LANES, tw), lambda w: (w, 0, 0)),
                      win(D_MODEL), weight(wg), weight(wu), weight(wd)],
            out_specs=win(D_MODEL),
            out_shape=jax.ShapeDtypeStruct((rows, D_MODEL), F32),
            compiler_params=_params("parallel"),
            name="moe",
        )(h, rec, rect, acc, wg, wu, wd)
    return acc


def _ffn_kernel(x_ref, g_ref, gate_ref, wg_ref, wu_ref, wd_ref, o_ref, h_scr, acc_scr):
    e = pl.program_id(1)
    f = pl.program_id(2)

    @pl.when((e == 0) & (f == 0))
    def _():
        x = x_ref[...]
        h = x * lax.rsqrt(jnp.mean(x * x, axis=-1, keepdims=True) + EPS) * g_ref[...]
        h_scr[...] = h.astype(BF16)
        acc_scr[...] = jnp.zeros(acc_scr.shape, F32)

    h = h_scr[...]
    a = _dot(h, wg_ref[0])
    u = _dot(h, wu_ref[0])
    act = (a * _sigmoid(a) * u).astype(BF16)
    lane = lax.broadcasted_iota(jnp.int32, gate_ref.shape, 1)
    gcol = jnp.sum(jnp.where(lane == e, gate_ref[...], 0.0), axis=-1, keepdims=True)
    acc_scr[...] += gcol * _dot(act, wd_ref[0])

    @pl.when((e == pl.num_programs(1) - 1) & (f == pl.num_programs(2) - 1))
    def _():
        o_ref[...] = x_ref[...] + acc_scr[...]


def _ffn(x, g, gates, wg, wu, wd):
    rows = x.shape[0]
    n_e = wg.shape[0]
    tm = _pick_tile(rows, 704, 16)
    tf = D_FF // 2
    return pl.pallas_call(
        _ffn_kernel,
        grid=(rows // tm, n_e, D_FF // tf),
        in_specs=[pl.BlockSpec((tm, D_MODEL), lambda i, e, f: (i, 0)),
                  pl.BlockSpec((1, D_MODEL), lambda i, e, f: (0, 0)),
                  pl.BlockSpec((tm, LANES), lambda i, e, f: (i, 0)),
                  pl.BlockSpec((1, D_MODEL, tf), lambda i, e, f: (e, 0, f)),
                  pl.BlockSpec((1, D_MODEL, tf), lambda i, e, f: (e, 0, f)),
                  pl.BlockSpec((1, tf, D_MODEL), lambda i, e, f: (e, f, 0))],
        out_specs=pl.BlockSpec((tm, D_MODEL), lambda i, e, f: (i, 0)),
        out_shape=jax.ShapeDtypeStruct((rows, D_MODEL), F32),
        scratch_shapes=[pltpu.VMEM((tm, D_MODEL), BF16), pltpu.VMEM((tm, D_MODEL), F32)],
        compiler_params=_params("parallel", "arbitrary", "arbitrary"),
        name="ffn",
    )(x, g, gates, wg, wu, wd)


def _wkv_to_pairs(s):
    n = s.shape[0]
    assert n % SCAN_SEQS == 0
    s = s.reshape(n // SCAN_SEQS, SCAN_SEQS, 2, 2, A_HEAD_DIM, A_HEAD_DIM)
    return s.transpose(0, 4, 2, 1, 3, 5).reshape(n // SCAN_SEQS, A_HEAD_DIM, 2 * SCAN_SEQS, LANES)


def _wkv_from_pairs(s):
    nblk = s.shape[0]
    s = s.reshape(nblk, A_HEAD_DIM, 2, SCAN_SEQS, 2, A_HEAD_DIM)
    return s.transpose(0, 3, 2, 4, 1, 5).reshape(nblk * SCAN_SEQS, A_HEADS, A_HEAD_DIM, A_HEAD_DIM)


def _mixers(x, n, t, shift0, wkv0, conv0, attend, lp, c, lam_init):
    z, qn, kn, kb, v, vb = _in_proj(x, lp['norm1_g'].reshape(1, -1), lp['w_in_bf'], lp['qg'], lp['kg'],
                                    c['bd512'])
    sc_t, gb = _rwkv_pre(z, shift0, lp, c['bd256'], n, t)
    y_t, s1 = _rwkv_scan(sc_t, _wkv_to_pairs(wkv0), c['bd128'], c['eye'])
    yb, conv1 = _conformer_conv(z, conv0, lp, lp['pw_bf'], n, t)
    yc = attend(qn, kn, kb, vb, v)
    x = _out_proj(x, y_t, gb, yb, yc, lp, c['bd256'], lp['w_out_bf'])
    k_rows = kn.reshape(n, t, C_HEADS, 2 * C_QK_DIM)
    v_rows = v.reshape(n, t, C_HEADS, C_V_DIM)
    shift1 = z.reshape(n, t, AB_COLS)[:, -1, :A_COLS]
    return x, k_rows, v_rows, _wkv_from_pairs(s1), shift1, conv1


def kernel(x_prompt, x_sample, cache_k, cache_v, page_table, state_wkv, state_shift, state_conv, meta_tokens, norm1_g, norm2_g, w_in, w_out, a_mu, a_w0, a_w_up, a_a0, a_a_up, a_g_up, a_k_k, a_k_a, a_r_k, a_lnx_g, a_lnx_b, b_conv_w, b_conv_b, b_ln_g, b_ln_b, b_pw_w, b_pw_b, c_qn_g, c_kn_g, c_lam_q1, c_lam_k1, c_lam_q2, c_lam_k2, c_subln_g, ffn_w_gate, ffn_w_up, ffn_w_down, moe_router, moe_w_gate, moe_w_up, moe_w_down):
    depth = w_in.shape[0]
    b, seq, _ = x_prompt.shape
    n_s, t_s, _ = x_sample.shape
    t_p = seq + N_META
    n_pool = cache_k.shape[1]
    pcols = PAGE_SIZE * C_HEADS

    consts = {
        'bd512': _block_diag2(C_QK_COLS),
        'bd256': _block_diag2(A_WIDTH),
        'bd128': _block_diag2(LANES),
        'eye': jnp.broadcast_to(
            (jnp.arange(LANES)[None, None, :] % A_HEAD_DIM == jnp.arange(A_HEAD_DIM)[:, None, None]).astype(F32),
            (A_HEAD_DIM, 2 * SCAN_SEQS, LANES)),
    }
    meta = jnp.broadcast_to(meta_tokens.astype(F32)[None], (b, N_META, D_MODEL))
    xp = jnp.concatenate([meta, x_prompt], axis=1).reshape(b * t_p, D_MODEL)
    xs = x_sample.reshape(n_s * t_s, D_MODEL)
    ck_all = cache_k.reshape(depth * n_pool, pcols, LANES)
    cv_all = cache_v.reshape(depth * n_pool, pcols, LANES)
    sample_tables = _sample_bias_tables(page_table.shape[1], t_s)

    outs = [[] for _ in range(10)]
    for l in range(depth):
        lam_init = 0.8 - 0.6 * math.exp(-0.3 * l)
        lp = {
            'norm1_g': norm1_g[l], 'w_in_bf': w_in[l].astype(BF16), 'w_out_bf': w_out[l].astype(BF16),
            'a_mu': a_mu[l], 'a_w0': a_w0[l], 'a_w_up': a_w_up[l], 'a_a0': a_a0[l], 'a_a_up': a_a_up[l],
            'a_g_up': a_g_up[l], 'a_k_k': a_k_k[l], 'a_k_a': a_k_a[l], 'a_r_k': a_r_k[l],
            'a_lnx_g': a_lnx_g[l], 'a_lnx_b': a_lnx_b[l],
            'b_conv_w': b_conv_w[l], 'b_conv_b': b_conv_b[l], 'b_ln_g': b_ln_g[l], 'b_ln_b': b_ln_b[l],
            'pw_bf': b_pw_w[l].astype(BF16), 'b_pw_b': b_pw_b[l],
            'qg': jnp.tile(c_qn_g[l], 2 * C_HEADS).reshape(1, -1),
            'kg': jnp.tile(c_kn_g[l], 2 * C_HEADS).reshape(1, -1),
        }
        lamv = jnp.stack([c_lam_q1[l], c_lam_k1[l], c_lam_q2[l], c_lam_k2[l]]).astype(F32)
        sg = c_subln_g[l].reshape(1, -1)

        def attend_prompt(qn, kn, kb, vb, v, lamv=lamv, sg=sg, lam_init=lam_init):
            return _attn_prompt(qn, kb, vb, lamv, sg, b, t_p, lam_init)

        def attend_sample(qn, kn, kb, vb, v, l=l, lamv=lamv, sg=sg, lam_init=lam_init):
            rows = t_s * C_HEADS
            qn3 = qn.reshape(n_s, rows, LANES)
            kn3 = kn.reshape(n_s, rows, LANES)
            vn3 = v.reshape(n_s, rows, LANES)
            o = _attn_sample(qn3, kn3, vn3, ck_all, cv_all, page_table, l * n_pool, lamv, sg, sample_tables,
                             lam_init)
            return o.reshape(n_s * t_s, C_WIDTH)

        xp, k_p, v_p, wkv_p, shift_p, conv_p = _mixers(
            xp, b, t_p, jnp.zeros((b, A_COLS), F32), jnp.zeros((b, A_HEADS, A_HEAD_DIM, A_HEAD_DIM), F32),
            jnp.zeros((b, CONV_HIST, B_WIDTH), F32), attend_prompt, lp, consts, lam_init)
        xs, k_s, v_s, wkv_s, shift_s, conv_s = _mixers(
            xs, n_s, t_s, state_shift[l], state_wkv[l], state_conv[l], attend_sample, lp, consts, lam_init)

        i = l // 2
        g2 = norm2_g[l].reshape(1, -1)
        if l % 2 == 0:
            wg = ffn_w_gate[i].astype(BF16)[None]
            wu = ffn_w_up[i].astype(BF16)[None]
            wd = ffn_w_down[i].astype(BF16)[None]
            xp = _ffn(xp, g2, jnp.ones((xp.shape[0], LANES), F32), wg, wu, wd)
            xs = _ffn(xs, g2, jnp.ones((xs.shape[0], LANES), F32), wg, wu, wd)
        else:
            wg = moe_w_gate[i].astype(BF16)
            wu = moe_w_up[i].astype(BF16)
            wd = moe_w_down[i].astype(BF16)
            router_pad = jnp.pad(moe_router[i], ((0, 0), (0, LANES - N_EXPERTS)))
            tw_p, tw_s = _moe_window(xp.shape[0]), _moe_window(xs.shape[0])
            xp = _moe(*_router(xp, g2, router_pad, tw_p), xp, wg, wu, wd, tw_p)
            xs = _moe(*_router(xs, g2, router_pad, tw_s), xs, wg, wu, wd, tw_s)

        for lst, val in zip(outs, (k_p, v_p, k_s, v_s, wkv_p, wkv_s, shift_p, shift_s, conv_p, conv_s)):
            lst.append(val)

    y_prompt = xp.reshape(b, t_p, D_MODEL)[:, N_META:]
    y_sample = xs.reshape(n_s, t_s, D_MODEL)
    return (y_prompt, y_sample) + tuple(jnp.stack(o) for o in outs)
```

```python
import functools
import math

import jax
import jax.numpy as jnp
from jax import lax
from jax.experimental import pallas as pl
from jax.experimental.pallas import tpu as pltpu

F32 = jnp.float32
BF16 = jnp.bfloat16
HIGHEST = lax.Precision.HIGHEST

D_MODEL = 1024
EPS = 1e-6
N_META = 16
A_HEAD_DIM = 64
A_WIDTH = 256
A_HEADS = 4
A_W_LORA = 64
A_A_LORA = 64
A_G_LORA = 128
A_COLS = 1024
A_GN_EPS = 64e-5
B_WIDTH = 256
B_COLS = 512
CONV_WIDTH = 31
CONV_HIST = CONV_WIDTH - 1
C_QK_DIM = 64
C_V_DIM = 128
C_WIDTH = 512
C_HEADS = 4
C_QK_COLS = 512
IN_COLS = 3072
Q_OFF = A_COLS + B_COLS
K_OFF = Q_OFF + C_QK_COLS
V_OFF = K_OFF + C_QK_COLS
AB_COLS = A_COLS + B_COLS
D_FF = 2816
N_EXPERTS = 8
PAGE_SIZE = 128
LANES = 128
NEG_BIG = -1e30
ALIBI_SLOPES = tuple(float((2.0 ** (-8.0 / C_HEADS)) ** (h + 1)) for h in range(C_HEADS))
EXP_NEG_HALF = math.exp(-0.5)
SCAN_SEQS = 8
SCAN_COLS = 7 * A_WIDTH
SCAN_SUB = 8
TOP_K = 2
MIN_TIME_MAJOR_T = 64
MOE_MAX_WINDOW = 704
MOE_CAP_SLACK_PCT = 130
VMEM_LIMIT = 56 * 1024 * 1024


def _params(*sem):
    return pltpu.CompilerParams(dimension_semantics=sem, vmem_limit_bytes=VMEM_LIMIT)


def _pick_tile(n, target, mult=8):
    best = None
    for d in range(mult, min(n, target) + 1, mult):
        if n % d == 0:
            best = d
    return n if best is None else best


def _dot(a, b):
    return jnp.dot(a, b, preferred_element_type=F32)


def _dot_hi(a, b):
    return jnp.dot(a, b, preferred_element_type=F32, precision=HIGHEST)


def _dot_nt(a, b):
    return lax.dot_general(a, b, (((1,), (1,)), ((), ())), preferred_element_type=F32)


def _sigmoid(x):
    return 1.0 / (1.0 + jnp.exp(-x))


def _seg_sum(x, bd2):
    hi = x.astype(BF16)
    lo = (x - hi.astype(F32)).astype(BF16)
    return _dot(jnp.concatenate([hi, lo], axis=-1), bd2)


def _seg_sum1(x, bd2):
    return _dot(x.astype(BF16), bd2[0:x.shape[-1]])


def _block_diag2(width, seg=64):
    r = jnp.arange(width) // seg
    bd = (r[:, None] == r[None, :]).astype(BF16)
    return jnp.concatenate([bd, bd], axis=0)


def _full(shape):
    nd = len(shape)
    return pl.BlockSpec(shape, lambda *_: (0,) * nd)


def _in_proj_kernel(x_ref, g_ref, w_ref, qg_ref, kg_ref, bd_ref, z_ref, qn_ref, kn_ref, kb_ref, v_ref, vb_ref):
    x = x_ref[...]
    h = x * lax.rsqrt(jnp.mean(x * x, axis=-1, keepdims=True) + EPS) * g_ref[...]
    z = _dot(h.astype(BF16), w_ref[...])
    z_ref[...] = z[:, :AB_COLS]
    v_ref[...] = z[:, V_OFF:]
    q = z[:, Q_OFF:Q_OFF + C_QK_COLS]
    k = z[:, K_OFF:K_OFF + C_QK_COLS]
    bd = bd_ref[...]
    inv = 1.0 / C_QK_DIM
    qn = q * lax.rsqrt(_seg_sum1(q * q, bd) * inv + EPS) * qg_ref[...]
    kn = k * lax.rsqrt(_seg_sum1(k * k, bd) * inv + EPS) * kg_ref[...]
    qn_ref[...] = qn * (C_QK_DIM ** -0.5)
    kn_ref[...] = kn
    kb_ref[...] = kn.astype(BF16)
    vb_ref[...] = z[:, V_OFF:].astype(BF16)


def _in_proj(x, g, w_bf, qg, kg, bd512):
    rows = x.shape[0]
    tm = _pick_tile(rows, 512, 16)
    row = lambda c: pl.BlockSpec((tm, c), lambda i: (i, 0))
    return pl.pallas_call(
        _in_proj_kernel,
        grid=(rows // tm,),
        in_specs=[row(D_MODEL), _full((1, D_MODEL)), _full((D_MODEL, IN_COLS)),
                  _full((1, C_QK_COLS)), _full((1, C_QK_COLS)), _full((2 * C_QK_COLS, C_QK_COLS))],
        out_specs=[row(AB_COLS), row(C_QK_COLS), row(C_QK_COLS), row(C_QK_COLS), row(C_WIDTH), row(C_WIDTH)],
        out_shape=[jax.ShapeDtypeStruct((rows, AB_COLS), F32),
                   jax.ShapeDtypeStruct((rows, C_QK_COLS), F32),
                   jax.ShapeDtypeStruct((rows, C_QK_COLS), F32),
                   jax.ShapeDtypeStruct((rows, C_QK_COLS), BF16),
                   jax.ShapeDtypeStruct((rows, C_WIDTH), F32),
                   jax.ShapeDtypeStruct((rows, C_WIDTH), BF16)],
        compiler_params=_params("parallel"),
        name="in_proj",
    )(x, g, w_bf, qg, kg, bd512)


def _rwkv_pre_kernel(za_ref, bnd_ref, mu_ref, w0_ref, wup_ref, a0_ref, aup_ref, gup_ref,
                     kk_ref, ka_ref, rk_ref, bd_ref, sc_ref, gb_ref, *, period):
    za = za_ref[...]
    rid = lax.broadcasted_iota(jnp.int32, za.shape, 0)
    at_boundary = (rid == 0) if period == za.shape[0] else (jnp.bitwise_and(rid, period - 1) == 0)
    prev = jnp.where(at_boundary, bnd_ref[...].reshape(-1, A_COLS), pltpu.roll(za, 1, 0))
    zs = za + mu_ref[...] * (prev - za)
    w = A_WIDTH
    r = zs[:, 0:w]
    k = zs[:, w:2 * w]
    v = zs[:, 2 * w:3 * w]
    wd = zs[:, 3 * w:3 * w + A_W_LORA]
    ad = zs[:, 3 * w + A_W_LORA:3 * w + A_W_LORA + A_A_LORA]
    gd = zs[:, 3 * w + A_W_LORA + A_A_LORA:]
    lw = w0_ref[...] + _dot_hi(jnp.tanh(wd), wup_ref[...])
    decay = jnp.exp(-EXP_NEG_HALF * _sigmoid(lw))
    a = _sigmoid(a0_ref[...] + _dot_hi(ad, aup_ref[...]))
    g = _dot(_sigmoid(gd).astype(BF16), gup_ref[...].astype(BF16))
    bd = bd_ref[...]
    kk = k * kk_ref[...]
    k2 = k * (1.0 + (a - 1.0) * ka_ref[...])
    kk = kk * lax.rsqrt(jnp.maximum(_seg_sum(kk * kk, bd), 1e-24))
    bonus = _seg_sum(r * k2 * rk_ref[...], bd) * v
    b = kk * a
    sc_ref[:, 0:w] = decay
    sc_ref[:, w:2 * w] = kk
    sc_ref[:, 2 * w:3 * w] = b
    sc_ref[:, 3 * w:4 * w] = k2
    sc_ref[:, 4 * w:5 * w] = v
    sc_ref[:, 5 * w:6 * w] = decay * r - kk * _seg_sum(b * r, bd)
    sc_ref[:, 6 * w:7 * w] = _seg_sum(k2 * r, bd)
    gb_ref[:, 0:w] = g
    gb_ref[:, w:2 * w] = bonus


def _time_major_spec(tm, cols, tiles_per_seq):
    return pl.BlockSpec((tm, cols), lambda i: (i % tiles_per_seq, i // tiles_per_seq))


def _seq_tile(t):
    return _pick_tile(t, 512, 8) if t >= MIN_TIME_MAJOR_T else None


def _rwkv_pre(z, shift0, lp, bd256, n, t):
    rows = z.shape[0]
    tm = _seq_tile(t) or _pick_tile(rows, 512, 8)
    row = lambda c: pl.BlockSpec((tm, c), lambda i: (i, 0))
    vec = lambda a: a.reshape(1, -1)
    if _seq_tile(t):
        tiles = t // tm
        last = z.reshape(n, t, AB_COLS)[:, tm - 1::tm, :A_COLS][:, :tiles - 1]
        bnd = jnp.concatenate([shift0[:, None, :], last], axis=1).reshape(n * tiles, 1, A_COLS)
        bnd_spec, period = pl.BlockSpec((1, 1, A_COLS), lambda i: (i, 0, 0)), tm
        sc_spec, sc_shape = _time_major_spec(tm, SCAN_COLS, tiles), (t, n * SCAN_COLS)
    else:
        assert t & (t - 1) == 0 and tm % t == 0
        bnd = jnp.repeat(shift0, t, axis=0)
        bnd_spec, period = row(A_COLS), t
        sc_spec, sc_shape = row(SCAN_COLS), (rows, SCAN_COLS)
    args = (z, bnd, vec(lp['a_mu']), vec(lp['a_w0']), lp['a_w_up'], vec(lp['a_a0']), lp['a_a_up'],
            lp['a_g_up'], vec(lp['a_k_k']), vec(lp['a_k_a']), vec(lp['a_r_k']), bd256)
    in_specs = [row(A_COLS), bnd_spec] + [_full(a.shape) for a in args[2:]]
    sc, gb = pl.pallas_call(
        functools.partial(_rwkv_pre_kernel, period=period),
        grid=(rows // tm,),
        in_specs=in_specs,
        out_specs=[sc_spec, row(2 * A_WIDTH)],
        out_shape=[jax.ShapeDtypeStruct(sc_shape, F32),
                   jax.ShapeDtypeStruct((rows, 2 * A_WIDTH), F32)],
        compiler_params=_params("parallel"),
        name="rwkv_pre",
    )(*args)
    if _seq_tile(t):
        return sc.reshape(t, n, SCAN_COLS), gb
    return sc.reshape(n, t, SCAN_COLS).transpose(1, 0, 2), gb


def _rwkv_scan_kernel(x_ref, s0_ref, bd_ref, eye_ref, eyeb_ref, y_ref, s1_ref, s_scr, vb_scr, *, tb, sub):
    nseq = x_ref.shape[1]
    npair = 2 * nseq
    w = A_WIDTH
    rows = A_HEAD_DIM * npair

    @pl.when(pl.program_id(1) == 0)
    def _():
        s_scr[...] = s0_ref[0]

    bd2 = bd_ref[...]
    bd1 = bd2[0:LANES]
    eye_bf = eyeb_ref[...]

    def rowvec(xt, c):
        return jnp.concatenate([xt[:, c:c + LANES], xt[:, c + LANES:c + 2 * LANES]], axis=0)

    def value_bcast(t_src, slot, row):
        v = rowvec(x_ref[t_src], 4 * w).astype(BF16)
        vb = _dot((eye_bf * v[None]).reshape(rows, LANES), bd1)
        vb_scr[slot, row] = vb.reshape(A_HEAD_DIM, npair, LANES)

    def seg1(a):
        nv = a.shape[0]
        return _dot(a.reshape(nv * npair, LANES).astype(BF16), bd1).reshape(nv, npair, LANES)

    for tt in range(sub):
        value_bcast(tt, 0, tt)

    def step(t, carry):
        slot = (t // sub) % 2
        row = t % sub
        value_bcast(jnp.minimum(t + sub, tb - 1), 1 - slot, row)
        xt = x_ref[t]
        dec, kk, b, k, v, rq, kr = (rowvec(xt, i * w) for i in range(7))
        s = s_scr[...]
        sa = seg1(s * kk[None])
        yq = seg1(s * rq[None])
        s_scr[...] = s * dec[None] - sa * b[None] + vb_scr[slot, row] * k[None]
        y = jnp.sum(yq * eye_ref[...], axis=0) + v * kr
        y_ref[t] = jnp.concatenate([y[0:nseq], y[nseq:npair]], axis=-1)
        return carry

    lax.fori_loop(0, tb, step, 0, unroll=2)

    @pl.when(pl.program_id(1) == pl.num_programs(1) - 1)
    def _():
        s1_ref[0] = s_scr[...]


def _rwkv_scan(sc_t, s0, bd128, eye):
    t, n, _ = sc_t.shape
    nb = s0.shape[2] // 2
    tb = _pick_tile(t, 64, SCAN_SUB)
    assert tb % SCAN_SUB == 0
    state_spec = pl.BlockSpec((1, A_HEAD_DIM, 2 * nb, LANES), lambda i, j: (i, 0, 0, 0))
    return pl.pallas_call(
        functools.partial(_rwkv_scan_kernel, tb=tb, sub=SCAN_SUB),
        grid=(n // nb, t // tb),
        in_specs=[pl.BlockSpec((tb, nb, SCAN_COLS), lambda i, j: (j, i, 0)),
                  state_spec,
                  _full((2 * LANES, LANES)), _full((A_HEAD_DIM, 2 * nb, LANES)),
                  _full((A_HEAD_DIM, 2 * nb, LANES))],
        out_specs=[pl.BlockSpec((tb, nb, A_WIDTH), lambda i, j: (j, i, 0)), state_spec],
        out_shape=[jax.ShapeDtypeStruct((t, n, A_WIDTH), F32),
                   jax.ShapeDtypeStruct(s0.shape, F32)],
        scratch_shapes=[pltpu.VMEM((A_HEAD_DIM, 2 * nb, LANES), F32),
                        pltpu.VMEM((2, SCAN_SUB, A_HEAD_DIM, 2 * nb, LANES), F32)],
        compiler_params=_params("parallel", "arbitrary"),
        name="rwkv_scan",
    )(sc_t, s0, bd128, eye, eye.astype(BF16))


_CONV_PAD = 32


def _conv_kernel(zb_ref, c0_ref, cw_ref, cb_ref, lg_ref, lb_ref, pw_ref, pb_ref,
                 y_ref, c1_ref, ext_scr, h_scr, *, nb, t, tc):
    cw = cw_ref[...]
    for s in range(nb):
        zb = zb_ref[s * t:(s + 1) * t, :]
        u = zb[:, :B_WIDTH] * _sigmoid(zb[:, B_WIDTH:])
        ext_scr[s, _CONV_PAD - CONV_HIST:_CONV_PAD, :] = c0_ref[s]
        ext_scr[s, _CONV_PAD:_CONV_PAD + t, :] = u
        c1_ref[s] = ext_scr[s, t + _CONV_PAD - CONV_HIST:t + _CONV_PAD, :]

        def chunk(c, carry, s=s):
            base = pl.multiple_of(c * tc, 8)
            win = ext_scr[s, pl.ds(base, tc + _CONV_PAD), :]
            acc = jnp.zeros((tc, B_WIDTH), F32)
            first = _CONV_PAD - CONV_HIST
            for b in range(8):
                taps = [j for j in range(CONV_WIDTH) if (first + j) % 8 == b]
                if not taps:
                    continue
                span = max(first + j - b for j in taps) + tc
                wb = win[b:b + span, :]
                for j in taps:
                    o = first + j - b
                    acc = acc + wb[o:o + tc, :] * cw[j:j + 1, :]
            acc = acc + cb_ref[...]
            xc = acc - jnp.mean(acc, axis=-1, keepdims=True)
            hn = xc * lax.rsqrt(jnp.mean(xc * xc, axis=-1, keepdims=True) + EPS)
            hn = hn * lg_ref[...] + lb_ref[...]
            hn = hn * _sigmoid(hn)
            h_scr[pl.ds(pl.multiple_of(s * t + c * tc, 8), tc), :] = hn
            return carry

        lax.fori_loop(0, t // tc, chunk, 0)
    y_ref[...] = _dot(h_scr[...].astype(BF16), pw_ref[...]) + pb_ref[...]


def _conformer_conv(z, conv0, lp, pw_bf, n, t):
    nb = 1 if t >= 64 else _pick_tile(n, 16, 1)
    tc = _pick_tile(t, 48, 8)
    rows = nb * t
    vec = lambda a: a.reshape(1, -1)
    return pl.pallas_call(
        functools.partial(_conv_kernel, nb=nb, t=t, tc=tc),
        grid=(n // nb,),
        in_specs=[pl.BlockSpec((rows, B_COLS), lambda i: (i, A_COLS // B_COLS)),
                  pl.BlockSpec((nb, CONV_HIST, B_WIDTH), lambda i: (i, 0, 0)),
                  _full((CONV_WIDTH, B_WIDTH)), _full((1, B_WIDTH)), _full((1, B_WIDTH)),
                  _full((1, B_WIDTH)), _full((B_WIDTH, B_WIDTH)), _full((1, B_WIDTH))],
        out_specs=[pl.BlockSpec((rows, B_WIDTH), lambda i: (i, 0)),
                   pl.BlockSpec((nb, CONV_HIST, B_WIDTH), lambda i: (i, 0, 0))],
        out_shape=[jax.ShapeDtypeStruct((n * t, B_WIDTH), F32),
                   jax.ShapeDtypeStruct((n, CONV_HIST, B_WIDTH), F32)],
        scratch_shapes=[pltpu.VMEM((nb, t + _CONV_PAD, B_WIDTH), F32),
                        pltpu.VMEM((rows, B_WIDTH), F32)],
        compiler_params=_params("parallel"),
        name="conformer_conv",
    )(z, conv0, lp['b_conv_w'], vec(lp['b_conv_b']), vec(lp['b_ln_g']), vec(lp['b_ln_b']),
      pw_bf, vec(lp['b_pw_b']))


def _lambda_from(lamv_ref, lam_init):
    lv = lamv_ref[...]
    s1 = jnp.sum(lv[0:1] * lv[1:2], axis=-1, keepdims=True)
    s2 = jnp.sum(lv[2:3] * lv[3:4], axis=-1, keepdims=True)
    return jnp.exp(s1) - jnp.exp(s2) + lam_init


def _sub_ln(o, sg, lam_init):
    o = o * lax.rsqrt(jnp.mean(o * o, axis=-1, keepdims=True) + EPS) * sg
    return o * (1.0 - lam_init)


def _map_masks(rows):
    lane = lax.broadcasted_iota(jnp.int32, (rows, LANES), 1)
    return lane < C_QK_DIM, lane >= C_QK_DIM


def _attn_prompt_kernel(q_ref, k_ref, v_ref, lamv_ref, sg_ref, o_ref, m_scr, l_scr, acc_scr,
                        *, tq, lam_init):
    qi = pl.program_id(1)
    kj = pl.program_id(2)

    @pl.when(kj == 0)
    def _():
        m_scr[...] = jnp.full(m_scr.shape, NEG_BIG, F32)
        l_scr[...] = jnp.zeros(l_scr.shape, F32)
        acc_scr[...] = jnp.zeros(acc_scr.shape, F32)

    @pl.when(kj <= qi)
    def _():
        qpos = qi * tq + lax.broadcasted_iota(jnp.int32, (tq, tq), 0)
        kpos = kj * tq + lax.broadcasted_iota(jnp.int32, (tq, tq), 1)
        dist = (qpos - kpos).astype(F32)
        causal = kpos <= qpos
        masks = _map_masks(tq)
        for h in range(C_HEADS):
            qh = q_ref[:, h * LANES:(h + 1) * LANES]
            kh = k_ref[:, h * LANES:(h + 1) * LANES]
            vh = v_ref[:, h * LANES:(h + 1) * LANES]
            for m in range(2):
                i = 2 * h + m
                qm = jnp.where(masks[m], qh, 0.0).astype(BF16)
                s = _dot_nt(qm, kh) - ALIBI_SLOPES[h] * dist
                s = jnp.where(causal, s, NEG_BIG)
                m_prev = m_scr[i]
                m_new = jnp.maximum(m_prev, jnp.max(s, axis=-1, keepdims=True))
                alpha = jnp.exp(m_prev - m_new)
                p = jnp.exp(s - m_new)
                l_scr[i] = alpha * l_scr[i] + jnp.sum(p, axis=-1, keepdims=True)
                acc_scr[i] = alpha * acc_scr[i] + _dot(p.astype(BF16), vh)
                m_scr[i] = m_new

    @pl.when(kj == qi)
    def _():
        lam = _lambda_from(lamv_ref, lam_init)
        for h in range(C_HEADS):
            o = acc_scr[2 * h] / l_scr[2 * h] - lam * (acc_scr[2 * h + 1] / l_scr[2 * h + 1])
            o_ref[:, h * LANES:(h + 1) * LANES] = _sub_ln(o, sg_ref[...], lam_init)


def _attn_prompt(qn, kb, vb, lamv, sg, n, t, lam_init):
    tq = _pick_tile(t, 704, 16)
    nq = t // tq
    return pl.pallas_call(
        functools.partial(_attn_prompt_kernel, tq=tq, lam_init=lam_init),
        grid=(n, nq, nq),
        in_specs=[pl.BlockSpec((tq, C_QK_COLS), lambda b, i, j: (b * nq + i, 0)),
                  pl.BlockSpec((tq, C_QK_COLS), lambda b, i, j: (b * nq + jnp.minimum(i, j), 0)),
                  pl.BlockSpec((tq, C_WIDTH), lambda b, i, j: (b * nq + jnp.minimum(i, j), 0)),
                  _full((4, C_QK_DIM)), _full((1, C_V_DIM))],
        out_specs=pl.BlockSpec((tq, C_WIDTH), lambda b, i, j: (b * nq + i, 0)),
        out_shape=jax.ShapeDtypeStruct((n * t, C_WIDTH), F32),
        scratch_shapes=[pltpu.VMEM((2 * C_HEADS, tq, 1), F32),
                        pltpu.VMEM((2 * C_HEADS, tq, 1), F32),
                        pltpu.VMEM((2 * C_HEADS, tq, C_V_DIM), F32)],
        compiler_params=_params("parallel", "parallel", "arbitrary"),
        name="attn_prompt",
    )(qn, kb, vb, lamv, sg)


def _attn_sample_kernel(pt_ref, q_ref, kn_ref, vn_ref, lamv_ref, sg_ref, tb_ref, cj_ref, tn_ref, *rest,
                        n_pages, t_s, lam_init):
    k_refs = rest[:n_pages]
    v_refs = rest[n_pages:2 * n_pages]
    o_ref = rest[2 * n_pages]
    del pt_ref
    rows = t_s * C_HEADS
    q = q_ref[0]
    masks = _map_masks(rows)
    qcat = jnp.concatenate([jnp.where(masks[0], q, 0.0), jnp.where(masks[1], q, 0.0),
                            jnp.zeros((LANES - 2 * rows, LANES), F32)], axis=0).astype(BF16)

    m_run = jnp.full((1, LANES), NEG_BIG, F32)
    l_run = jnp.zeros((1, LANES), F32)
    acc = jnp.zeros((C_V_DIM, LANES), F32)

    def update(kblk, vblk, bias, col_off, m_run, l_run, acc):
        s = _dot_nt(kblk.astype(BF16), qcat) + bias
        m_new = jnp.maximum(m_run, jnp.max(s, axis=0, keepdims=True) + col_off)
        alpha = jnp.exp(m_run - m_new)
        p = jnp.exp(s - (m_new - col_off))
        l_new = alpha * l_run + jnp.sum(p, axis=0, keepdims=True)
        acc_new = alpha * acc + _dot(vblk.T.astype(BF16), p.astype(BF16))
        return m_new, l_new, acc_new

    tbias = tb_ref[...]
    for j in range(n_pages):
        m_run, l_run, acc = update(k_refs[j][0], v_refs[j][0], tbias, cj_ref[j:j + 1, :], m_run, l_run, acc)
    m_run, l_run, acc = update(kn_ref[0], vn_ref[0], tn_ref[...], jnp.zeros((1, LANES), F32), m_run, l_run, acc)

    lam = _lambda_from(lamv_ref, lam_init)
    o_all = (acc / l_run).T
    o = o_all[0:rows] - lam * o_all[rows:2 * rows]
    o_ref[0] = _sub_ln(o, sg_ref[...], lam_init)


def _sample_bias_tables(n_pages, t_s):
    rows = t_s * C_HEADS
    past = n_pages * PAGE_SIZE
    col = jnp.arange(LANES)
    used = col < 2 * rows
    c_r = col % rows
    c_t = c_r // C_HEADS
    c_h = c_r % C_HEADS
    slope = jnp.where(used, jnp.asarray(ALIBI_SLOPES, F32)[c_h], 0.0)
    prow = jnp.arange(PAGE_SIZE * C_HEADS)
    p_tok = prow // C_HEADS
    p_h = prow % C_HEADS
    ok = (p_h[:, None] == c_h[None, :]) | ~used[None, :]
    tb = jnp.where(ok, slope[None, :] * p_tok[:, None].astype(F32), NEG_BIG)
    starts = jnp.arange(n_pages) * PAGE_SIZE
    cj = -slope[None, :] * (past + c_t[None, :] - starts[:, None]).astype(F32)
    nrow = jnp.arange(rows)
    n_t = nrow // C_HEADS
    n_h = nrow % C_HEADS
    okn = ((n_h[:, None] == c_h[None, :]) & (n_t[:, None] <= c_t[None, :])) | ~used[None, :]
    tn = jnp.where(okn, -slope[None, :] * (c_t[None, :] - n_t[:, None]).astype(F32), NEG_BIG)
    return tb.astype(F32), cj.astype(F32), tn.astype(F32)


def _attn_sample(qn3, kn3, vn3, ck, cv, page_table, page_base, lamv, sg, tables, lam_init):
    n, rows, _ = qn3.shape
    n_pages = page_table.shape[1]
    pcols = PAGE_SIZE * C_HEADS
    assert 2 * rows <= LANES
    seq = lambda: pl.BlockSpec((1, rows, LANES), lambda i, pt: (i, 0, 0))
    page = lambda j: pl.BlockSpec((1, pcols, LANES), lambda i, pt, j=j: (page_base + pt[i, j], 0, 0))
    const = lambda a: pl.BlockSpec(a.shape, lambda i, pt: (0, 0))
    grid_spec = pltpu.PrefetchScalarGridSpec(
        num_scalar_prefetch=1,
        grid=(n,),
        in_specs=[seq(), seq(), seq(), const(lamv), const(sg)] + [const(a) for a in tables]
                 + [page(j) for j in range(n_pages)] + [page(j) for j in range(n_pages)],
        out_specs=seq(),
    )
    return pl.pallas_call(
        functools.partial(_attn_sample_kernel, n_pages=n_pages, t_s=rows // C_HEADS, lam_init=lam_init),
        grid_spec=grid_spec,
        out_shape=jax.ShapeDtypeStruct((n, rows, LANES), F32),
        compiler_params=_params("parallel"),
        name="attn_sample",
    )(page_table, qn3, kn3, vn3, lamv, sg, *tables, *([ck] * n_pages), *([cv] * n_pages))


def _out_proj_kernel(x_ref, y_ref, gb_ref, yb_ref, yc_ref, lg_ref, lb_ref, bd_ref, w_ref, o_ref):
    y = y_ref[...]
    bd = bd_ref[...]
    inv = 1.0 / A_HEAD_DIM
    yc = y - _seg_sum(y, bd) * inv
    yn = yc * lax.rsqrt(_seg_sum1(yc * yc, bd) * inv + A_GN_EPS)
    ya = (yn * lg_ref[...] + lb_ref[...] + gb_ref[:, A_WIDTH:]) * gb_ref[:, :A_WIDTH]
    acc = _dot(ya.astype(BF16), w_ref[0:A_WIDTH, :])
    acc += _dot(yb_ref[...].astype(BF16), w_ref[A_WIDTH:A_WIDTH + B_WIDTH, :])
    acc += _dot(yc_ref[...].astype(BF16), w_ref[A_WIDTH + B_WIDTH:, :])
    o_ref[...] = x_ref[...] + acc


def _out_proj(x, y_t, gb, yb, yc, lp, bd256, w_bf):
    rows = x.shape[0]
    t, n, _ = y_t.shape
    tm = _seq_tile(t) or _pick_tile(rows, 512, 8)
    row = lambda c: pl.BlockSpec((tm, c), lambda i: (i, 0))
    vec = lambda a: a.reshape(1, -1)
    if _seq_tile(t):
        y, y_spec = y_t.reshape(t, n * A_WIDTH), _time_major_spec(tm, A_WIDTH, t // tm)
    else:
        y, y_spec = y_t.transpose(1, 0, 2).reshape(rows, A_WIDTH), row(A_WIDTH)
    return pl.pallas_call(
        _out_proj_kernel,
        grid=(rows // tm,),
        in_specs=[row(D_MODEL), y_spec, row(2 * A_WIDTH), row(B_WIDTH), row(C_WIDTH),
                  _full((1, A_WIDTH)), _full((1, A_WIDTH)), _full((2 * A_WIDTH, A_WIDTH)),
                  _full((D_MODEL, D_MODEL))],
        out_specs=row(D_MODEL),
        out_shape=jax.ShapeDtypeStruct((rows, D_MODEL), F32),
        compiler_params=_params("parallel"),
        name="out_proj",
    )(x, y, gb, yb, yc, vec(lp['a_lnx_g']), vec(lp['a_lnx_b']), bd256, w_bf)


GATE_LANE = 0
FLAG_LANE = N_EXPERTS
RANK_LANE = 2 * N_EXPERTS


def _router_kernel(x_ref, g_ref, r_ref, tri_ref, eye_ref, h_ref, rec_ref, rect_ref):
    x = x_ref[...]
    h = x * lax.rsqrt(jnp.mean(x * x, axis=-1, keepdims=True) + EPS) * g_ref[...]
    h_ref[...] = h.astype(BF16)
    logits = _dot_hi(h, r_ref[...])
    lane = lax.broadcasted_iota(jnp.int32, logits.shape, 1).astype(F32)
    lg = jnp.where(lane < N_EXPERTS, logits, NEG_BIG)
    m1 = jnp.max(lg, axis=-1, keepdims=True)
    i1 = jnp.min(jnp.where(lg == m1, lane, float(LANES)), axis=-1, keepdims=True)
    lg2 = jnp.where(lane == i1, NEG_BIG, lg)
    m2 = jnp.max(lg2, axis=-1, keepdims=True)
    i2 = jnp.min(jnp.where(lg2 == m2, lane, float(LANES)), axis=-1, keepdims=True)
    e = jnp.exp(m2 - m1)
    g1 = 1.0 / (1.0 + e)
    gates = jnp.where(lane == i1, g1, 0.0) + jnp.where(lane == i2, e * g1, 0.0)
    chosen = lambda off: jnp.where((lane == i1 + off) | (lane == i2 + off), 1.0, 0.0)
    rank = _dot(tri_ref[...], chosen(float(RANK_LANE)).astype(BF16))
    rec = gates + chosen(float(FLAG_LANE)) + rank
    rec_ref[...] = rec
    rect_ref[0] = lax.dot_general(eye_ref[...], rec, (((1,), (1,)), ((), ())),
                                  preferred_element_type=F32, precision=HIGHEST)


def _router(x, g, router_pad, tw):
    rows = x.shape[0]
    nw = rows // tw
    tri = (jnp.arange(tw)[:, None] > jnp.arange(tw)[None, :]).astype(BF16)
    eye = jnp.eye(LANES, dtype=F32)
    return pl.pallas_call(
        _router_kernel,
        grid=(nw,),
        in_specs=[pl.BlockSpec((tw, D_MODEL), lambda i: (i, 0)), _full((1, D_MODEL)),
                  _full((D_MODEL, LANES)), _full((tw, tw)), _full((LANES, LANES))],
        out_specs=[pl.BlockSpec((tw, D_MODEL), lambda i: (i, 0)),
                   pl.BlockSpec((tw, LANES), lambda i: (i, 0)),
                   pl.BlockSpec((1, LANES, tw), lambda i: (i, 0, 0))],
        out_shape=[jax.ShapeDtypeStruct((rows, D_MODEL), BF16),
                   jax.ShapeDtypeStruct((rows, LANES), F32),
                   jax.ShapeDtypeStruct((nw, LANES, tw), F32)],
        compiler_params=_params("parallel"),
        name="router",
    )(x, g, router_pad, tri, eye)


def _moe_kernel(h_ref, rec_ref, rect_ref, acc_ref, wg_ref, wu_ref, wd_ref, o_ref, *, e, cap, small):
    tw = h_ref.shape[0]
    rec = rec_ref[...]
    rect = rect_ref[0]
    lane = lax.broadcasted_iota(jnp.int32, rec.shape, 1)
    sub = lax.broadcasted_iota(jnp.int32, rect.shape, 0)
    col = lambda off: jnp.sum(jnp.where(lane == off + e, rec, 0.0), axis=-1, keepdims=True)
    row = lambda off: jnp.sum(jnp.where(sub == off + e, rect, 0.0), axis=0, keepdims=True)
    gate_c, flag_c, rank_c = col(GATE_LANE), col(FLAG_LANE), col(RANK_LANE)
    flag_r, rank_r = row(FLAG_LANE), row(RANK_LANE)
    o_ref[...] = acc_ref[...]
    count = jnp.max(rank_c + flag_c)

    def run_pass(base, size):
        slot_r = lax.broadcasted_iota(jnp.int32, (size, tw), 0).astype(F32)
        slot_c = lax.broadcasted_iota(jnp.int32, (tw, size), 1).astype(F32)
        sel = jnp.where((rank_r - base == slot_r) & (flag_r > 0.0), 1.0, 0.0).astype(BF16)
        sel_t = jnp.where((rank_c - base == slot_c) & (flag_c > 0.0), 1.0, 0.0).astype(BF16)
        xs = _dot(sel, h_ref[...]).astype(BF16)
        a = _dot(xs, wg_ref[0])
        u = _dot(xs, wu_ref[0])
        y = _dot((a * _sigmoid(a) * u).astype(BF16), wd_ref[0])
        o_ref[...] += gate_c * _dot(sel_t, y.astype(BF16))

    n_full = sum((count >= float((k + 1) * cap)).astype(jnp.int32) for k in range(tw // cap))

    def full_pass(p, carry):
        run_pass((p * cap).astype(F32), cap)
        return carry

    lax.fori_loop(0, n_full, full_pass, 0)
    done = (n_full * cap).astype(F32)
    rem = count - done

    @pl.when(rem > float(small))
    def _():
        run_pass(done, cap)

    @pl.when((rem > 0.0) & (rem <= float(small)))
    def _():
        run_pass(done, small)


def _moe_window(rows):
    best = None
    for d in range(16, min(rows, MOE_MAX_WINDOW) + 1, 16):
        if rows % d == 0:
            best = d
    assert best is not None, rows
    return best


def _moe(h, rec, rect, x, wg, wu, wd, tw):
    rows = x.shape[0]
    nw = rows // tw
    cap = -(-(tw * TOP_K * MOE_CAP_SLACK_PCT) // (N_EXPERTS * 100 * 16)) * 16
    small = -(-(2 * cap) // (3 * 16)) * 16
    win = lambda c: pl.BlockSpec((tw, c), lambda w: (w, 0))
    acc = x
    for e in range(wg.shape[0]):
        weight = lambda a, e=e: pl.BlockSpec((1,) + a.shape[1:], lambda w: (e, 0, 0),
                                             pipeline_mode=pl.Buffered(1))
        acc = pl.pallas_call(
            functools.partial(_moe_kernel, e=e, cap=cap, small=small),
            grid=(nw,),
            in_specs=[win(D_MODEL), win(LANES), pl.BlockSpec((1, LANES, tw), lambda w: (w, 0, 0)),
                      win(D_MODEL), weight(wg), weight(wu), weight(wd)],
            out_specs=win(D_MODEL),
            out_shape=jax.ShapeDtypeStruct((rows, D_MODEL), F32),
            compiler_params=_params("parallel"),
            name="moe",
        )(h, rec, rect, acc, wg, wu, wd)
    return acc


def _ffn_kernel(x_ref, g_ref, gate_ref, wg_ref, wu_ref, wd_ref, o_ref, h_scr, acc_scr):
    e = pl.program_id(1)
    f = pl.program_id(2)

    @pl.when((e == 0) & (f == 0))
    def _():
        x = x_ref[...]
        h = x * lax.rsqrt(jnp.mean(x * x, axis=-1, keepdims=True) + EPS) * g_ref[...]
        h_scr[...] = h.astype(BF16)
        acc_scr[...] = jnp.zeros(acc_scr.shape, F32)

    h = h_scr[...]
    a = _dot(h, wg_ref[0])
    u = _dot(h, wu_ref[0])
    act = (a * _sigmoid(a) * u).astype(BF16)
    lane = lax.broadcasted_iota(jnp.int32, gate_ref.shape, 1)
    gcol = jnp.sum(jnp.where(lane == e, gate_ref[...], 0.0), axis=-1, keepdims=True)
    acc_scr[...] += gcol * _dot(act, wd_ref[0])

    @pl.when((e == pl.num_programs(1) - 1) & (f == pl.num_programs(2) - 1))
    def _():
        o_ref[...] = x_ref[...] + acc_scr[...]


def _ffn(x, g, gates, wg, wu, wd):
    rows = x.shape[0]
    n_e = wg.shape[0]
    tm = _pick_tile(rows, 704, 16)
    tf = D_FF // 2
    return pl.pallas_call(
        _ffn_kernel,
        grid=(rows // tm, n_e, D_FF // tf),
        in_specs=[pl.BlockSpec((tm, D_MODEL), lambda i, e, f: (i, 0)),
                  pl.BlockSpec((1, D_MODEL), lambda i, e, f: (0, 0)),
                  pl.BlockSpec((tm, LANES), lambda i, e, f: (i, 0)),
                  pl.BlockSpec((1, D_MODEL, tf), lambda i, e, f: (e, 0, f)),
                  pl.BlockSpec((1, D_MODEL, tf), lambda i, e, f: (e, 0, f)),
                  pl.BlockSpec((1, tf, D_MODEL), lambda i, e, f: (e, f, 0))],
        out_specs=pl.BlockSpec((tm, D_MODEL), lambda i, e, f: (i, 0)),
        out_shape=jax.ShapeDtypeStruct((rows, D_MODEL), F32),
        scratch_shapes=[pltpu.VMEM((tm, D_MODEL), BF16), pltpu.VMEM((tm, D_MODEL), F32)],
        compiler_params=_params("parallel", "arbitrary", "arbitrary"),
        name="ffn",
    )(x, g, gates, wg, wu, wd)


def _wkv_to_pairs(s):
    n = s.shape[0]
    assert n % SCAN_SEQS == 0
    s = s.reshape(n // SCAN_SEQS, SCAN_SEQS, 2, 2, A_HEAD_DIM, A_HEAD_DIM)
    return s.transpose(0, 4, 2, 1, 3, 5).reshape(n // SCAN_SEQS, A_HEAD_DIM, 2 * SCAN_SEQS, LANES)


def _wkv_from_pairs(s):
    nblk = s.shape[0]
    s = s.reshape(nblk, A_HEAD_DIM, 2, SCAN_SEQS, 2, A_HEAD_DIM)
    return s.transpose(0, 3, 2, 4, 1, 5).reshape(nblk * SCAN_SEQS, A_HEADS, A_HEAD_DIM, A_HEAD_DIM)


def _mixers(x, n, t, shift0, wkv0, conv0, attend, lp, c, lam_init):
    z, qn, kn, kb, v, vb = _in_proj(x, lp['norm1_g'].reshape(1, -1), lp['w_in_bf'], lp['qg'], lp['kg'],
                                    c['bd512'])
    sc_t, gb = _rwkv_pre(z, shift0, lp, c['bd256'], n, t)
    y_t, s1 = _rwkv_scan(sc_t, _wkv_to_pairs(wkv0), c['bd128'], c['eye'])
    yb, conv1 = _conformer_conv(z, conv0, lp, lp['pw_bf'], n, t)
    yc = attend(qn, kn, kb, vb, v)
    x = _out_proj(x, y_t, gb, yb, yc, lp, c['bd256'], lp['w_out_bf'])
    k_rows = kn.reshape(n, t, C_HEADS, 2 * C_QK_DIM)
    v_rows = v.reshape(n, t, C_HEADS, C_V_DIM)
    shift1 = z.reshape(n, t, AB_COLS)[:, -1, :A_COLS]
    return x, k_rows, v_rows, _wkv_from_pairs(s1), shift1, conv1


def kernel(x_prompt, x_sample, cache_k, cache_v, page_table, state_wkv, state_shift, state_conv, meta_tokens, norm1_g, norm2_g, w_in, w_out, a_mu, a_w0, a_w_up, a_a0, a_a_up, a_g_up, a_k_k, a_k_a, a_r_k, a_lnx_g, a_lnx_b, b_conv_w, b_conv_b, b_ln_g, b_ln_b, b_pw_w, b_pw_b, c_qn_g, c_kn_g, c_lam_q1, c_lam_k1, c_lam_q2, c_lam_k2, c_subln_g, ffn_w_gate, ffn_w_up, ffn_w_down, moe_router, moe_w_gate, moe_w_up, moe_w_down):
    depth = w_in.shape[0]
    b, seq, _ = x_prompt.shape
    n_s, t_s, _ = x_sample.shape
    t_p = seq + N_META
    n_pool = cache_k.shape[1]
    pcols = PAGE_SIZE * C_HEADS

    consts = {
        'bd512': _block_diag2(C_QK_COLS),
        'bd256': _block_diag2(A_WIDTH),
        'bd128': _block_diag2(LANES),
        'eye': jnp.broadcast_to(
            (jnp.arange(LANES)[None, None, :] % A_HEAD_DIM == jnp.arange(A_HEAD_DIM)[:, None, None]).astype(F32),
            (A_HEAD_DIM, 2 * SCAN_SEQS, LANES)),
    }
    meta = jnp.broadcast_to(meta_tokens.astype(F32)[None], (b, N_META, D_MODEL))
    xp = jnp.concatenate([meta, x_prompt], axis=1).reshape(b * t_p, D_MODEL)
    xs = x_sample.reshape(n_s * t_s, D_MODEL)
    ck_all = cache_k.reshape(depth * n_pool, pcols, LANES)
    cv_all = cache_v.reshape(depth * n_pool, pcols, LANES)
    sample_tables = _sample_bias_tables(page_table.shape[1], t_s)

    outs = [[] for _ in range(10)]
    for l in range(depth):
        lam_init = 0.8 - 0.6 * math.exp(-0.3 * l)
        lp = {
            'norm1_g': norm1_g[l], 'w_in_bf': w_in[l].astype(BF16), 'w_out_bf': w_out[l].astype(BF16),
            'a_mu': a_mu[l], 'a_w0': a_w0[l], 'a_w_up': a_w_up[l], 'a_a0': a_a0[l], 'a_a_up': a_a_up[l],
            'a_g_up': a_g_up[l], 'a_k_k': a_k_k[l], 'a_k_a': a_k_a[l], 'a_r_k': a_r_k[l],
            'a_lnx_g': a_lnx_g[l], 'a_lnx_b': a_lnx_b[l],
            'b_conv_w': b_conv_w[l], 'b_conv_b': b_conv_b[l], 'b_ln_g': b_ln_g[l], 'b_ln_b': b_ln_b[l],
            'pw_bf': b_pw_w[l].astype(BF16), 'b_pw_b': b_pw_b[l],
            'qg': jnp.tile(c_qn_g[l], 2 * C_HEADS).reshape(1, -1),
            'kg': jnp.tile(c_kn_g[l], 2 * C_HEADS).reshape(1, -1),
        }
        lamv = jnp.stack([c_lam_q1[l], c_lam_k1[l], c_lam_q2[l], c_lam_k2[l]]).astype(F32)
        sg = c_subln_g[l].reshape(1, -1)

        def attend_prompt(qn, kn, kb, vb, v, lamv=lamv, sg=sg, lam_init=lam_init):
            return _attn_prompt(qn, kb, vb, lamv, sg, b, t_p, lam_init)

        def attend_sample(qn, kn, kb, vb, v, l=l, lamv=lamv, sg=sg, lam_init=lam_init):
            rows = t_s * C_HEADS
            qn3 = qn.reshape(n_s, rows, LANES)
            kn3 = kn.reshape(n_s, rows, LANES)
            vn3 = v.reshape(n_s, rows, LANES)
            o = _attn_sample(qn3, kn3, vn3, ck_all, cv_all, page_table, l * n_pool, lamv, sg, sample_tables,
                             lam_init)
            return o.reshape(n_s * t_s, C_WIDTH)

        xp, k_p, v_p, wkv_p, shift_p, conv_p = _mixers(
            xp, b, t_p, jnp.zeros((b, A_COLS), F32), jnp.zeros((b, A_HEADS, A_HEAD_DIM, A_HEAD_DIM), F32),
            jnp.zeros((b, CONV_HIST, B_WIDTH), F32), attend_prompt, lp, consts, lam_init)
        xs, k_s, v_s, wkv_s, shift_s, conv_s = _mixers(
            xs, n_s, t_s, state_shift[l], state_wkv[l], state_conv[l], attend_sample, lp, consts, lam_init)

        i = l // 2
        g2 = norm2_g[l].reshape(1, -1)
        if l % 2 == 0:
            wg = ffn_w_gate[i].astype(BF16)[None]
            wu = ffn_w_up[i].astype(BF16)[None]
            wd = ffn_w_down[i].astype(BF16)[None]
            xp = _ffn(xp, g2, jnp.ones((xp.shape[0], LANES), F32), wg, wu, wd)
            xs = _ffn(xs, g2, jnp.ones((xs.shape[0], LANES), F32), wg, wu, wd)
        else:
            wg = moe_w_gate[i].astype(BF16)
            wu = moe_w_up[i].astype(BF16)
            wd = moe_w_down[i].astype(BF16)
            router_pad = jnp.pad(moe_router[i], ((0, 0), (0, LANES - N_EXPERTS)))
            tw_p, tw_s = _moe_window(xp.shape[0]), _moe_window(xs.shape[0])
            xp = _moe(*_router(xp, g2, router_pad, tw_p), xp, wg, wu, wd, tw_p)
            xs = _moe(*_router(xs, g2, router_pad, tw_s), xs, wg, wu, wd, tw_s)

        for lst, val in zip(outs, (k_p, v_p, k_s, v_s, wkv_p, wkv_s, shift_p, shift_s, conv_p, conv_s)):
            lst.append(val)

    y_prompt = xp.reshape(b, t_p, D_MODEL)[:, N_META:]
    y_sample = xs.reshape(n_s, t_s, D_MODEL)
    return (y_prompt, y_sample) + tuple(jnp.stack(o) for o in outs)
```
